```python
import jax, jax.numpy as jnp
from jax import lax
import numpy as np

D_MODEL = 2048
BATCH = 2
SEQ = 16384
DEPTH = 2

GRID_W = 64
Q_BLOCK = 128
NORM_EPS = 1e-6
ROPE_THETA = 10000.0

HEAD_DIM = 128
ATTN_HEADS = D_MODEL // (2 * HEAD_DIM)
ATTN_KV_HEADS = ATTN_HEADS // 4
ATTN_WIDTH = ATTN_HEADS * HEAD_DIM
KV_WIDTH = ATTN_KV_HEADS * HEAD_DIM
CONV_GROUPS = 8
CONV_WIDTH = D_MODEL // 2
CONV_K = 3
IN0_WIDTH = ATTN_WIDTH + 2 * KV_WIDTH + 3 * CONV_WIDTH
MIX0_WIDTH = ATTN_WIDTH + CONV_WIDTH
MLA_HEADS = D_MODEL // 128
MLA_NOPE = 128
MLA_ROPE = 64
MLA_V = 128
Q_LORA = 512
KV_LORA = 512
MLA_DOWN_WIDTH = Q_LORA + KV_LORA + MLA_ROPE
D_FF = 4 * D_MODEL

N_EVEN = (DEPTH + 1) // 2
N_ODD = DEPTH // 2

kernel_name = 'hybrid_gqa_shortconv_mla_encoder'


def rms_norm(x, g):
    xf = x.astype(jnp.float32)
    y = xf * lax.rsqrt(jnp.mean(xf * xf, axis=-1, keepdims=True) + NORM_EPS)
    return (y * g.astype(jnp.float32)).astype(x.dtype)


def axial_rope_angles(seq, rot_dim):
    rows = seq // GRID_W
    row = jnp.repeat(jnp.arange(rows, dtype=jnp.float32), GRID_W)
    col = jnp.tile(jnp.arange(GRID_W, dtype=jnp.float32), rows)
    axis_dim = rot_dim // 2
    inv_freq = ROPE_THETA ** (-jnp.arange(0, axis_dim, 2, dtype=jnp.float32) / axis_dim)
    ang = jnp.concatenate([row[:, None] * inv_freq, col[:, None] * inv_freq], axis=-1)
    return jnp.cos(ang), jnp.sin(ang)


def apply_rope(x, cos, sin):
    xf = x.astype(jnp.float32).reshape(x.shape[:-1] + (-1, 2))
    x1, x2 = xf[..., 0], xf[..., 1]
    out = jnp.stack([x1 * cos - x2 * sin, x1 * sin + x2 * cos], axis=-1)
    return out.reshape(x.shape).astype(x.dtype)


def gqa_attention(q, k, v):
    b, s, h, dh = q.shape
    kvh = k.shape[2]
    g = h // kvh
    nblk = s // Q_BLOCK
    qb = q.reshape(b, nblk, Q_BLOCK, kvh, g, dh).transpose(1, 0, 2, 3, 4, 5)
    scale = dh ** -0.5

    def block(q_blk):
        sc = jnp.einsum('bqkgd,bskd->bkgqs', q_blk, k).astype(jnp.float32) * scale
        p = jax.nn.softmax(sc, axis=-1).astype(v.dtype)
        return jnp.einsum('bkgqs,bskd->bqkgd', p, v)

    out = lax.map(block, qb)
    return out.transpose(1, 0, 2, 3, 4, 5).reshape(b, s, h * dh)


def mla_attention(q_nope, q_rope, k_nope, k_rope, v):
    b, s, h, dn = q_nope.shape
    dr = q_rope.shape[-1]
    nblk = s // Q_BLOCK
    qn = q_nope.reshape(b, nblk, Q_BLOCK, h, dn).transpose(1, 0, 2, 3, 4)
    qr = q_rope.reshape(b, nblk, Q_BLOCK, h, dr).transpose(1, 0, 2, 3, 4)
    scale = (dn + dr) ** -0.5

    def block(args):
        qn_blk, qr_blk = args
        sc = (jnp.einsum('bqhd,bshd->bhqs', qn_blk, k_nope)
              + jnp.einsum('bqhr,bsr->bhqs', qr_blk, k_rope)).astype(jnp.float32) * scale
        p = jax.nn.softmax(sc, axis=-1).astype(v.dtype)
        return jnp.einsum('bhqs,bshd->bqhd', p, v)

    out = lax.map(block, (qn, qr))
    return out.transpose(1, 0, 2, 3, 4).reshape(b, s, h * v.shape[-1])


def even_mixer(h, w_in, q_norm_g, k_norm_g, conv_w, w_out, cos, sin):
    b, s, _ = h.shape
    proj = h @ w_in
    o1 = ATTN_WIDTH
    o2 = o1 + KV_WIDTH
    o3 = o2 + KV_WIDTH
    o4 = o3 + CONV_WIDTH
    o5 = o4 + CONV_WIDTH
    q, k, v, gate_b, gate_c, u = jnp.split(proj, [o1, o2, o3, o4, o5], axis=-1)
    cos_h, sin_h = cos[:, None, :], sin[:, None, :]
    q = apply_rope(rms_norm(q.reshape(b, s, ATTN_HEADS, HEAD_DIM), q_norm_g), cos_h, sin_h)
    k = apply_rope(rms_norm(k.reshape(b, s, ATTN_KV_HEADS, HEAD_DIM), k_norm_g), cos_h, sin_h)
    v = v.reshape(b, s, ATTN_KV_HEADS, HEAD_DIM)
    attn = gqa_attention(q, k, v)
    conv_in = gate_c * u
    conv = lax.conv_general_dilated(
        conv_in, conv_w[:, None, :].astype(conv_in.dtype), window_strides=(1,), padding='SAME',
        dimension_numbers=('NWC', 'WIO', 'NWC'), feature_group_count=CONV_WIDTH)
    sconv = gate_b * conv
    return jnp.concatenate([attn, sconv], axis=-1) @ w_out


def odd_mixer(h, w_down, q_lat_g, kv_lat_g, w_uq, w_ukv, w_o, cos, sin):
    b, s, _ = h.shape
    lat = h @ w_down
    c_q, c_kv, k_rope = jnp.split(lat, [Q_LORA, Q_LORA + KV_LORA], axis=-1)
    q = (rms_norm(c_q, q_lat_g) @ w_uq).reshape(b, s, MLA_HEADS, MLA_NOPE + MLA_ROPE)
    q_nope, q_rope = q[..., :MLA_NOPE], q[..., MLA_NOPE:]
    q_rope = apply_rope(q_rope, cos[:, None, :], sin[:, None, :])
    k_rope = apply_rope(k_rope, cos, sin)
    kv = (rms_norm(c_kv, kv_lat_g) @ w_ukv).reshape(b, s, MLA_HEADS, MLA_NOPE + MLA_V)
    k_nope, v = kv[..., :MLA_NOPE], kv[..., MLA_NOPE:]
    attn = mla_attention(q_nope, q_rope, k_nope, k_rope, v)
    return attn @ w_o


def squared_relu_mlp(h, w_up, w_dn):
    return jnp.square(jax.nn.relu(h @ w_up)) @ w_dn


def setup_inputs(seed: int = 0) -> dict:
    key = jax.random.key(seed)
    ks = jax.random.split(key, 20)

    def w(k, shape, fan_in):
        return jax.random.normal(k, shape, jnp.float32) * (fan_in ** -0.5)

    def gain(k, shape):
        return 1.0 + 0.02 * jax.random.normal(k, shape, jnp.float32)

    return {
        'x': jax.random.normal(ks[0], (BATCH, SEQ, D_MODEL), jnp.float32),
        'even_norm_g': gain(ks[1], (N_EVEN, D_MODEL)),
        'even_w_in': w(ks[2], (N_EVEN, D_MODEL, IN0_WIDTH), D_MODEL),
        'even_q_norm_g': gain(ks[3], (N_EVEN, HEAD_DIM)),
        'even_k_norm_g': gain(ks[4], (N_EVEN, HEAD_DIM)),
        'even_conv_w': w(ks[5], (N_EVEN, CONV_K, CONV_WIDTH), CONV_K),
        'even_w_out': w(ks[6], (N_EVEN, MIX0_WIDTH, D_MODEL), MIX0_WIDTH),
        'odd_norm_g': gain(ks[7], (N_ODD, D_MODEL)),
        'odd_w_down': w(ks[8], (N_ODD, D_MODEL, MLA_DOWN_WIDTH), D_MODEL),
        'odd_q_lat_g': gain(ks[9], (N_ODD, Q_LORA)),
        'odd_kv_lat_g': gain(ks[10], (N_ODD, KV_LORA)),
        'odd_w_uq': w(ks[11], (N_ODD, Q_LORA, MLA_HEADS * (MLA_NOPE + MLA_ROPE)), Q_LORA),
        'odd_w_ukv': w(ks[12], (N_ODD, KV_LORA, MLA_HEADS * (MLA_NOPE + MLA_V)), KV_LORA),
        'odd_w_o': w(ks[13], (N_ODD, MLA_HEADS * MLA_V, D_MODEL), MLA_HEADS * MLA_V),
        'mlp_norm_g': gain(ks[14], (DEPTH, D_MODEL)),
        'mlp_w_up': w(ks[15], (DEPTH, D_MODEL, D_FF), D_MODEL),
        'mlp_w_down': w(ks[16], (DEPTH, D_FF, D_MODEL), D_FF),
        'final_norm_g': gain(ks[17], (D_MODEL,)),
    }


def reference(x, even_norm_g, even_w_in, even_q_norm_g, even_k_norm_g, even_conv_w, even_w_out,
              odd_norm_g, odd_w_down, odd_q_lat_g, odd_kv_lat_g, odd_w_uq, odd_w_ukv, odd_w_o,
              mlp_norm_g, mlp_w_up, mlp_w_down, final_norm_g):
    s = x.shape[1]
    cos_a, sin_a = axial_rope_angles(s, HEAD_DIM)
    cos_c, sin_c = axial_rope_angles(s, MLA_ROPE)
    h = x
    for layer in range(DEPTH):
        i = layer // 2
        if layer % 2 == 0:
            h = h + even_mixer(rms_norm(h, even_norm_g[i]), even_w_in[i], even_q_norm_g[i],
                               even_k_norm_g[i], even_conv_w[i], even_w_out[i], cos_a, sin_a)
        else:
            h = h + odd_mixer(rms_norm(h, odd_norm_g[i]), odd_w_down[i], odd_q_lat_g[i],
                              odd_kv_lat_g[i], odd_w_uq[i], odd_w_ukv[i], odd_w_o[i], cos_c, sin_c)
        h = h + squared_relu_mlp(rms_norm(h, mlp_norm_g[layer]), mlp_w_up[layer], mlp_w_down[layer])
    return rms_norm(h, final_norm_g)
```

```python
import functools

import jax
import jax.numpy as jnp
from jax import lax
from jax.experimental import pallas as pl
from jax.experimental.pallas import tpu as pltpu

F32 = jnp.float32
BF16 = jnp.bfloat16

NORM_EPS = 1e-6
ROPE_THETA = 10000.0
GRID_W = 64

HEAD_DIM = 128
ATTN_HEADS = 8
ATTN_KV_HEADS = 2
ATTN_GROUP = ATTN_HEADS // ATTN_KV_HEADS
ATTN_WIDTH = ATTN_HEADS * HEAD_DIM
KV_WIDTH = ATTN_KV_HEADS * HEAD_DIM
CONV_WIDTH = 1024

MLA_HEADS = 16
MLA_NOPE = 128
MLA_ROPE = 64
MLA_V = 128
Q_LORA = 512
KV_LORA = 512
MLA_QK_PAD = 256
MLA_DOWN_PAD = Q_LORA + KV_LORA + 128

LANES = 128
BF16_SUBLANES = 16
VMEM_LIMIT_BYTES = 56 * 1024 * 1024
NEG_BIG = -1e30


def _tiles(seq):
    chunk = min(512, seq)
    return dict(
        chunk=chunk,
        bm_mlp=min(1024, seq),
        bf_mlp=512,
        bm_out=min(512, seq),
        bq_gqa=min(512, seq),
        bq_mla=min(1024, seq),
    )


def _params(sem):
    return pltpu.CompilerParams(dimension_semantics=sem, vmem_limit_bytes=VMEM_LIMIT_BYTES)


def _rms(x, g):
    return x * lax.rsqrt(jnp.mean(x * x, axis=-1, keepdims=True) + NORM_EPS) * g


def _rope(x, cos, sin):
    return x * cos + pltpu.roll(x, LANES // 2, 1) * sin


def _proj0_kernel(h_ref, g_ref, w_ref, qg_ref, kg_ref, cos_ref, sin_ref,
                  q_ref, k_ref, vt_ref, gb_ref, cin_ref):
    xn = _rms(h_ref[...], g_ref[...]).astype(BF16)
    y = jnp.dot(xn, w_ref[...], preferred_element_type=F32)
    cos = cos_ref[...]
    sin = sin_ref[...]
    scale = HEAD_DIM ** -0.5
    for hh in range(ATTN_HEADS):
        yh = _rms(y[:, hh * HEAD_DIM:(hh + 1) * HEAD_DIM], qg_ref[...])
        q_ref[hh] = (_rope(yh, cos, sin) * scale).astype(BF16)
    o = ATTN_WIDTH
    for hh in range(ATTN_KV_HEADS):
        yh = _rms(y[:, o + hh * HEAD_DIM:o + (hh + 1) * HEAD_DIM], kg_ref[...])
        k_ref[hh] = _rope(yh, cos, sin).astype(BF16)
    o += KV_WIDTH
    for hh in range(ATTN_KV_HEADS):
        vt_ref[hh] = y[:, o + hh * HEAD_DIM:o + (hh + 1) * HEAD_DIM].T.astype(BF16)
    o += KV_WIDTH
    gb_ref[...] = y[:, o:o + CONV_WIDTH].astype(BF16)
    o += CONV_WIDTH
    cin_ref[...] = (y[:, o:o + CONV_WIDTH] * y[:, o + CONV_WIDTH:o + 2 * CONV_WIDTH]).astype(BF16)


def _proj0(h, g, w, qg, kg, cos, sin, chunk):
    b, s, d = h.shape
    n = w.shape[1]
    nc = s // chunk
    return pl.pallas_call(
        _proj0_kernel,
        grid=(b, nc),
        in_specs=[
            pl.BlockSpec((None, chunk, d), lambda bi, i: (bi, i, 0)),
            pl.BlockSpec((1, d), lambda bi, i: (0, 0)),
            pl.BlockSpec((d, n), lambda bi, i: (0, 0)),
            pl.BlockSpec((1, HEAD_DIM), lambda bi, i: (0, 0)),
            pl.BlockSpec((1, HEAD_DIM), lambda bi, i: (0, 0)),
            pl.BlockSpec((chunk, HEAD_DIM), lambda bi, i: (i, 0)),
            pl.BlockSpec((chunk, HEAD_DIM), lambda bi, i: (i, 0)),
        ],
        out_specs=[
            pl.BlockSpec((None, ATTN_HEADS, chunk, HEAD_DIM), lambda bi, i: (bi, 0, i, 0)),
            pl.BlockSpec((None, ATTN_KV_HEADS, chunk, HEAD_DIM), lambda bi, i: (bi, 0, i, 0)),
            pl.BlockSpec((None, ATTN_KV_HEADS, None, HEAD_DIM, chunk), lambda bi, i: (bi, 0, i, 0, 0)),
            pl.BlockSpec((None, chunk, CONV_WIDTH), lambda bi, i: (bi, i, 0)),
            pl.BlockSpec((None, chunk, CONV_WIDTH), lambda bi, i: (bi, i, 0)),
        ],
        out_shape=[
            jax.ShapeDtypeStruct((b, ATTN_HEADS, s, HEAD_DIM), BF16),
            jax.ShapeDtypeStruct((b, ATTN_KV_HEADS, s, HEAD_DIM), BF16),
            jax.ShapeDtypeStruct((b, ATTN_KV_HEADS, nc, HEAD_DIM, chunk), BF16),
            jax.ShapeDtypeStruct((b, s, CONV_WIDTH), BF16),
            jax.ShapeDtypeStruct((b, s, CONV_WIDTH), BF16),
        ],
        compiler_params=_params(("parallel", "parallel")),
        name="proj0",
    )(h, g, w, qg, kg, cos, sin)


def _proj1_kernel(h_ref, g_ref, wd_ref, qg_ref, kvg_ref, wuq_ref, wukv_ref, cos_ref, sin_ref,
                  q_ref, k_ref, vt_ref):
    xn = _rms(h_ref[...], g_ref[...]).astype(BF16)
    lat = jnp.dot(xn, wd_ref[...], preferred_element_type=F32)
    cq = _rms(lat[:, :Q_LORA], qg_ref[...]).astype(BF16)
    ckv = _rms(lat[:, Q_LORA:Q_LORA + KV_LORA], kvg_ref[...]).astype(BF16)
    cos = cos_ref[...]
    sin = sin_ref[...]
    kr = _rope(lat[:, Q_LORA + KV_LORA:], cos, sin).astype(BF16)
    q = jnp.dot(cq, wuq_ref[...], preferred_element_type=F32)
    kv = jnp.dot(ckv, wukv_ref[...], preferred_element_type=F32)
    scale = (MLA_NOPE + MLA_ROPE) ** -0.5
    for hh in range(MLA_HEADS):
        o = hh * MLA_QK_PAD
        q_ref[hh, :, :MLA_NOPE] = (q[:, o:o + MLA_NOPE] * scale).astype(BF16)
        qr = _rope(q[:, o + MLA_NOPE:o + MLA_QK_PAD], cos, sin)
        q_ref[hh, :, MLA_NOPE:] = (qr * scale).astype(BF16)
        o = hh * (MLA_NOPE + MLA_V)
        k_ref[hh, :, :MLA_NOPE] = kv[:, o:o + MLA_NOPE].astype(BF16)
        k_ref[hh, :, MLA_NOPE:] = kr
        vt_ref[hh] = kv[:, o + MLA_NOPE:o + MLA_NOPE + MLA_V].T.astype(BF16)


def _proj1(h, g, wd, qg, kvg, wuq, wukv, cos, sin, chunk):
    b, s, d = h.shape
    nc = s // chunk
    const = lambda bi, i: (0, 0)
    return pl.pallas_call(
        _proj1_kernel,
        grid=(b, nc),
        in_specs=[
            pl.BlockSpec((None, chunk, d), lambda bi, i: (bi, i, 0)),
            pl.BlockSpec((1, d), const),
            pl.BlockSpec(wd.shape, const),
            pl.BlockSpec((1, Q_LORA), const),
            pl.BlockSpec((1, KV_LORA), const),
            pl.BlockSpec(wuq.shape, const),
            pl.BlockSpec(wukv.shape, const),
            pl.BlockSpec((chunk, LANES), lambda bi, i: (i, 0)),
            pl.BlockSpec((chunk, LANES), lambda bi, i: (i, 0)),
        ],
        out_specs=[
            pl.BlockSpec((None, MLA_HEADS, chunk, MLA_QK_PAD), lambda bi, i: (bi, 0, i, 0)),
            pl.BlockSpec((None, MLA_HEADS, chunk, MLA_QK_PAD), lambda bi, i: (bi, 0, i, 0)),
            pl.BlockSpec((None, MLA_HEADS, None, MLA_V, chunk), lambda bi, i: (bi, 0, i, 0, 0)),
        ],
        out_shape=[
            jax.ShapeDtypeStruct((b, MLA_HEADS, s, MLA_QK_PAD), BF16),
            jax.ShapeDtypeStruct((b, MLA_HEADS, s, MLA_QK_PAD), BF16),
            jax.ShapeDtypeStruct((b, MLA_HEADS, nc, MLA_V, chunk), BF16),
        ],
        compiler_params=_params(("parallel", "parallel")),
        name="proj1",
    )(h, g, wd, qg, kvg, wuq, wukv, cos, sin)


def _attn_kernel(q_ref, k_ref, vt_ref, o_ref, m_sc, l_sc, acc_sc, *, group, bq, chunk, n_chunks):
    dqk = q_ref.shape[-1]
    q2 = q_ref[...].reshape(group * bq, dqk)
    m_sc[...] = jnp.full(m_sc.shape, NEG_BIG, F32)
    l_sc[...] = jnp.zeros(l_sc.shape, F32)
    acc_sc[...] = jnp.zeros(acc_sc.shape, F32)

    def body(c, carry):
        kc = k_ref[pl.ds(pl.multiple_of(c * chunk, chunk), chunk), :]
        s = lax.dot_general(kc, q2, (((1,), (1,)), ((), ())), preferred_element_type=F32)
        m_prev = m_sc[...]
        m_new = jnp.maximum(m_prev, jnp.max(s, axis=0, keepdims=True))
        alpha = jnp.exp(m_prev - m_new)
        p = jnp.exp(s - m_new)
        l_sc[...] = alpha * l_sc[...] + jnp.sum(p, axis=0, keepdims=True)
        pv = jnp.dot(vt_ref[c], p.astype(BF16), preferred_element_type=F32)
        acc_sc[...] = alpha * acc_sc[...] + pv
        m_sc[...] = m_new
        return carry

    lax.fori_loop(0, n_chunks, body, 0)
    o = acc_sc[...] / l_sc[...]
    dv = acc_sc.shape[0]
    for gi in range(group):
        o_ref[:, gi * dv:(gi + 1) * dv] = o[:, gi * bq:(gi + 1) * bq].T.astype(o_ref.dtype)


def _attention(q, k, vt, *, group, bq, name):
    b, h, s, dqk = q.shape
    hkv = k.shape[1]
    n_chunks, dv, chunk = vt.shape[2:]
    n = group * bq
    kern = functools.partial(_attn_kernel, group=group, bq=bq, chunk=chunk, n_chunks=n_chunks)
    return pl.pallas_call(
        kern,
        grid=(b, hkv, s // bq),
        in_specs=[
            pl.BlockSpec((None, group, bq, dqk), lambda bi, hi, qi: (bi, hi, qi, 0)),
            pl.BlockSpec((None, None, s, dqk), lambda bi, hi, qi: (bi, hi, 0, 0)),
            pl.BlockSpec((None, None, n_chunks, dv, chunk), lambda bi, hi, qi: (bi, hi, 0, 0, 0)),
        ],
        out_specs=pl.BlockSpec((None, bq, group * dv), lambda bi, hi, qi: (bi, qi, hi)),
        out_shape=jax.ShapeDtypeStruct((b, s, h * dv), BF16),
        scratch_shapes=[
            pltpu.VMEM((1, n), F32),
            pltpu.VMEM((1, n), F32),
            pltpu.VMEM((dv, n), F32),
        ],
        compiler_params=_params(("parallel", "parallel", "arbitrary")),
        name=name,
    )(q, k, vt)


def _out0_kernel(h_ref, a_ref, gb_ref, cin_ref, cprev_ref, cnext_ref, cw_ref, w_ref, o_ref):
    i = pl.program_id(1)
    bm = cin_ref.shape[0]
    c = cin_ref[...].astype(F32)
    prev_row = cprev_ref[BF16_SUBLANES - 1:BF16_SUBLANES, :].astype(F32)
    next_row = cnext_ref[0:1, :].astype(F32)
    prev_row = jnp.where(i == 0, 0.0, prev_row)
    next_row = jnp.where(i == pl.num_programs(1) - 1, 0.0, next_row)
    rows = lax.broadcasted_iota(jnp.int32, (bm, 1), 0)
    c_m1 = jnp.where(rows == 0, prev_row, pltpu.roll(c, 1, 0))
    c_p1 = jnp.where(rows == bm - 1, next_row, pltpu.roll(c, bm - 1, 0))
    conv = cw_ref[0:1, :] * c_m1 + cw_ref[1:2, :] * c + cw_ref[2:3, :] * c_p1
    sconv = (gb_ref[...].astype(F32) * conv).astype(BF16)
    aw = a_ref.shape[1]
    y = jnp.dot(a_ref[...], w_ref[:aw, :], preferred_element_type=F32)
    y = y + jnp.dot(sconv, w_ref[aw:, :], preferred_element_type=F32)
    o_ref[...] = h_ref[...] + y


def _out0(h, attn, gb, cin, conv_w, w, bm):
    b, s, d = h.shape
    nb = s // bm
    r = bm // BF16_SUBLANES
    last = s // BF16_SUBLANES - 1
    tile = lambda width: pl.BlockSpec((None, bm, width), lambda bi, i: (bi, i, 0))
    return pl.pallas_call(
        _out0_kernel,
        grid=(b, nb),
        in_specs=[
            tile(d), tile(attn.shape[2]), tile(CONV_WIDTH), tile(CONV_WIDTH),
            pl.BlockSpec((None, BF16_SUBLANES, CONV_WIDTH),
                         lambda bi, i: (bi, jnp.maximum(i * r - 1, 0), 0)),
            pl.BlockSpec((None, BF16_SUBLANES, CONV_WIDTH),
                         lambda bi, i: (bi, jnp.minimum((i + 1) * r, last), 0)),
            pl.BlockSpec(conv_w.shape, lambda bi, i: (0, 0)),
            pl.BlockSpec(w.shape, lambda bi, i: (0, 0)),
        ],
        out_specs=tile(d),
        out_shape=jax.ShapeDtypeStruct(h.shape, F32),
        compiler_params=_params(("parallel", "parallel")),
        name="out0",
    )(h, attn, gb, cin, cin, cin, conv_w, w)


def _out1_kernel(h_ref, a_ref, w_ref, o_ref):
    o_ref[...] = h_ref[...] + jnp.dot(a_ref[...], w_ref[...], preferred_element_type=F32)


def _out1(h, attn, w, bm):
    t, d = h.shape
    return pl.pallas_call(
        _out1_kernel,
        grid=(t // bm,),
        in_specs=[
            pl.BlockSpec((bm, d), lambda i: (i, 0)),
            pl.BlockSpec((bm, attn.shape[1]), lambda i: (i, 0)),
            pl.BlockSpec(w.shape, lambda i: (0, 0)),
        ],
        out_specs=pl.BlockSpec((bm, d), lambda i: (i, 0)),
        out_shape=jax.ShapeDtypeStruct(h.shape, F32),
        compiler_params=_params(("parallel",)),
        name="out1",
    )(h, attn, w)


def _mlp_kernel(h_ref, g_ref, wup_ref, wdn_ref, fg_ref, o_ref, xn_sc, *, final_norm):
    f = pl.program_id(1)

    @pl.when(f == 0)
    def _():
        x = h_ref[...]
        xn_sc[...] = _rms(x, g_ref[...]).astype(BF16)
        o_ref[...] = x

    u = jnp.dot(xn_sc[...], wup_ref[...], preferred_element_type=F32)
    a = jnp.square(jnp.maximum(u, 0.0)).astype(BF16)
    o_ref[...] += jnp.dot(a, wdn_ref[...], preferred_element_type=F32)

    if final_norm:
        @pl.when(f == pl.num_programs(1) - 1)
        def _():
            o_ref[...] = _rms(o_ref[...], fg_ref[...])


def _mlp(h, g, wup, wdn, fg, *, bm, bf, final_norm):
    t, d = h.shape
    dff = wup.shape[1]
    return pl.pallas_call(
        functools.partial(_mlp_kernel, final_norm=final_norm),
        grid=(t // bm, dff // bf),
        in_specs=[
            pl.BlockSpec((bm, d), lambda i, f: (i, 0)),
            pl.BlockSpec((1, d), lambda i, f: (0, 0)),
            pl.BlockSpec((d, bf), lambda i, f: (0, f)),
            pl.BlockSpec((bf, d), lambda i, f: (f, 0)),
            pl.BlockSpec((1, d), lambda i, f: (0, 0)),
        ],
        out_specs=pl.BlockSpec((bm, d), lambda i, f: (i, 0)),
        out_shape=jax.ShapeDtypeStruct(h.shape, F32),
        scratch_shapes=[pltpu.VMEM((bm, d), BF16)],
        compiler_params=_params(("parallel", "arbitrary")),
        name="mlp_final" if final_norm else "mlp",
    )(h, g, wup, wdn, fg)


def _rope_angles(seq, rot_dim):
    rows = seq // GRID_W
    row = jnp.repeat(jnp.arange(rows, dtype=F32), GRID_W)
    col = jnp.tile(jnp.arange(GRID_W, dtype=F32), rows)
    axis_dim = rot_dim // 2
    inv_freq = ROPE_THETA ** (-jnp.arange(0, axis_dim, 2, dtype=F32) / axis_dim)
    return jnp.concatenate([row[:, None] * inv_freq, col[:, None] * inv_freq], axis=-1)


def _rope_tables(seq, rot_dim):
    ang = _rope_angles(seq, rot_dim)
    cos, sin = jnp.cos(ang), jnp.sin(ang)
    pad = jnp.zeros((seq, LANES // 2 - rot_dim // 2), F32)
    cos_t = jnp.concatenate([cos, pad, cos, pad], axis=-1)
    sin_t = jnp.concatenate([-sin, pad, sin, pad], axis=-1)
    return cos_t, sin_t


def _pair_split(w):
    return w[..., 0::2], w[..., 1::2]


def kernel(x, even_norm_g, even_w_in, even_q_norm_g, even_k_norm_g, even_conv_w, even_w_out,
           odd_norm_g, odd_w_down, odd_q_lat_g, odd_kv_lat_g, odd_w_uq, odd_w_ukv, odd_w_o,
           mlp_norm_g, mlp_w_up, mlp_w_down, final_norm_g):
    b, s, d = x.shape
    t = _tiles(s)
    depth = mlp_norm_g.shape[0]
    cos_a, sin_a = _rope_tables(s, HEAD_DIM)
    cos_c, sin_c = _rope_tables(s, MLA_ROPE)
    fg = final_norm_g.reshape(1, d)

    h = x
    for layer in range(depth):
        i = layer // 2
        if layer % 2 == 0:
            w_in = even_w_in[i]
            wq = w_in[:, :ATTN_WIDTH].reshape(d, ATTN_HEADS, HEAD_DIM)
            wk = w_in[:, ATTN_WIDTH:ATTN_WIDTH + KV_WIDTH].reshape(d, ATTN_KV_HEADS, HEAD_DIM)
            wq = jnp.concatenate(_pair_split(wq), axis=-1).reshape(d, ATTN_WIDTH)
            wk = jnp.concatenate(_pair_split(wk), axis=-1).reshape(d, KV_WIDTH)
            w0 = jnp.concatenate([wq, wk, w_in[:, ATTN_WIDTH + KV_WIDTH:]], axis=-1).astype(BF16)
            qg = jnp.concatenate(_pair_split(even_q_norm_g[i]), axis=-1).reshape(1, HEAD_DIM)
            kg = jnp.concatenate(_pair_split(even_k_norm_g[i]), axis=-1).reshape(1, HEAD_DIM)
            q, k, vt, gb, cin = _proj0(h, even_norm_g[i].reshape(1, d), w0, qg, kg,
                                       cos_a, sin_a, t["chunk"])
            attn = _attention(q, k, vt, group=ATTN_GROUP, bq=t["bq_gqa"], name="gqa_attn")
            h = _out0(h, attn, gb, cin, even_conv_w[i], even_w_out[i].astype(BF16), t["bm_out"])
        else:
            w_down = odd_w_down[i]
            kr_e, kr_o = _pair_split(w_down[:, Q_LORA + KV_LORA:])
            zpad = jnp.zeros((d, LANES // 2 - MLA_ROPE // 2), F32)
            wd = jnp.concatenate([w_down[:, :Q_LORA + KV_LORA], kr_e, zpad, kr_o, zpad],
                                 axis=-1).astype(BF16)
            wuq = odd_w_uq[i].reshape(Q_LORA, MLA_HEADS, MLA_NOPE + MLA_ROPE)
            qr_e, qr_o = _pair_split(wuq[..., MLA_NOPE:])
            zq = jnp.zeros((Q_LORA, MLA_HEADS, LANES // 2 - MLA_ROPE // 2), F32)
            wuq = jnp.concatenate([wuq[..., :MLA_NOPE], qr_e, zq, qr_o, zq], axis=-1)
            wuq = wuq.reshape(Q_LORA, MLA_HEADS * MLA_QK_PAD).astype(BF16)
            q, k, vt = _proj1(h, odd_norm_g[i].reshape(1, d), wd,
                              odd_q_lat_g[i].reshape(1, Q_LORA), odd_kv_lat_g[i].reshape(1, KV_LORA),
                              wuq, odd_w_ukv[i].astype(BF16), cos_c, sin_c, t["chunk"])
            attn = _attention(q, k, vt, group=1, bq=t["bq_mla"], name="mla_attn")
            h = _out1(h.reshape(b * s, d), attn.reshape(b * s, -1), odd_w_o[i].astype(BF16),
                      t["bm_out"]).reshape(b, s, d)
        h = _mlp(h.reshape(b * s, d), mlp_norm_g[layer].reshape(1, d),
                 mlp_w_up[layer].astype(BF16), mlp_w_down[layer].astype(BF16), fg,
                 bm=t["bm_mlp"], bf=t["bf_mlp"],
                 final_norm=(layer == depth - 1)).reshape(b, s, d)
    if depth == 0:
        h = _rms(h, final_norm_g)
    return h
```

```python
import functools

import jax
import jax.numpy as jnp
from jax import lax
from jax.experimental import pallas as pl
from jax.experimental.pallas import tpu as pltpu

F32 = jnp.float32
BF16 = jnp.bfloat16

NORM_EPS = 1e-6
ROPE_THETA = 10000.0
GRID_W = 64

HEAD_DIM = 128
ATTN_HEADS = 8
ATTN_KV_HEADS = 2
ATTN_GROUP = ATTN_HEADS // ATTN_KV_HEADS
ATTN_WIDTH = ATTN_HEADS * HEAD_DIM
KV_WIDTH = ATTN_KV_HEADS * HEAD_DIM
CONV_WIDTH = 1024

MLA_HEADS = 16
MLA_NOPE = 128
MLA_ROPE = 64
MLA_V = 128
Q_LORA = 512
KV_LORA = 512
MLA_QK_PAD = 256
MLA_DOWN_PAD = Q_LORA + KV_LORA + 128

LANES = 128
BF16_SUBLANES = 16
VMEM_LIMIT_BYTES = 56 * 1024 * 1024
NEG_BIG = -1e30
LOG2_E = 1.4426950408889634
ATTN_COL_GROUP = 512
ATTN_TRIP_ITEMS = 8
ATTN_LOOKAHEAD = 2
ATTN_RING = 4


def _tiles(seq):
    chunk = min(512, seq)
    return dict(
        chunk=chunk,
        bm_mlp=min(1024, seq),
        bf_mlp=512,
        bm_out=min(512, seq),
        bq_gqa=min(512, seq),
        bq_mla=min(1024, seq),
    )


def _params(sem):
    return pltpu.CompilerParams(dimension_semantics=sem, vmem_limit_bytes=VMEM_LIMIT_BYTES)


def _rms(x, g):
    return x * lax.rsqrt(jnp.mean(x * x, axis=-1, keepdims=True) + NORM_EPS) * g


def _rope(x, cos, sin):
    return x * cos + pltpu.roll(x, LANES // 2, 1) * sin


def _proj0_kernel(h_ref, g_ref, w_ref, qg_ref, kg_ref, cos_ref, sin_ref,
                  q_ref, k_ref, vt_ref, gb_ref, cin_ref):
    xn = _rms(h_ref[...], g_ref[...]).astype(BF16)
    y = jnp.dot(xn, w_ref[...], preferred_element_type=F32)
    cos = cos_ref[...]
    sin = sin_ref[...]
    scale = HEAD_DIM ** -0.5 * LOG2_E
    for hh in range(ATTN_HEADS):
        yh = _rms(y[:, hh * HEAD_DIM:(hh + 1) * HEAD_DIM], qg_ref[...])
        q_ref[hh] = (_rope(yh, cos, sin) * scale).astype(BF16)
    o = ATTN_WIDTH
    for hh in range(ATTN_KV_HEADS):
        yh = _rms(y[:, o + hh * HEAD_DIM:o + (hh + 1) * HEAD_DIM], kg_ref[...])
        k_ref[hh] = _rope(yh, cos, sin).astype(BF16)
    o += KV_WIDTH
    for hh in range(ATTN_KV_HEADS):
        vt_ref[hh] = y[:, o + hh * HEAD_DIM:o + (hh + 1) * HEAD_DIM].T.astype(BF16)
    o += KV_WIDTH
    gb_ref[...] = y[:, o:o + CONV_WIDTH].astype(BF16)
    o += CONV_WIDTH
    cin_ref[...] = (y[:, o:o + CONV_WIDTH] * y[:, o + CONV_WIDTH:o + 2 * CONV_WIDTH]).astype(BF16)


def _proj0(h, g, w, qg, kg, cos, sin, chunk):
    b, s, d = h.shape
    n = w.shape[1]
    nc = s // chunk
    return pl.pallas_call(
        _proj0_kernel,
        grid=(b, nc),
        in_specs=[
            pl.BlockSpec((None, chunk, d), lambda bi, i: (bi, i, 0)),
            pl.BlockSpec((1, d), lambda bi, i: (0, 0)),
            pl.BlockSpec((d, n), lambda bi, i: (0, 0)),
            pl.BlockSpec((1, HEAD_DIM), lambda bi, i: (0, 0)),
            pl.BlockSpec((1, HEAD_DIM), lambda bi, i: (0, 0)),
            pl.BlockSpec((chunk, HEAD_DIM), lambda bi, i: (i, 0)),
            pl.BlockSpec((chunk, HEAD_DIM), lambda bi, i: (i, 0)),
        ],
        out_specs=[
            pl.BlockSpec((None, ATTN_HEADS, chunk, HEAD_DIM), lambda bi, i: (bi, 0, i, 0)),
            pl.BlockSpec((None, ATTN_KV_HEADS, chunk, HEAD_DIM), lambda bi, i: (bi, 0, i, 0)),
            pl.BlockSpec((None, ATTN_KV_HEADS, None, HEAD_DIM, chunk), lambda bi, i: (bi, 0, i, 0, 0)),
            pl.BlockSpec((None, chunk, CONV_WIDTH), lambda bi, i: (bi, i, 0)),
            pl.BlockSpec((None, chunk, CONV_WIDTH), lambda bi, i: (bi, i, 0)),
        ],
        out_shape=[
            jax.ShapeDtypeStruct((b, ATTN_HEADS, s, HEAD_DIM), BF16),
            jax.ShapeDtypeStruct((b, ATTN_KV_HEADS, s, HEAD_DIM), BF16),
            jax.ShapeDtypeStruct((b, ATTN_KV_HEADS, nc, HEAD_DIM, chunk), BF16),
            jax.ShapeDtypeStruct((b, s, CONV_WIDTH), BF16),
            jax.ShapeDtypeStruct((b, s, CONV_WIDTH), BF16),
        ],
        compiler_params=_params(("parallel", "parallel")),
        name="proj0",
    )(h, g, w, qg, kg, cos, sin)


def _proj1_kernel(h_ref, g_ref, wd_ref, qg_ref, kvg_ref, wuq_ref, wukv_ref, cos_ref, sin_ref,
                  q_ref, k_ref, vt_ref):
    xn = _rms(h_ref[...], g_ref[...]).astype(BF16)
    lat = jnp.dot(xn, wd_ref[...], preferred_element_type=F32)
    cq = _rms(lat[:, :Q_LORA], qg_ref[...]).astype(BF16)
    ckv = _rms(lat[:, Q_LORA:Q_LORA + KV_LORA], kvg_ref[...]).astype(BF16)
    cos = cos_ref[...]
    sin = sin_ref[...]
    kr = _rope(lat[:, Q_LORA + KV_LORA:], cos, sin).astype(BF16)
    q = jnp.dot(cq, wuq_ref[...], preferred_element_type=F32)
    kv = jnp.dot(ckv, wukv_ref[...], preferred_element_type=F32)
    scale = (MLA_NOPE + MLA_ROPE) ** -0.5 * LOG2_E
    for hh in range(MLA_HEADS):
        o = hh * MLA_QK_PAD
        q_ref[hh, :, :MLA_NOPE] = (q[:, o:o + MLA_NOPE] * scale).astype(BF16)
        qr = _rope(q[:, o + MLA_NOPE:o + MLA_QK_PAD], cos, sin)
        q_ref[hh, :, MLA_NOPE:] = (qr * scale).astype(BF16)
        o = hh * (MLA_NOPE + MLA_V)
        k_ref[hh, :, :MLA_NOPE] = kv[:, o:o + MLA_NOPE].astype(BF16)
        k_ref[hh, :, MLA_NOPE:] = kr
        vt_ref[hh] = kv[:, o + MLA_NOPE:o + MLA_NOPE + MLA_V].T.astype(BF16)


def _proj1(h, g, wd, qg, kvg, wuq, wukv, cos, sin, chunk):
    b, s, d = h.shape
    nc = s // chunk
    const = lambda bi, i: (0, 0)
    return pl.pallas_call(
        _proj1_kernel,
        grid=(b, nc),
        in_specs=[
            pl.BlockSpec((None, chunk, d), lambda bi, i: (bi, i, 0)),
            pl.BlockSpec((1, d), const),
            pl.BlockSpec(wd.shape, const),
            pl.BlockSpec((1, Q_LORA), const),
            pl.BlockSpec((1, KV_LORA), const),
            pl.BlockSpec(wuq.shape, const),
            pl.BlockSpec(wukv.shape, const),
            pl.BlockSpec((chunk, LANES), lambda bi, i: (i, 0)),
            pl.BlockSpec((chunk, LANES), lambda bi, i: (i, 0)),
        ],
        out_specs=[
            pl.BlockSpec((None, MLA_HEADS, chunk, MLA_QK_PAD), lambda bi, i: (bi, 0, i, 0)),
            pl.BlockSpec((None, MLA_HEADS, chunk, MLA_QK_PAD), lambda bi, i: (bi, 0, i, 0)),
            pl.BlockSpec((None, MLA_HEADS, None, MLA_V, chunk), lambda bi, i: (bi, 0, i, 0, 0)),
        ],
        out_shape=[
            jax.ShapeDtypeStruct((b, MLA_HEADS, s, MLA_QK_PAD), BF16),
            jax.ShapeDtypeStruct((b, MLA_HEADS, s, MLA_QK_PAD), BF16),
            jax.ShapeDtypeStruct((b, MLA_HEADS, nc, MLA_V, chunk), BF16),
        ],
        compiler_params=_params(("parallel", "parallel")),
        name="proj1",
    )(h, g, wd, qg, kvg, wuq, wukv, cos, sin)


def _attn_kernel(q_ref, k_ref, vt_ref, o_ref, m_sc, l_sc, acc_sc, s_sc, mx_sc, *, group, bq, chunk,
                 n_chunks, unroll):
    n = group * bq
    gw = s_sc.shape[2]
    n_groups = n // gw
    ring = s_sc.shape[0]
    m_sc[...] = jnp.full(m_sc.shape, NEG_BIG, F32)
    l_sc[...] = jnp.zeros(l_sc.shape, F32)
    acc_sc[...] = jnp.zeros(acc_sc.shape, F32)

    items = [(j, gi) for j in range(unroll) for gi in range(n_groups)]
    assert len(items) % ring == 0
    lookahead = min(ATTN_LOOKAHEAD, ring - 1)

    def scores(c, gi, slot):
        kc = k_ref[pl.ds(pl.multiple_of(c * chunk, chunk), chunk), :]
        qg = q_ref[(gi * gw) // bq, pl.ds((gi * gw) % bq, gw), :]
        s = lax.dot_general(kc, qg, (((1,), (1,)), ((), ())), preferred_element_type=F32)
        s_sc[slot] = s
        mx_sc[slot] = jnp.max(s, axis=0, keepdims=True)

    for idx in range(lookahead):
        scores(items[idx][0], items[idx][1], idx)

    def body(t, carry):
        for idx, (j, gi) in enumerate(items):
            ahead = idx + lookahead
            ja, ga = items[ahead % len(items)]
            ca = jnp.minimum((t + ahead // len(items)) * unroll + ja, n_chunks - 1)
            scores(ca, ga, ahead % ring)

            cols = slice(gi * gw, (gi + 1) * gw)
            slot = idx % ring
            m_prev = m_sc[:, cols]
            m_new = jnp.maximum(m_prev, mx_sc[slot])
            alpha = jnp.exp2(m_prev - m_new)
            p = jnp.exp2(s_sc[slot] - m_new)
            l_sc[:, cols] = alpha * l_sc[:, cols] + jnp.sum(p, axis=0, keepdims=True)
            pv = jnp.dot(vt_ref[t * unroll + j], p.astype(BF16), preferred_element_type=F32)
            acc_sc[:, cols] = alpha * acc_sc[:, cols] + pv
            m_sc[:, cols] = m_new
        return carry

    lax.fori_loop(0, n_chunks // unroll, body, 0)
    o = acc_sc[...] / l_sc[...]
    dv = acc_sc.shape[0]
    for gi in range(group):
        o_ref[:, gi * dv:(gi + 1) * dv] = o[:, gi * bq:(gi + 1) * bq].T.astype(o_ref.dtype)


def _attention(q, k, vt, *, group, bq, name):
    b, h, s, dqk = q.shape
    hkv = k.shape[1]
    n_chunks, dv, chunk = vt.shape[2:]
    n = group * bq
    gw = min(ATTN_COL_GROUP, bq)
    unroll = max(1, ATTN_TRIP_ITEMS // (n // gw))
    while n_chunks % unroll:
        unroll -= 1
    items = unroll * (n // gw)
    ring = ATTN_RING if items % ATTN_RING == 0 else items
    kern = functools.partial(_attn_kernel, group=group, bq=bq, chunk=chunk, n_chunks=n_chunks,
                             unroll=unroll)
    return pl.pallas_call(
        kern,
        grid=(b, hkv, s // bq),
        in_specs=[
            pl.BlockSpec((None, group, bq, dqk), lambda bi, hi, qi: (bi, hi, qi, 0)),
            pl.BlockSpec((None, None, s, dqk), lambda bi, hi, qi: (bi, hi, 0, 0)),
            pl.BlockSpec((None, None, n_chunks, dv, chunk), lambda bi, hi, qi: (bi, hi, 0, 0, 0)),
        ],
        out_specs=pl.BlockSpec((None, bq, group * dv), lambda bi, hi, qi: (bi, qi, hi)),
        out_shape=jax.ShapeDtypeStruct((b, s, h * dv), BF16),
        scratch_shapes=[
            pltpu.VMEM((1, n), F32),
            pltpu.VMEM((1, n), F32),
            pltpu.VMEM((dv, n), F32),
            pltpu.VMEM((ring, chunk, gw), F32),
            pltpu.VMEM((ring, 1, gw), F32),
        ],
        compiler_params=_params(("parallel", "parallel", "arbitrary")),
        name=name,
    )(q, k, vt)


def _out0_kernel(h_ref, a_ref, gb_ref, cin_ref, cprev_ref, cnext_ref, cw_ref, w_ref, o_ref):
    i = pl.program_id(1)
    bm = cin_ref.shape[0]
    c = cin_ref[...].astype(F32)
    prev_row = cprev_ref[BF16_SUBLANES - 1:BF16_SUBLANES, :].astype(F32)
    next_row = cnext_ref[0:1, :].astype(F32)
    prev_row = jnp.where(i == 0, 0.0, prev_row)
    next_row = jnp.where(i == pl.num_programs(1) - 1, 0.0, next_row)
    rows = lax.broadcasted_iota(jnp.int32, (bm, 1), 0)
    c_m1 = jnp.where(rows == 0, prev_row, pltpu.roll(c, 1, 0))
    c_p1 = jnp.where(rows == bm - 1, next_row, pltpu.roll(c, bm - 1, 0))
    conv = cw_ref[0:1, :] * c_m1 + cw_ref[1:2, :] * c + cw_ref[2:3, :] * c_p1
    sconv = (gb_ref[...].astype(F32) * conv).astype(BF16)
    aw = a_ref.shape[1]
    y = jnp.dot(a_ref[...], w_ref[:aw, :], preferred_element_type=F32)
    y = y + jnp.dot(sconv, w_ref[aw:, :], preferred_element_type=F32)
    o_ref[...] = h_ref[...] + y


def _out0(h, attn, gb, cin, conv_w, w, bm):
    b, s, d = h.shape
    nb = s // bm
    r = bm // BF16_SUBLANES
    last = s // BF16_SUBLANES - 1
    tile = lambda width: pl.BlockSpec((None, bm, width), lambda bi, i: (bi, i, 0))
    return pl.pallas_call(
        _out0_kernel,
        grid=(b, nb),
        in_specs=[
            tile(d), tile(attn.shape[2]), tile(CONV_WIDTH), tile(CONV_WIDTH),
            pl.BlockSpec((None, BF16_SUBLANES, CONV_WIDTH),
                         lambda bi, i: (bi, jnp.maximum(i * r - 1, 0), 0)),
            pl.BlockSpec((None, BF16_SUBLANES, CONV_WIDTH),
                         lambda bi, i: (bi, jnp.minimum((i + 1) * r, last), 0)),
            pl.BlockSpec(conv_w.shape, lambda bi, i: (0, 0)),
            pl.BlockSpec(w.shape, lambda bi, i: (0, 0)),
        ],
        out_specs=tile(d),
        out_shape=jax.ShapeDtypeStruct(h.shape, F32),
        compiler_params=_params(("parallel", "parallel")),
        name="out0",
    )(h, attn, gb, cin, cin, cin, conv_w, w)


def _out1_kernel(h_ref, a_ref, w_ref, o_ref):
    o_ref[...] = h_ref[...] + jnp.dot(a_ref[...], w_ref[...], preferred_element_type=F32)


def _out1(h, attn, w, bm):
    t, d = h.shape
    return pl.pallas_call(
        _out1_kernel,
        grid=(t // bm,),
        in_specs=[
            pl.BlockSpec((bm, d), lambda i: (i, 0)),
            pl.BlockSpec((bm, attn.shape[1]), lambda i: (i, 0)),
            pl.BlockSpec(w.shape, lambda i: (0, 0)),
        ],
        out_specs=pl.BlockSpec((bm, d), lambda i: (i, 0)),
        out_shape=jax.ShapeDtypeStruct(h.shape, F32),
        compiler_params=_params(("parallel",)),
        name="out1",
    )(h, attn, w)


def _mlp_kernel(h_ref, g_ref, wup_ref, wdn_ref, fg_ref, o_ref, xn_sc, *, final_norm):
    f = pl.program_id(1)

    @pl.when(f == 0)
    def _():
        x = h_ref[...]
        xn_sc[...] = _rms(x, g_ref[...]).astype(BF16)
        o_ref[...] = x

    u = jnp.dot(xn_sc[...], wup_ref[...], preferred_element_type=F32)
    a = jnp.square(jnp.maximum(u, 0.0)).astype(BF16)
    o_ref[...] += jnp.dot(a, wdn_ref[...], preferred_element_type=F32)

    if final_norm:
        @pl.when(f == pl.num_programs(1) - 1)
        def _():
            o_ref[...] = _rms(o_ref[...], fg_ref[...])


def _mlp(h, g, wup, wdn, fg, *, bm, bf, final_norm):
    t, d = h.shape
    dff = wup.shape[1]
    return pl.pallas_call(
        functools.partial(_mlp_kernel, final_norm=final_norm),
        grid=(t // bm, dff // bf),
        in_specs=[
            pl.BlockSpec((bm, d), lambda i, f: (i, 0)),
            pl.BlockSpec((1, d), lambda i, f: (0, 0)),
            pl.BlockSpec((d, bf), lambda i, f: (0, f)),
            pl.BlockSpec((bf, d), lambda i, f: (f, 0)),
            pl.BlockSpec((1, d), lambda i, f: (0, 0)),
        ],
        out_specs=pl.BlockSpec((bm, d), lambda i, f: (i, 0)),
        out_shape=jax.ShapeDtypeStruct(h.shape, F32),
        scratch_shapes=[pltpu.VMEM((bm, d), BF16)],
        compiler_params=_params(("parallel", "arbitrary")),
        name="mlp_final" if final_norm else "mlp",
    )(h, g, wup, wdn, fg)


def _rope_angles(seq, rot_dim):
    rows = seq // GRID_W
    row = jnp.repeat(jnp.arange(rows, dtype=F32), GRID_W)
    col = jnp.tile(jnp.arange(GRID_W, dtype=F32), rows)
    axis_dim = rot_dim // 2
    inv_freq = ROPE_THETA ** (-jnp.arange(0, axis_dim, 2, dtype=F32) / axis_dim)
    return jnp.concatenate([row[:, None] * inv_freq, col[:, None] * inv_freq], axis=-1)


def _rope_tables(seq, rot_dim):
    ang = _rope_angles(seq, rot_dim)
    cos, sin = jnp.cos(ang), jnp.sin(ang)
    pad = jnp.zeros((seq, LANES // 2 - rot_dim // 2), F32)
    cos_t = jnp.concatenate([cos, pad, cos, pad], axis=-1)
    sin_t = jnp.concatenate([-sin, pad, sin, pad], axis=-1)
    return cos_t, sin_t


def _pair_split(w):
    return w[..., 0::2], w[..., 1::2]


def kernel(x, even_norm_g, even_w_in, even_q_norm_g, even_k_norm_g, even_conv_w, even_w_out,
           odd_norm_g, odd_w_down, odd_q_lat_g, odd_kv_lat_g, odd_w_uq, odd_w_ukv, odd_w_o,
           mlp_norm_g, mlp_w_up, mlp_w_down, final_norm_g):
    b, s, d = x.shape
    t = _tiles(s)
    depth = mlp_norm_g.shape[0]
    cos_a, sin_a = _rope_tables(s, HEAD_DIM)
    cos_c, sin_c = _rope_tables(s, MLA_ROPE)
    fg = final_norm_g.reshape(1, d)

    h = x
    for layer in range(depth):
        i = layer // 2
        if layer % 2 == 0:
            w_in = even_w_in[i]
            wq = w_in[:, :ATTN_WIDTH].reshape(d, ATTN_HEADS, HEAD_DIM)
            wk = w_in[:, ATTN_WIDTH:ATTN_WIDTH + KV_WIDTH].reshape(d, ATTN_KV_HEADS, HEAD_DIM)
            wq = jnp.concatenate(_pair_split(wq), axis=-1).reshape(d, ATTN_WIDTH)
            wk = jnp.concatenate(_pair_split(wk), axis=-1).reshape(d, KV_WIDTH)
            w0 = jnp.concatenate([wq, wk, w_in[:, ATTN_WIDTH + KV_WIDTH:]], axis=-1).astype(BF16)
            qg = jnp.concatenate(_pair_split(even_q_norm_g[i]), axis=-1).reshape(1, HEAD_DIM)
            kg = jnp.concatenate(_pair_split(even_k_norm_g[i]), axis=-1).reshape(1, HEAD_DIM)
            q, k, vt, gb, cin = _proj0(h, even_norm_g[i].reshape(1, d), w0, qg, kg,
                                       cos_a, sin_a, t["chunk"])
            attn = _attention(q, k, vt, group=ATTN_GROUP, bq=t["bq_gqa"], name="gqa_attn")
            h = _out0(h, attn, gb, cin, even_conv_w[i], even_w_out[i].astype(BF16), t["bm_out"])
        else:
            w_down = odd_w_down[i]
            kr_e, kr_o = _pair_split(w_down[:, Q_LORA + KV_LORA:])
            zpad = jnp.zeros((d, LANES // 2 - MLA_ROPE // 2), F32)
            wd = jnp.concatenate([w_down[:, :Q_LORA + KV_LORA], kr_e, zpad, kr_o, zpad],
                                 axis=-1).astype(BF16)
            wuq = odd_w_uq[i].reshape(Q_LORA, MLA_HEADS, MLA_NOPE + MLA_ROPE)
            qr_e, qr_o = _pair_split(wuq[..., MLA_NOPE:])
            zq = jnp.zeros((Q_LORA, MLA_HEADS, LANES // 2 - MLA_ROPE // 2), F32)
            wuq = jnp.concatenate([wuq[..., :MLA_NOPE], qr_e, zq, qr_o, zq], axis=-1)
            wuq = wuq.reshape(Q_LORA, MLA_HEADS * MLA_QK_PAD).astype(BF16)
            q, k, vt = _proj1(h, odd_norm_g[i].reshape(1, d), wd,
                              odd_q_lat_g[i].reshape(1, Q_LORA), odd_kv_lat_g[i].reshape(1, KV_LORA),
                              wuq, odd_w_ukv[i].astype(BF16), cos_c, sin_c, t["chunk"])
            attn = _attention(q, k, vt, group=1, bq=t["bq_mla"], name="mla_attn")
            h = _out1(h.reshape(b * s, d), attn.reshape(b * s, -1), odd_w_o[i].astype(BF16),
                      t["bm_out"]).reshape(b, s, d)
        h = _mlp(h.reshape(b * s, d), mlp_norm_g[layer].reshape(1, d),
                 mlp_w_up[layer].astype(BF16), mlp_w_down[layer].astype(BF16), fg,
                 bm=t["bm_mlp"], bf=t["bf_mlp"],
                 final_norm=(layer == depth - 1)).reshape(b, s, d)
    if depth == 0:
        h = _rms(h, final_norm_g)
    return h
```

```python
import functools

import jax
import jax.numpy as jnp
from jax import lax
from jax.experimental import pallas as pl
from jax.experimental.pallas import tpu as pltpu

F32 = jnp.float32
BF16 = jnp.bfloat16

NORM_EPS = 1e-6
ROPE_THETA = 10000.0
GRID_W = 64

HEAD_DIM = 128
ATTN_HEADS = 8
ATTN_KV_HEADS = 2
ATTN_GROUP = ATTN_HEADS // ATTN_KV_HEADS
ATTN_WIDTH = ATTN_HEADS * HEAD_DIM
KV_WIDTH = ATTN_KV_HEADS * HEAD_DIM
CONV_WIDTH = 1024

MLA_HEADS = 16
MLA_NOPE = 128
MLA_ROPE = 64
MLA_V = 128
Q_LORA = 512
KV_LORA = 512
MLA_QK_PAD = 256
MLA_DOWN_PAD = Q_LORA + KV_LORA + 128

LANES = 128
BF16_SUBLANES = 16
VMEM_LIMIT_BYTES = 56 * 1024 * 1024
NEG_BIG = -1e30
LOG2_E = 1.4426950408889634
ATTN_COL_GROUP = 512
ATTN_TRIP_ITEMS = 16
ATTN_LOOKAHEAD = 2
ATTN_RING = 4


def _tiles(seq):
    chunk = min(512, seq)
    return dict(
        chunk=chunk,
        bm_mlp=min(1024, seq),
        bf_mlp=512,
        bm_out=min(512, seq),
        bq_gqa=min(1024, seq),
        bq_mla=min(2048, seq),
    )


def _params(sem):
    return pltpu.CompilerParams(dimension_semantics=sem, vmem_limit_bytes=VMEM_LIMIT_BYTES)


def _rms(x, g):
    return x * lax.rsqrt(jnp.mean(x * x, axis=-1, keepdims=True) + NORM_EPS) * g


def _rope(x, cos, sin):
    return x * cos + pltpu.roll(x, LANES // 2, 1) * sin


def _proj0_kernel(h_ref, g_ref, w_ref, qg_ref, kg_ref, cos_ref, sin_ref,
                  q_ref, k_ref, vt_ref, gb_ref, cin_ref):
    xn = _rms(h_ref[...], g_ref[...]).astype(BF16)
    y = jnp.dot(xn, w_ref[...], preferred_element_type=F32)
    cos = cos_ref[...]
    sin = sin_ref[...]
    scale = HEAD_DIM ** -0.5 * LOG2_E
    for hh in range(ATTN_HEADS):
        yh = _rms(y[:, hh * HEAD_DIM:(hh + 1) * HEAD_DIM], qg_ref[...])
        q_ref[hh] = (_rope(yh, cos, sin) * scale).astype(BF16)
    o = ATTN_WIDTH
    for hh in range(ATTN_KV_HEADS):
        yh = _rms(y[:, o + hh * HEAD_DIM:o + (hh + 1) * HEAD_DIM], kg_ref[...])
        k_ref[hh] = _rope(yh, cos, sin).astype(BF16)
    o += KV_WIDTH
    for hh in range(ATTN_KV_HEADS):
        vt_ref[hh] = y[:, o + hh * HEAD_DIM:o + (hh + 1) * HEAD_DIM].T.astype(BF16)
    o += KV_WIDTH
    gb_ref[...] = y[:, o:o + CONV_WIDTH].astype(BF16)
    o += CONV_WIDTH
    cin_ref[...] = (y[:, o:o + CONV_WIDTH] * y[:, o + CONV_WIDTH:o + 2 * CONV_WIDTH]).astype(BF16)


def _proj0(h, g, w, qg, kg, cos, sin, chunk):
    b, s, d = h.shape
    n = w.shape[1]
    nc = s // chunk
    return pl.pallas_call(
        _proj0_kernel,
        grid=(b, nc),
        in_specs=[
            pl.BlockSpec((None, chunk, d), lambda bi, i: (bi, i, 0)),
            pl.BlockSpec((1, d), lambda bi, i: (0, 0)),
            pl.BlockSpec((d, n), lambda bi, i: (0, 0)),
            pl.BlockSpec((1, HEAD_DIM), lambda bi, i: (0, 0)),
            pl.BlockSpec((1, HEAD_DIM), lambda bi, i: (0, 0)),
            pl.BlockSpec((chunk, HEAD_DIM), lambda bi, i: (i, 0)),
            pl.BlockSpec((chunk, HEAD_DIM), lambda bi, i: (i, 0)),
        ],
        out_specs=[
            pl.BlockSpec((None, ATTN_HEADS, chunk, HEAD_DIM), lambda bi, i: (bi, 0, i, 0)),
            pl.BlockSpec((None, ATTN_KV_HEADS, chunk, HEAD_DIM), lambda bi, i: (bi, 0, i, 0)),
            pl.BlockSpec((None, ATTN_KV_HEADS, None, HEAD_DIM, chunk), lambda bi, i: (bi, 0, i, 0, 0)),
            pl.BlockSpec((None, chunk, CONV_WIDTH), lambda bi, i: (bi, i, 0)),
            pl.BlockSpec((None, chunk, CONV_WIDTH), lambda bi, i: (bi, i, 0)),
        ],
        out_shape=[
            jax.ShapeDtypeStruct((b, ATTN_HEADS, s, HEAD_DIM), BF16),
            jax.ShapeDtypeStruct((b, ATTN_KV_HEADS, s, HEAD_DIM), BF16),
            jax.ShapeDtypeStruct((b, ATTN_KV_HEADS, nc, HEAD_DIM, chunk), BF16),
            jax.ShapeDtypeStruct((b, s, CONV_WIDTH), BF16),
            jax.ShapeDtypeStruct((b, s, CONV_WIDTH), BF16),
        ],
        compiler_params=_params(("parallel", "parallel")),
        name="proj0",
    )(h, g, w, qg, kg, cos, sin)


def _proj1_kernel(h_ref, g_ref, wd_ref, qg_ref, kvg_ref, wuq_ref, wukv_ref, cos_ref, sin_ref,
                  q_ref, k_ref, vt_ref):
    xn = _rms(h_ref[...], g_ref[...]).astype(BF16)
    lat = jnp.dot(xn, wd_ref[...], preferred_element_type=F32)
    cq = _rms(lat[:, :Q_LORA], qg_ref[...]).astype(BF16)
    ckv = _rms(lat[:, Q_LORA:Q_LORA + KV_LORA], kvg_ref[...]).astype(BF16)
    cos = cos_ref[...]
    sin = sin_ref[...]
    kr = _rope(lat[:, Q_LORA + KV_LORA:], cos, sin).astype(BF16)
    q = jnp.dot(cq, wuq_ref[...], preferred_element_type=F32)
    kv = jnp.dot(ckv, wukv_ref[...], preferred_element_type=F32)
    scale = (MLA_NOPE + MLA_ROPE) ** -0.5 * LOG2_E
    for hh in range(MLA_HEADS):
        o = hh * MLA_QK_PAD
        q_ref[hh, :, :MLA_NOPE] = (q[:, o:o + MLA_NOPE] * scale).astype(BF16)
        qr = _rope(q[:, o + MLA_NOPE:o + MLA_QK_PAD], cos, sin)
        q_ref[hh, :, MLA_NOPE:] = (qr * scale).astype(BF16)
        o = hh * (MLA_NOPE + MLA_V)
        k_ref[hh, :, :MLA_NOPE] = kv[:, o:o + MLA_NOPE].astype(BF16)
        k_ref[hh, :, MLA_NOPE:] = kr
        vt_ref[hh] = kv[:, o + MLA_NOPE:o + MLA_NOPE + MLA_V].T.astype(BF16)


def _proj1(h, g, wd, qg, kvg, wuq, wukv, cos, sin, chunk):
    b, s, d = h.shape
    nc = s // chunk
    const = lambda bi, i: (0, 0)
    return pl.pallas_call(
        _proj1_kernel,
        grid=(b, nc),
        in_specs=[
            pl.BlockSpec((None, chunk, d), lambda bi, i: (bi, i, 0)),
            pl.BlockSpec((1, d), const),
            pl.BlockSpec(wd.shape, const),
            pl.BlockSpec((1, Q_LORA), const),
            pl.BlockSpec((1, KV_LORA), const),
            pl.BlockSpec(wuq.shape, const),
            pl.BlockSpec(wukv.shape, const),
            pl.BlockSpec((chunk, LANES), lambda bi, i: (i, 0)),
            pl.BlockSpec((chunk, LANES), lambda bi, i: (i, 0)),
        ],
        out_specs=[
            pl.BlockSpec((None, MLA_HEADS, chunk, MLA_QK_PAD), lambda bi, i: (bi, 0, i, 0)),
            pl.BlockSpec((None, MLA_HEADS, chunk, MLA_QK_PAD), lambda bi, i: (bi, 0, i, 0)),
            pl.BlockSpec((None, MLA_HEADS, None, MLA_V, chunk), lambda bi, i: (bi, 0, i, 0, 0)),
        ],
        out_shape=[
            jax.ShapeDtypeStruct((b, MLA_HEADS, s, MLA_QK_PAD), BF16),
            jax.ShapeDtypeStruct((b, MLA_HEADS, s, MLA_QK_PAD), BF16),
            jax.ShapeDtypeStruct((b, MLA_HEADS, nc, MLA_V, chunk), BF16),
        ],
        compiler_params=_params(("parallel", "parallel")),
        name="proj1",
    )(h, g, wd, qg, kvg, wuq, wukv, cos, sin)


def _attn_kernel(q_ref, k_ref, vt_ref, o_ref, m_sc, l_sc, acc_sc, s_sc, mx_sc, *, group, bq, chunk,
                 n_chunks, unroll):
    n = group * bq
    gw = s_sc.shape[2]
    n_groups = n // gw
    ring = s_sc.shape[0]
    m_sc[...] = jnp.full(m_sc.shape, NEG_BIG, F32)
    l_sc[...] = jnp.zeros(l_sc.shape, F32)
    acc_sc[...] = jnp.zeros(acc_sc.shape, F32)

    items = [(j, gi) for j in range(unroll) for gi in range(n_groups)]
    assert len(items) % ring == 0
    lookahead = min(ATTN_LOOKAHEAD, ring - 1)

    def scores(c, gi, slot):
        kc = k_ref[pl.ds(pl.multiple_of(c * chunk, chunk), chunk), :]
        qg = q_ref[(gi * gw) // bq, pl.ds((gi * gw) % bq, gw), :]
        s = lax.dot_general(kc, qg, (((1,), (1,)), ((), ())), preferred_element_type=F32)
        s_sc[slot] = s
        mx_sc[slot] = jnp.max(s, axis=0, keepdims=True)

    for idx in range(lookahead):
        scores(items[idx][0], items[idx][1], idx)

    def body(t, carry):
        for idx, (j, gi) in enumerate(items):
            ahead = idx + lookahead
            ja, ga = items[ahead % len(items)]
            ca = jnp.minimum((t + ahead // len(items)) * unroll + ja, n_chunks - 1)
            scores(ca, ga, ahead % ring)

            cols = slice(gi * gw, (gi + 1) * gw)
            slot = idx % ring
            m_prev = m_sc[:, cols]
            m_new = jnp.maximum(m_prev, mx_sc[slot])
            alpha = jnp.exp2(m_prev - m_new)
            p = jnp.exp2(s_sc[slot] - m_new)
            l_sc[:, cols] = alpha * l_sc[:, cols] + jnp.sum(p, axis=0, keepdims=True)
            pv = jnp.dot(vt_ref[t * unroll + j], p.astype(BF16), preferred_element_type=F32)
            acc_sc[:, cols] = alpha * acc_sc[:, cols] + pv
            m_sc[:, cols] = m_new
        return carry

    lax.fori_loop(0, n_chunks // unroll, body, 0)
    o = acc_sc[...] / l_sc[...]
    dv = acc_sc.shape[0]
    for gi in range(group):
        o_ref[:, gi * dv:(gi + 1) * dv] = o[:, gi * bq:(gi + 1) * bq].T.astype(o_ref.dtype)


def _attention(q, k, vt, *, group, bq, name):
    b, h, s, dqk = q.shape
    hkv = k.shape[1]
    n_chunks, dv, chunk = vt.shape[2:]
    n = group * bq
    gw = min(ATTN_COL_GROUP, bq)
    unroll = max(1, ATTN_TRIP_ITEMS // (n // gw))
    while n_chunks % unroll:
        unroll -= 1
    items = unroll * (n // gw)
    ring = ATTN_RING if items % ATTN_RING == 0 else items
    kern = functools.partial(_attn_kernel, group=group, bq=bq, chunk=chunk, n_chunks=n_chunks,
                             unroll=unroll)
    return pl.pallas_call(
        kern,
        grid=(b, hkv, s // bq),
        in_specs=[
            pl.BlockSpec((None, group, bq, dqk), lambda bi, hi, qi: (bi, hi, qi, 0)),
            pl.BlockSpec((None, None, s, dqk), lambda bi, hi, qi: (bi, hi, 0, 0)),
            pl.BlockSpec((None, None, n_chunks, dv, chunk), lambda bi, hi, qi: (bi, hi, 0, 0, 0)),
        ],
        out_specs=pl.BlockSpec((None, bq, group * dv), lambda bi, hi, qi: (bi, qi, hi)),
        out_shape=jax.ShapeDtypeStruct((b, s, h * dv), BF16),
        scratch_shapes=[
            pltpu.VMEM((1, n), F32),
            pltpu.VMEM((1, n), F32),
            pltpu.VMEM((dv, n), F32),
            pltpu.VMEM((ring, chunk, gw), F32),
            pltpu.VMEM((ring, 1, gw), F32),
        ],
        compiler_params=_params(("parallel", "parallel", "arbitrary")),
        name=name,
    )(q, k, vt)


def _out0_kernel(h_ref, a_ref, gb_ref, cin_ref, cprev_ref, cnext_ref, cw_ref, w_ref, o_ref):
    i = pl.program_id(1)
    bm = cin_ref.shape[0]
    c = cin_ref[...].astype(F32)
    prev_row = cprev_ref[BF16_SUBLANES - 1:BF16_SUBLANES, :].astype(F32)
    next_row = cnext_ref[0:1, :].astype(F32)
    prev_row = jnp.where(i == 0, 0.0, prev_row)
    next_row = jnp.where(i == pl.num_programs(1) - 1, 0.0, next_row)
    rows = lax.broadcasted_iota(jnp.int32, (bm, 1), 0)
    c_m1 = jnp.where(rows == 0, prev_row, pltpu.roll(c, 1, 0))
    c_p1 = jnp.where(rows == bm - 1, next_row, pltpu.roll(c, bm - 1, 0))
    conv = cw_ref[0:1, :] * c_m1 + cw_ref[1:2, :] * c + cw_ref[2:3, :] * c_p1
    sconv = (gb_ref[...].astype(F32) * conv).astype(BF16)
    aw = a_ref.shape[1]
    y = jnp.dot(a_ref[...], w_ref[:aw, :], preferred_element_type=F32)
    y = y + jnp.dot(sconv, w_ref[aw:, :], preferred_element_type=F32)
    o_ref[...] = h_ref[...] + y


def _out0(h, attn, gb, cin, conv_w, w, bm):
    b, s, d = h.shape
    nb = s // bm
    r = bm // BF16_SUBLANES
    last = s // BF16_SUBLANES - 1
    tile = lambda width: pl.BlockSpec((None, bm, width), lambda bi, i: (bi, i, 0))
    return pl.pallas_call(
        _out0_kernel,
        grid=(b, nb),
        in_specs=[
            tile(d), tile(attn.shape[2]), tile(CONV_WIDTH), tile(CONV_WIDTH),
            pl.BlockSpec((None, BF16_SUBLANES, CONV_WIDTH),
                         lambda bi, i: (bi, jnp.maximum(i * r - 1, 0), 0)),
            pl.BlockSpec((None, BF16_SUBLANES, CONV_WIDTH),
                         lambda bi, i: (bi, jnp.minimum((i + 1) * r, last), 0)),
            pl.BlockSpec(conv_w.shape, lambda bi, i: (0, 0)),
            pl.BlockSpec(w.shape, lambda bi, i: (0, 0)),
        ],
        out_specs=tile(d),
        out_shape=jax.ShapeDtypeStruct(h.shape, F32),
        compiler_params=_params(("parallel", "parallel")),
        name="out0",
    )(h, attn, gb, cin, cin, cin, conv_w, w)


def _out1_kernel(h_ref, a_ref, w_ref, o_ref):
    o_ref[...] = h_ref[...] + jnp.dot(a_ref[...], w_ref[...], preferred_element_type=F32)


def _out1(h, attn, w, bm):
    t, d = h.shape
    return pl.pallas_call(
        _out1_kernel,
        grid=(t // bm,),
        in_specs=[
            pl.BlockSpec((bm, d), lambda i: (i, 0)),
            pl.BlockSpec((bm, attn.shape[1]), lambda i: (i, 0)),
            pl.BlockSpec(w.shape, lambda i: (0, 0)),
        ],
        out_specs=pl.BlockSpec((bm, d), lambda i: (i, 0)),
        out_shape=jax.ShapeDtypeStruct(h.shape, F32),
        compiler_params=_params(("parallel",)),
        name="out1",
    )(h, attn, w)


def _mlp_kernel(h_ref, g_ref, wup_ref, wdn_ref, fg_ref, o_ref, xn_sc, *, final_norm):
    f = pl.program_id(1)

    @pl.when(f == 0)
    def _():
        x = h_ref[...]
        xn_sc[...] = _rms(x, g_ref[...]).astype(BF16)
        o_ref[...] = x

    u = jnp.dot(xn_sc[...], wup_ref[...], preferred_element_type=F32)
    a = jnp.square(jnp.maximum(u, 0.0)).astype(BF16)
    o_ref[...] += jnp.dot(a, wdn_ref[...], preferred_element_type=F32)

    if final_norm:
        @pl.when(f == pl.num_programs(1) - 1)
        def _():
            o_ref[...] = _rms(o_ref[...], fg_ref[...])


def _mlp(h, g, wup, wdn, fg, *, layer, bm, bf, final_norm):
    t, d = h.shape
    dff = wup.shape[2]
    return pl.pallas_call(
        functools.partial(_mlp_kernel, final_norm=final_norm),
        grid=(t // bm, dff // bf),
        in_specs=[
            pl.BlockSpec((bm, d), lambda i, f: (i, 0)),
            pl.BlockSpec((1, d), lambda i, f: (0, 0)),
            pl.BlockSpec((None, d, bf), lambda i, f: (layer, 0, f)),
            pl.BlockSpec((None, bf, d), lambda i, f: (layer, f, 0)),
            pl.BlockSpec((1, d), lambda i, f: (0, 0)),
        ],
        out_specs=pl.BlockSpec((bm, d), lambda i, f: (i, 0)),
        out_shape=jax.ShapeDtypeStruct(h.shape, F32),
        scratch_shapes=[pltpu.VMEM((bm, d), BF16)],
        compiler_params=_params(("parallel", "arbitrary")),
        name="mlp_final" if final_norm else "mlp",
    )(h, g, wup, wdn, fg)


def _rope_angles(seq, rot_dim):
    rows = seq // GRID_W
    row = jnp.repeat(jnp.arange(rows, dtype=F32), GRID_W)
    col = jnp.tile(jnp.arange(GRID_W, dtype=F32), rows)
    axis_dim = rot_dim // 2
    inv_freq = ROPE_THETA ** (-jnp.arange(0, axis_dim, 2, dtype=F32) / axis_dim)
    return jnp.concatenate([row[:, None] * inv_freq, col[:, None] * inv_freq], axis=-1)


def _rope_tables(seq, rot_dim):
    ang = _rope_angles(seq, rot_dim)
    cos, sin = jnp.cos(ang), jnp.sin(ang)
    pad = jnp.zeros((seq, LANES // 2 - rot_dim // 2), F32)
    cos_t = jnp.concatenate([cos, pad, cos, pad], axis=-1)
    sin_t = jnp.concatenate([-sin, pad, sin, pad], axis=-1)
    return cos_t, sin_t


def _pair_split(w):
    return w[..., 0::2], w[..., 1::2]


def kernel(x, even_norm_g, even_w_in, even_q_norm_g, even_k_norm_g, even_conv_w, even_w_out,
           odd_norm_g, odd_w_down, odd_q_lat_g, odd_kv_lat_g, odd_w_uq, odd_w_ukv, odd_w_o,
           mlp_norm_g, mlp_w_up, mlp_w_down, final_norm_g):
    b, s, d = x.shape
    t = _tiles(s)
    depth = mlp_norm_g.shape[0]
    cos_a, sin_a = _rope_tables(s, HEAD_DIM)
    cos_c, sin_c = _rope_tables(s, MLA_ROPE)
    fg = final_norm_g.reshape(1, d)
    w_up = mlp_w_up.astype(BF16)
    w_dn = mlp_w_down.astype(BF16)

    h = x
    for layer in range(depth):
        i = layer // 2
        if layer % 2 == 0:
            w_in = even_w_in[i]
            wq = w_in[:, :ATTN_WIDTH].reshape(d, ATTN_HEADS, HEAD_DIM)
            wk = w_in[:, ATTN_WIDTH:ATTN_WIDTH + KV_WIDTH].reshape(d, ATTN_KV_HEADS, HEAD_DIM)
            wq = jnp.concatenate(_pair_split(wq), axis=-1).reshape(d, ATTN_WIDTH)
            wk = jnp.concatenate(_pair_split(wk), axis=-1).reshape(d, KV_WIDTH)
            w0 = jnp.concatenate([wq, wk, w_in[:, ATTN_WIDTH + KV_WIDTH:]], axis=-1).astype(BF16)
            qg = jnp.concatenate(_pair_split(even_q_norm_g[i]), axis=-1).reshape(1, HEAD_DIM)
            kg = jnp.concatenate(_pair_split(even_k_norm_g[i]), axis=-1).reshape(1, HEAD_DIM)
            q, k, vt, gb, cin = _proj0(h, even_norm_g[i].reshape(1, d), w0, qg, kg,
                                       cos_a, sin_a, t["chunk"])
            attn = _attention(q, k, vt, group=ATTN_GROUP, bq=t["bq_gqa"], name="gqa_attn")
            h = _out0(h, attn, gb, cin, even_conv_w[i], even_w_out[i].astype(BF16), t["bm_out"])
        else:
            w_down = odd_w_down[i]
            kr_e, kr_o = _pair_split(w_down[:, Q_LORA + KV_LORA:])
            zpad = jnp.zeros((d, LANES // 2 - MLA_ROPE // 2), F32)
            wd = jnp.concatenate([w_down[:, :Q_LORA + KV_LORA], kr_e, zpad, kr_o, zpad],
                                 axis=-1).astype(BF16)
            wuq = odd_w_uq[i].reshape(Q_LORA, MLA_HEADS, MLA_NOPE + MLA_ROPE)
            qr_e, qr_o = _pair_split(wuq[..., MLA_NOPE:])
            zq = jnp.zeros((Q_LORA, MLA_HEADS, LANES // 2 - MLA_ROPE // 2), F32)
            wuq = jnp.concatenate([wuq[..., :MLA_NOPE], qr_e, zq, qr_o, zq], axis=-1)
            wuq = wuq.reshape(Q_LORA, MLA_HEADS * MLA_QK_PAD).astype(BF16)
            q, k, vt = _proj1(h, odd_norm_g[i].reshape(1, d), wd,
                              odd_q_lat_g[i].reshape(1, Q_LORA), odd_kv_lat_g[i].reshape(1, KV_LORA),
                              wuq, odd_w_ukv[i].astype(BF16), cos_c, sin_c, t["chunk"])
            attn = _attention(q, k, vt, group=1, bq=t["bq_mla"], name="mla_attn")
            h = _out1(h.reshape(b * s, d), attn.reshape(b * s, -1), odd_w_o[i].astype(BF16),
                      t["bm_out"]).reshape(b, s, d)
        h = _mlp(h.reshape(b * s, d), mlp_norm_g[layer].reshape(1, d), w_up, w_dn, fg,
                 layer=layer, bm=t["bm_mlp"], bf=t["bf_mlp"],
                 final_norm=(layer == depth - 1)).reshape(b, s, d)
    return h
```

```python
import functools

import jax
import jax.numpy as jnp
from jax import lax
from jax.experimental import pallas as pl
from jax.experimental.pallas import tpu as pltpu

F32 = jnp.float32
BF16 = jnp.bfloat16

NORM_EPS = 1e-6
ROPE_THETA = 10000.0
GRID_W = 64

HEAD_DIM = 128
ATTN_HEADS = 8
ATTN_KV_HEADS = 2
ATTN_GROUP = ATTN_HEADS // ATTN_KV_HEADS
ATTN_WIDTH = ATTN_HEADS * HEAD_DIM
KV_WIDTH = ATTN_KV_HEADS * HEAD_DIM
CONV_WIDTH = 1024

MLA_HEADS = 16
MLA_NOPE = 128
MLA_ROPE = 64
MLA_V = 128
Q_LORA = 512
KV_LORA = 512
MLA_QK_PAD = 256
MLA_DOWN_PAD = Q_LORA + KV_LORA + 128

LANES = 128
BF16_SUBLANES = 16
VMEM_LIMIT_BYTES = 56 * 1024 * 1024
NEG_BIG = -1e30
LOG2_E = 1.4426950408889634
ATTN_COL_GROUP = 512
ATTN_TRIP_ITEMS = 16
ATTN_LOOKAHEAD = 2
ATTN_RING = 4


def _tiles(seq):
    chunk = min(512, seq)
    return dict(
        chunk=chunk,
        bm_mlp=min(1024, seq),
        bf_mlp=512,
        bm_out=min(512, seq),
        bq_gqa=min(1024, seq),
        bq_mla=min(2048, seq),
    )


def _params(sem):
    return pltpu.CompilerParams(dimension_semantics=sem, vmem_limit_bytes=VMEM_LIMIT_BYTES)


def _rms(x, g):
    return x * lax.rsqrt(jnp.mean(x * x, axis=-1, keepdims=True) + NORM_EPS) * g


def _rope(x, cos, sin):
    return x * cos + pltpu.roll(x, LANES // 2, 1) * sin


def _proj0_kernel(h_ref, g_ref, w_ref, qg_ref, kg_ref, cos_ref, sin_ref,
                  q_ref, k_ref, vt_ref, gb_ref, cin_ref):
    xn = _rms(h_ref[...], g_ref[...]).astype(BF16)
    y = jnp.dot(xn, w_ref[...], preferred_element_type=F32)
    cos = cos_ref[...]
    sin = sin_ref[...]
    scale = HEAD_DIM ** -0.5 * LOG2_E
    for hh in range(ATTN_HEADS):
        yh = _rms(y[:, hh * HEAD_DIM:(hh + 1) * HEAD_DIM], qg_ref[...])
        q_ref[hh] = (_rope(yh, cos, sin) * scale).astype(BF16)
    o = ATTN_WIDTH
    for hh in range(ATTN_KV_HEADS):
        yh = _rms(y[:, o + hh * HEAD_DIM:o + (hh + 1) * HEAD_DIM], kg_ref[...])
        k_ref[hh] = _rope(yh, cos, sin).astype(BF16)
    o += KV_WIDTH
    for hh in range(ATTN_KV_HEADS):
        vt_ref[hh] = y[:, o + hh * HEAD_DIM:o + (hh + 1) * HEAD_DIM].T.astype(BF16)
    o += KV_WIDTH
    gb_ref[...] = y[:, o:o + CONV_WIDTH].astype(BF16)
    o += CONV_WIDTH
    cin_ref[...] = (y[:, o:o + CONV_WIDTH] * y[:, o + CONV_WIDTH:o + 2 * CONV_WIDTH]).astype(BF16)


def _proj0(h, g, w, qg, kg, cos, sin, chunk):
    b, s, d = h.shape
    n = w.shape[1]
    nc = s // chunk
    return pl.pallas_call(
        _proj0_kernel,
        grid=(b, nc),
        in_specs=[
            pl.BlockSpec((None, chunk, d), lambda bi, i: (bi, i, 0)),
            pl.BlockSpec((1, d), lambda bi, i: (0, 0)),
            pl.BlockSpec((d, n), lambda bi, i: (0, 0)),
            pl.BlockSpec((1, HEAD_DIM), lambda bi, i: (0, 0)),
            pl.BlockSpec((1, HEAD_DIM), lambda bi, i: (0, 0)),
            pl.BlockSpec((chunk, HEAD_DIM), lambda bi, i: (i, 0)),
            pl.BlockSpec((chunk, HEAD_DIM), lambda bi, i: (i, 0)),
        ],
        out_specs=[
            pl.BlockSpec((None, ATTN_HEADS, chunk, HEAD_DIM), lambda bi, i: (bi, 0, i, 0)),
            pl.BlockSpec((None, ATTN_KV_HEADS, chunk, HEAD_DIM), lambda bi, i: (bi, 0, i, 0)),
            pl.BlockSpec((None, ATTN_KV_HEADS, None, HEAD_DIM, chunk), lambda bi, i: (bi, 0, i, 0, 0)),
            pl.BlockSpec((None, chunk, CONV_WIDTH), lambda bi, i: (bi, i, 0)),
            pl.BlockSpec((None, chunk, CONV_WIDTH), lambda bi, i: (bi, i, 0)),
        ],
        out_shape=[
            jax.ShapeDtypeStruct((b, ATTN_HEADS, s, HEAD_DIM), BF16),
            jax.ShapeDtypeStruct((b, ATTN_KV_HEADS, s, HEAD_DIM), BF16),
            jax.ShapeDtypeStruct((b, ATTN_KV_HEADS, nc, HEAD_DIM, chunk), BF16),
            jax.ShapeDtypeStruct((b, s, CONV_WIDTH), BF16),
            jax.ShapeDtypeStruct((b, s, CONV_WIDTH), BF16),
        ],
        compiler_params=_params(("parallel", "parallel")),
        name="proj0",
    )(h, g, w, qg, kg, cos, sin)


def _proj1_kernel(h_ref, g_ref, wd_ref, qg_ref, kvg_ref, wuq_ref, wukv_ref, cos_ref, sin_ref,
                  q_ref, k_ref, vt_ref):
    xn = _rms(h_ref[...], g_ref[...]).astype(BF16)
    lat = jnp.dot(xn, wd_ref[...], preferred_element_type=F32)
    cq = _rms(lat[:, :Q_LORA], qg_ref[...]).astype(BF16)
    ckv = _rms(lat[:, Q_LORA:Q_LORA + KV_LORA], kvg_ref[...]).astype(BF16)
    cos = cos_ref[...]
    sin = sin_ref[...]
    kr = _rope(lat[:, Q_LORA + KV_LORA:], cos, sin).astype(BF16)
    q = jnp.dot(cq, wuq_ref[...], preferred_element_type=F32)
    kv = jnp.dot(ckv, wukv_ref[...], preferred_element_type=F32)
    scale = (MLA_NOPE + MLA_ROPE) ** -0.5 * LOG2_E
    for hh in range(MLA_HEADS):
        o = hh * MLA_QK_PAD
        q_ref[hh, :, :MLA_NOPE] = (q[:, o:o + MLA_NOPE] * scale).astype(BF16)
        qr = _rope(q[:, o + MLA_NOPE:o + MLA_QK_PAD], cos, sin)
        q_ref[hh, :, MLA_NOPE:] = (qr * scale).astype(BF16)
        o = hh * (MLA_NOPE + MLA_V)
        k_ref[hh, :, :MLA_NOPE] = kv[:, o:o + MLA_NOPE].astype(BF16)
        k_ref[hh, :, MLA_NOPE:] = kr
        vt_ref[hh] = kv[:, o + MLA_NOPE:o + MLA_NOPE + MLA_V].T.astype(BF16)


def _proj1(h, g, wd, qg, kvg, wuq, wukv, cos, sin, chunk):
    b, s, d = h.shape
    nc = s // chunk
    const = lambda bi, i: (0, 0)
    return pl.pallas_call(
        _proj1_kernel,
        grid=(b, nc),
        in_specs=[
            pl.BlockSpec((None, chunk, d), lambda bi, i: (bi, i, 0)),
            pl.BlockSpec((1, d), const),
            pl.BlockSpec(wd.shape, const),
            pl.BlockSpec((1, Q_LORA), const),
            pl.BlockSpec((1, KV_LORA), const),
            pl.BlockSpec(wuq.shape, const),
            pl.BlockSpec(wukv.shape, const),
            pl.BlockSpec((chunk, LANES), lambda bi, i: (i, 0)),
            pl.BlockSpec((chunk, LANES), lambda bi, i: (i, 0)),
        ],
        out_specs=[
            pl.BlockSpec((None, MLA_HEADS, chunk, MLA_QK_PAD), lambda bi, i: (bi, 0, i, 0)),
            pl.BlockSpec((None, MLA_HEADS, chunk, MLA_QK_PAD), lambda bi, i: (bi, 0, i, 0)),
            pl.BlockSpec((None, MLA_HEADS, None, MLA_V, chunk), lambda bi, i: (bi, 0, i, 0, 0)),
        ],
        out_shape=[
            jax.ShapeDtypeStruct((b, MLA_HEADS, s, MLA_QK_PAD), BF16),
            jax.ShapeDtypeStruct((b, MLA_HEADS, s, MLA_QK_PAD), BF16),
            jax.ShapeDtypeStruct((b, MLA_HEADS, nc, MLA_V, chunk), BF16),
        ],
        compiler_params=_params(("parallel", "parallel")),
        name="proj1",
    )(h, g, wd, qg, kvg, wuq, wukv, cos, sin)


def _attn_kernel(q_ref, k_ref, vt_ref, o_ref, m_sc, l_sc, acc_sc, s_sc, mx_sc, qt_sc, *, group, bq,
                 chunk, n_chunks, unroll):
    n = group * bq
    gw = s_sc.shape[2]
    n_groups = n // gw
    ring = s_sc.shape[0]
    m_sc[...] = jnp.full(m_sc.shape, NEG_BIG, F32)
    l_sc[...] = jnp.zeros(l_sc.shape, F32)
    acc_sc[...] = jnp.zeros(acc_sc.shape, F32)
    for gi in range(n_groups):
        qg = q_ref[(gi * gw) // bq, pl.ds((gi * gw) % bq, gw), :]
        qt_sc[:, gi * gw:(gi + 1) * gw] = qg.astype(F32).T.astype(BF16)

    items = [(j, gi) for j in range(unroll) for gi in range(n_groups)]
    assert len(items) % ring == 0
    lookahead = min(ATTN_LOOKAHEAD, ring - 1)

    def scores(c, gi, slot):
        kc = k_ref[pl.ds(pl.multiple_of(c * chunk, chunk), chunk), :]
        s = jnp.dot(kc, qt_sc[:, gi * gw:(gi + 1) * gw], preferred_element_type=F32)
        s_sc[slot] = s
        mx_sc[slot] = jnp.max(s, axis=0, keepdims=True)

    for idx in range(lookahead):
        scores(items[idx][0], items[idx][1], idx)

    def body(t, carry):
        for idx, (j, gi) in enumerate(items):
            ahead = idx + lookahead
            ja, ga = items[ahead % len(items)]
            ca = jnp.minimum((t + ahead // len(items)) * unroll + ja, n_chunks - 1)
            scores(ca, ga, ahead % ring)

            cols = slice(gi * gw, (gi + 1) * gw)
            slot = idx % ring
            m_prev = m_sc[:, cols]
            m_new = jnp.maximum(m_prev, mx_sc[slot])
            alpha = jnp.exp2(m_prev - m_new)
            p = jnp.exp2(s_sc[slot] - m_new)
            l_sc[:, cols] = alpha * l_sc[:, cols] + jnp.sum(p, axis=0, keepdims=True)
            pv = jnp.dot(vt_ref[t * unroll + j], p.astype(BF16), preferred_element_type=F32)
            acc_sc[:, cols] = alpha * acc_sc[:, cols] + pv
            m_sc[:, cols] = m_new
        return carry

    lax.fori_loop(0, n_chunks // unroll, body, 0)
    o = acc_sc[...] / l_sc[...]
    dv = acc_sc.shape[0]
    for gi in range(group):
        o_ref[:, gi * dv:(gi + 1) * dv] = o[:, gi * bq:(gi + 1) * bq].T.astype(o_ref.dtype)


def _attention(q, k, vt, *, group, bq, name):
    b, h, s, dqk = q.shape
    hkv = k.shape[1]
    n_chunks, dv, chunk = vt.shape[2:]
    n = group * bq
    gw = min(ATTN_COL_GROUP, bq)
    unroll = max(1, ATTN_TRIP_ITEMS // (n // gw))
    while n_chunks % unroll:
        unroll -= 1
    items = unroll * (n // gw)
    ring = ATTN_RING if items % ATTN_RING == 0 else items
    kern = functools.partial(_attn_kernel, group=group, bq=bq, chunk=chunk, n_chunks=n_chunks,
                             unroll=unroll)
    return pl.pallas_call(
        kern,
        grid=(b, hkv, s // bq),
        in_specs=[
            pl.BlockSpec((None, group, bq, dqk), lambda bi, hi, qi: (bi, hi, qi, 0)),
            pl.BlockSpec((None, None, s, dqk), lambda bi, hi, qi: (bi, hi, 0, 0)),
            pl.BlockSpec((None, None, n_chunks, dv, chunk), lambda bi, hi, qi: (bi, hi, 0, 0, 0)),
        ],
        out_specs=pl.BlockSpec((None, bq, group * dv), lambda bi, hi, qi: (bi, qi, hi)),
        out_shape=jax.ShapeDtypeStruct((b, s, h * dv), BF16),
        scratch_shapes=[
            pltpu.VMEM((1, n), F32),
            pltpu.VMEM((1, n), F32),
            pltpu.VMEM((dv, n), F32),
            pltpu.VMEM((ring, chunk, gw), F32),
            pltpu.VMEM((ring, 1, gw), F32),
            pltpu.VMEM((dqk, n), BF16),
        ],
        compiler_params=_params(("parallel", "parallel", "arbitrary")),
        name=name,
    )(q, k, vt)


def _out0_kernel(h_ref, a_ref, gb_ref, cin_ref, cprev_ref, cnext_ref, cw_ref, w_ref, o_ref):
    i = pl.program_id(1)
    bm = cin_ref.shape[0]
    c = cin_ref[...].astype(F32)
    prev_row = cprev_ref[BF16_SUBLANES - 1:BF16_SUBLANES, :].astype(F32)
    next_row = cnext_ref[0:1, :].astype(F32)
    prev_row = jnp.where(i == 0, 0.0, prev_row)
    next_row = jnp.where(i == pl.num_programs(1) - 1, 0.0, next_row)
    rows = lax.broadcasted_iota(jnp.int32, (bm, 1), 0)
    c_m1 = jnp.where(rows == 0, prev_row, pltpu.roll(c, 1, 0))
    c_p1 = jnp.where(rows == bm - 1, next_row, pltpu.roll(c, bm - 1, 0))
    conv = cw_ref[0:1, :] * c_m1 + cw_ref[1:2, :] * c + cw_ref[2:3, :] * c_p1
    sconv = (gb_ref[...].astype(F32) * conv).astype(BF16)
    aw = a_ref.shape[1]
    y = jnp.dot(a_ref[...], w_ref[:aw, :], preferred_element_type=F32)
    y = y + jnp.dot(sconv, w_ref[aw:, :], preferred_element_type=F32)
    o_ref[...] = h_ref[...] + y


def _out0(h, attn, gb, cin, conv_w, w, bm):
    b, s, d = h.shape
    nb = s // bm
    r = bm // BF16_SUBLANES
    last = s // BF16_SUBLANES - 1
    tile = lambda width: pl.BlockSpec((None, bm, width), lambda bi, i: (bi, i, 0))
    return pl.pallas_call(
        _out0_kernel,
        grid=(b, nb),
        in_specs=[
            tile(d), tile(attn.shape[2]), tile(CONV_WIDTH), tile(CONV_WIDTH),
            pl.BlockSpec((None, BF16_SUBLANES, CONV_WIDTH),
                         lambda bi, i: (bi, jnp.maximum(i * r - 1, 0), 0)),
            pl.BlockSpec((None, BF16_SUBLANES, CONV_WIDTH),
                         lambda bi, i: (bi, jnp.minimum((i + 1) * r, last), 0)),
            pl.BlockSpec(conv_w.shape, lambda bi, i: (0, 0)),
            pl.BlockSpec(w.shape, lambda bi, i: (0, 0)),
        ],
        out_specs=tile(d),
        out_shape=jax.ShapeDtypeStruct(h.shape, F32),
        compiler_params=_params(("parallel", "parallel")),
        name="out0",
    )(h, attn, gb, cin, cin, cin, conv_w, w)


def _out1_kernel(h_ref, a_ref, w_ref, o_ref):
    o_ref[...] = h_ref[...] + jnp.dot(a_ref[...], w_ref[...], preferred_element_type=F32)


def _out1(h, attn, w, bm):
    t, d = h.shape
    return pl.pallas_call(
        _out1_kernel,
        grid=(t // bm,),
        in_specs=[
            pl.BlockSpec((bm, d), lambda i: (i, 0)),
            pl.BlockSpec((bm, attn.shape[1]), lambda i: (i, 0)),
            pl.BlockSpec(w.shape, lambda i: (0, 0)),
        ],
        out_specs=pl.BlockSpec((bm, d), lambda i: (i, 0)),
        out_shape=jax.ShapeDtypeStruct(h.shape, F32),
        compiler_params=_params(("parallel",)),
        name="out1",
    )(h, attn, w)


def _mlp_kernel(h_ref, g_ref, wup_ref, wdn_ref, fg_ref, o_ref, xn_sc, *, final_norm):
    f = pl.program_id(1)

    @pl.when(f == 0)
    def _():
        x = h_ref[...]
        xn_sc[...] = _rms(x, g_ref[...]).astype(BF16)
        o_ref[...] = x

    u = jnp.dot(xn_sc[...], wup_ref[...], preferred_element_type=F32)
    a = jnp.square(jnp.maximum(u, 0.0)).astype(BF16)
    o_ref[...] += jnp.dot(a, wdn_ref[...], preferred_element_type=F32)

    if final_norm:
        @pl.when(f == pl.num_programs(1) - 1)
        def _():
            o_ref[...] = _rms(o_ref[...], fg_ref[...])


def _mlp(h, g, wup, wdn, fg, *, layer, bm, bf, final_norm):
    t, d = h.shape
    dff = wup.shape[2]
    return pl.pallas_call(
        functools.partial(_mlp_kernel, final_norm=final_norm),
        grid=(t // bm, dff // bf),
        in_specs=[
            pl.BlockSpec((bm, d), lambda i, f: (i, 0)),
            pl.BlockSpec((1, d), lambda i, f: (0, 0)),
            pl.BlockSpec((None, d, bf), lambda i, f: (layer, 0, f)),
            pl.BlockSpec((None, bf, d), lambda i, f: (layer, f, 0)),
            pl.BlockSpec((1, d), lambda i, f: (0, 0)),
        ],
        out_specs=pl.BlockSpec((bm, d), lambda i, f: (i, 0)),
        out_shape=jax.ShapeDtypeStruct(h.shape, F32),
        scratch_shapes=[pltpu.VMEM((bm, d), BF16)],
        compiler_params=_params(("parallel", "arbitrary")),
        name="mlp_final" if final_norm else "mlp",
    )(h, g, wup, wdn, fg)


def _rope_angles(seq, rot_dim):
    rows = seq // GRID_W
    row = jnp.repeat(jnp.arange(rows, dtype=F32), GRID_W)
    col = jnp.tile(jnp.arange(GRID_W, dtype=F32), rows)
    axis_dim = rot_dim // 2
    inv_freq = ROPE_THETA ** (-jnp.arange(0, axis_dim, 2, dtype=F32) / axis_dim)
    return jnp.concatenate([row[:, None] * inv_freq, col[:, None] * inv_freq], axis=-1)


def _rope_tables(seq, rot_dim):
    ang = _rope_angles(seq, rot_dim)
    cos, sin = jnp.cos(ang), jnp.sin(ang)
    pad = jnp.zeros((seq, LANES // 2 - rot_dim // 2), F32)
    cos_t = jnp.concatenate([cos, pad, cos, pad], axis=-1)
    sin_t = jnp.concatenate([-sin, pad, sin, pad], axis=-1)
    return cos_t, sin_t


def _pair_split(w):
    return w[..., 0::2], w[..., 1::2]


def _pair_split_sources(rot_dim, pad):
    blank = [-1] * pad
    return list(range(0, rot_dim, 2)) + blank + list(range(1, rot_dim, 2)) + blank


_PAIR_SPLIT_128 = _pair_split_sources(HEAD_DIM, 0)
_ROPE_PAD_64 = _pair_split_sources(MLA_ROPE, LANES // 2 - MLA_ROPE // 2)
_MLA_Q_HEAD = list(range(MLA_NOPE)) + [MLA_NOPE + c if c >= 0 else -1 for c in _ROPE_PAD_64]


def _relayout(w, sources):
    sel = [[1.0 if src == k else 0.0 for src in sources] for k in range(w.shape[-1])]
    return jnp.dot(w, jnp.array(sel, BF16), preferred_element_type=BF16)


def kernel(x, even_norm_g, even_w_in, even_q_norm_g, even_k_norm_g, even_conv_w, even_w_out,
           odd_norm_g, odd_w_down, odd_q_lat_g, odd_kv_lat_g, odd_w_uq, odd_w_ukv, odd_w_o,
           mlp_norm_g, mlp_w_up, mlp_w_down, final_norm_g):
    b, s, d = x.shape
    t = _tiles(s)
    depth = mlp_norm_g.shape[0]
    cos_a, sin_a = _rope_tables(s, HEAD_DIM)
    cos_c, sin_c = _rope_tables(s, MLA_ROPE)
    fg = final_norm_g.reshape(1, d)
    w_up = mlp_w_up.astype(BF16)
    w_dn = mlp_w_down.astype(BF16)

    h = x
    for layer in range(depth):
        i = layer // 2
        if layer % 2 == 0:
            w_in = even_w_in[i].astype(BF16)
            wq = w_in[:, :ATTN_WIDTH].reshape(d, ATTN_HEADS, HEAD_DIM)
            wk = w_in[:, ATTN_WIDTH:ATTN_WIDTH + KV_WIDTH].reshape(d, ATTN_KV_HEADS, HEAD_DIM)
            wq = _relayout(wq, _PAIR_SPLIT_128).reshape(d, ATTN_WIDTH)
            wk = _relayout(wk, _PAIR_SPLIT_128).reshape(d, KV_WIDTH)
            w0 = jnp.concatenate([wq, wk, w_in[:, ATTN_WIDTH + KV_WIDTH:]], axis=-1)
            qg = jnp.concatenate(_pair_split(even_q_norm_g[i]), axis=-1).reshape(1, HEAD_DIM)
            kg = jnp.concatenate(_pair_split(even_k_norm_g[i]), axis=-1).reshape(1, HEAD_DIM)
            q, k, vt, gb, cin = _proj0(h, even_norm_g[i].reshape(1, d), w0, qg, kg,
                                       cos_a, sin_a, t["chunk"])
            attn = _attention(q, k, vt, group=ATTN_GROUP, bq=t["bq_gqa"], name="gqa_attn")
            h = _out0(h, attn, gb, cin, even_conv_w[i], even_w_out[i].astype(BF16), t["bm_out"])
        else:
            w_down = odd_w_down[i].astype(BF16)
            wd = jnp.concatenate([w_down[:, :Q_LORA + KV_LORA],
                                  _relayout(w_down[:, Q_LORA + KV_LORA:], _ROPE_PAD_64)], axis=-1)
            wuq = odd_w_uq[i].astype(BF16).reshape(Q_LORA, MLA_HEADS, MLA_NOPE + MLA_ROPE)
            wuq = _relayout(wuq, _MLA_Q_HEAD).reshape(Q_LORA, MLA_HEADS * MLA_QK_PAD)
            q, k, vt = _proj1(h, odd_norm_g[i].reshape(1, d), wd,
                              odd_q_lat_g[i].reshape(1, Q_LORA), odd_kv_lat_g[i].reshape(1, KV_LORA),
                              wuq, odd_w_ukv[i].astype(BF16), cos_c, sin_c, t["chunk"])
            attn = _attention(q, k, vt, group=1, bq=t["bq_mla"], name="mla_attn")
            h = _out1(h.reshape(b * s, d), attn.reshape(b * s, -1), odd_w_o[i].astype(BF16),
                      t["bm_out"]).reshape(b, s, d)
        h = _mlp(h.reshape(b * s, d), mlp_norm_g[layer].reshape(1, d), w_up, w_dn, fg,
                 layer=layer, bm=t["bm_mlp"], bf=t["bf_mlp"],
                 final_norm=(layer == depth - 1)).reshape(b, s, d)
    return h
```

```python
import functools

import jax
import jax.numpy as jnp
from jax import lax
from jax.experimental import pallas as pl
from jax.experimental.pallas import tpu as pltpu

F32 = jnp.float32
BF16 = jnp.bfloat16

NORM_EPS = 1e-6
ROPE_THETA = 10000.0
GRID_W = 64

HEAD_DIM = 128
ATTN_HEADS = 8
ATTN_KV_HEADS = 2
ATTN_GROUP = ATTN_HEADS // ATTN_KV_HEADS
ATTN_WIDTH = ATTN_HEADS * HEAD_DIM
KV_WIDTH = ATTN_KV_HEADS * HEAD_DIM
CONV_WIDTH = 1024

MLA_HEADS = 16
MLA_NOPE = 128
MLA_ROPE = 64
MLA_V = 128
Q_LORA = 512
KV_LORA = 512
MLA_QK_PAD = 256
MLA_DOWN_PAD = Q_LORA + KV_LORA + 128

LANES = 128
BF16_SUBLANES = 16
VMEM_LIMIT_BYTES = 56 * 1024 * 1024
NEG_BIG = -1e30
LOG2_E = 1.4426950408889634
ATTN_COL_GROUP = 512
ATTN_TRIP_ITEMS = 32
ATTN_LOOKAHEAD = 2
ATTN_RING = 4


def _tiles(seq):
    chunk = min(512, seq)
    return dict(
        chunk=chunk,
        kv_chunk=min(512, seq),
        bm_mlp=min(1024, seq),
        bf_mlp=512,
        bm_out=min(512, seq),
        bq_gqa=min(1024, seq),
        bq_mla=min(2048, seq),
    )


def _params(sem):
    return pltpu.CompilerParams(dimension_semantics=sem, vmem_limit_bytes=VMEM_LIMIT_BYTES)


def _rms(x, g):
    return x * lax.rsqrt(jnp.mean(x * x, axis=-1, keepdims=True) + NORM_EPS) * g


def _rope(x, cos, sin):
    return x * cos + pltpu.roll(x, LANES // 2, 1) * sin


def _proj0_kernel(h_ref, g_ref, w_ref, qg_ref, kg_ref, cos_ref, sin_ref,
                  q_ref, k_ref, vt_ref, gb_ref, cin_ref):
    xn = _rms(h_ref[...], g_ref[...]).astype(BF16)
    y = jnp.dot(xn, w_ref[...], preferred_element_type=F32)
    cos = cos_ref[...]
    sin = sin_ref[...]
    scale = HEAD_DIM ** -0.5 * LOG2_E
    for hh in range(ATTN_HEADS):
        yh = _rms(y[:, hh * HEAD_DIM:(hh + 1) * HEAD_DIM], qg_ref[...])
        q_ref[hh] = (_rope(yh, cos, sin) * scale).astype(BF16)
    o = ATTN_WIDTH
    for hh in range(ATTN_KV_HEADS):
        yh = _rms(y[:, o + hh * HEAD_DIM:o + (hh + 1) * HEAD_DIM], kg_ref[...])
        k_ref[hh] = _rope(yh, cos, sin).astype(BF16)
    o += KV_WIDTH
    for hh in range(ATTN_KV_HEADS):
        vt_ref[hh] = y[:, o + hh * HEAD_DIM:o + (hh + 1) * HEAD_DIM].T.astype(BF16)
    o += KV_WIDTH
    gb_ref[...] = y[:, o:o + CONV_WIDTH].astype(BF16)
    o += CONV_WIDTH
    cin_ref[...] = (y[:, o:o + CONV_WIDTH] * y[:, o + CONV_WIDTH:o + 2 * CONV_WIDTH]).astype(BF16)


def _proj0(h, g, w, qg, kg, cos, sin, chunk, kv_chunk):
    b, s, d = h.shape
    n = w.shape[1]
    nc = s // chunk
    r = kv_chunk // chunk
    return pl.pallas_call(
        _proj0_kernel,
        grid=(b, nc),
        in_specs=[
            pl.BlockSpec((None, chunk, d), lambda bi, i: (bi, i, 0)),
            pl.BlockSpec((1, d), lambda bi, i: (0, 0)),
            pl.BlockSpec((d, n), lambda bi, i: (0, 0)),
            pl.BlockSpec((1, HEAD_DIM), lambda bi, i: (0, 0)),
            pl.BlockSpec((1, HEAD_DIM), lambda bi, i: (0, 0)),
            pl.BlockSpec((chunk, HEAD_DIM), lambda bi, i: (i, 0)),
            pl.BlockSpec((chunk, HEAD_DIM), lambda bi, i: (i, 0)),
        ],
        out_specs=[
            pl.BlockSpec((None, ATTN_HEADS, chunk, HEAD_DIM), lambda bi, i: (bi, 0, i, 0)),
            pl.BlockSpec((None, ATTN_KV_HEADS, chunk, HEAD_DIM), lambda bi, i: (bi, 0, i, 0)),
            pl.BlockSpec((None, ATTN_KV_HEADS, None, HEAD_DIM, chunk),
                         lambda bi, i: (bi, 0, i // r, 0, i % r)),
            pl.BlockSpec((None, chunk, CONV_WIDTH), lambda bi, i: (bi, i, 0)),
            pl.BlockSpec((None, chunk, CONV_WIDTH), lambda bi, i: (bi, i, 0)),
        ],
        out_shape=[
            jax.ShapeDtypeStruct((b, ATTN_HEADS, s, HEAD_DIM), BF16),
            jax.ShapeDtypeStruct((b, ATTN_KV_HEADS, s, HEAD_DIM), BF16),
            jax.ShapeDtypeStruct((b, ATTN_KV_HEADS, s // kv_chunk, HEAD_DIM, kv_chunk), BF16),
            jax.ShapeDtypeStruct((b, s, CONV_WIDTH), BF16),
            jax.ShapeDtypeStruct((b, s, CONV_WIDTH), BF16),
        ],
        compiler_params=_params(("parallel", "parallel")),
        name="proj0",
    )(h, g, w, qg, kg, cos, sin)


def _proj1_kernel(h_ref, g_ref, wd_ref, qg_ref, kvg_ref, wuq_ref, wukv_ref, cos_ref, sin_ref,
                  q_ref, k_ref, vt_ref):
    xn = _rms(h_ref[...], g_ref[...]).astype(BF16)
    lat = jnp.dot(xn, wd_ref[...], preferred_element_type=F32)
    cq = _rms(lat[:, :Q_LORA], qg_ref[...]).astype(BF16)
    ckv = _rms(lat[:, Q_LORA:Q_LORA + KV_LORA], kvg_ref[...]).astype(BF16)
    cos = cos_ref[...]
    sin = sin_ref[...]
    kr = _rope(lat[:, Q_LORA + KV_LORA:], cos, sin).astype(BF16)
    q = jnp.dot(cq, wuq_ref[...], preferred_element_type=F32)
    kv = jnp.dot(ckv, wukv_ref[...], preferred_element_type=F32)
    scale = (MLA_NOPE + MLA_ROPE) ** -0.5 * LOG2_E
    for hh in range(MLA_HEADS):
        o = hh * MLA_QK_PAD
        q_ref[hh, :, :MLA_NOPE] = (q[:, o:o + MLA_NOPE] * scale).astype(BF16)
        qr = _rope(q[:, o + MLA_NOPE:o + MLA_QK_PAD], cos, sin)
        q_ref[hh, :, MLA_NOPE:] = (qr * scale).astype(BF16)
        o = hh * (MLA_NOPE + MLA_V)
        k_ref[hh, :, :MLA_NOPE] = kv[:, o:o + MLA_NOPE].astype(BF16)
        k_ref[hh, :, MLA_NOPE:] = kr
        vt_ref[hh] = kv[:, o + MLA_NOPE:o + MLA_NOPE + MLA_V].T.astype(BF16)


def _proj1(h, g, wd, qg, kvg, wuq, wukv, cos, sin, chunk, kv_chunk):
    b, s, d = h.shape
    nc = s // chunk
    r = kv_chunk // chunk
    const = lambda bi, i: (0, 0)
    return pl.pallas_call(
        _proj1_kernel,
        grid=(b, nc),
        in_specs=[
            pl.BlockSpec((None, chunk, d), lambda bi, i: (bi, i, 0)),
            pl.BlockSpec((1, d), const),
            pl.BlockSpec(wd.shape, const),
            pl.BlockSpec((1, Q_LORA), const),
            pl.BlockSpec((1, KV_LORA), const),
            pl.BlockSpec(wuq.shape, const),
            pl.BlockSpec(wukv.shape, const),
            pl.BlockSpec((chunk, LANES), lambda bi, i: (i, 0)),
            pl.BlockSpec((chunk, LANES), lambda bi, i: (i, 0)),
        ],
        out_specs=[
            pl.BlockSpec((None, MLA_HEADS, chunk, MLA_QK_PAD), lambda bi, i: (bi, 0, i, 0)),
            pl.BlockSpec((None, MLA_HEADS, chunk, MLA_QK_PAD), lambda bi, i: (bi, 0, i, 0)),
            pl.BlockSpec((None, MLA_HEADS, None, MLA_V, chunk), lambda bi, i: (bi, 0, i // r, 0, i % r)),
        ],
        out_shape=[
            jax.ShapeDtypeStruct((b, MLA_HEADS, s, MLA_QK_PAD), BF16),
            jax.ShapeDtypeStruct((b, MLA_HEADS, s, MLA_QK_PAD), BF16),
            jax.ShapeDtypeStruct((b, MLA_HEADS, s // kv_chunk, MLA_V, kv_chunk), BF16),
        ],
        compiler_params=_params(("parallel", "parallel")),
        name="proj1",
    )(h, g, wd, qg, kvg, wuq, wukv, cos, sin)


def _attn_kernel(q_ref, k_ref, vt_ref, o_ref, m_sc, l_sc, acc_sc, s_sc, mx_sc, qt_sc, *, group, bq,
                 chunk, n_chunks, unroll):
    n = group * bq
    gw = s_sc.shape[2]
    n_groups = n // gw
    ring = s_sc.shape[0]
    m_sc[...] = jnp.full(m_sc.shape, NEG_BIG, F32)
    l_sc[...] = jnp.zeros(l_sc.shape, F32)
    acc_sc[...] = jnp.zeros(acc_sc.shape, F32)
    for gi in range(n_groups):
        qg = q_ref[(gi * gw) // bq, pl.ds((gi * gw) % bq, gw), :]
        qt_sc[:, gi * gw:(gi + 1) * gw] = qg.astype(F32).T.astype(BF16)

    items = [(j, gi) for j in range(unroll) for gi in range(n_groups)]
    assert len(items) % ring == 0
    lookahead = min(ATTN_LOOKAHEAD, ring - 1)

    def scores(c, gi, slot):
        kc = k_ref[pl.ds(pl.multiple_of(c * chunk, chunk), chunk), :]
        s = jnp.dot(kc, qt_sc[:, gi * gw:(gi + 1) * gw], preferred_element_type=F32)
        s_sc[slot] = s
        mx_sc[slot] = jnp.max(s, axis=0, keepdims=True)

    for idx in range(lookahead):
        scores(items[idx][0], items[idx][1], idx)

    def body(t, carry):
        for idx, (j, gi) in enumerate(items):
            ahead = idx + lookahead
            ja, ga = items[ahead % len(items)]
            ca = jnp.minimum((t + ahead // len(items)) * unroll + ja, n_chunks - 1)
            scores(ca, ga, ahead % ring)

            cols = slice(gi * gw, (gi + 1) * gw)
            slot = idx % ring
            m_prev = m_sc[:, cols]
            m_new = jnp.maximum(m_prev, mx_sc[slot])
            alpha = jnp.exp2(m_prev - m_new)
            p = jnp.exp2(s_sc[slot] - m_new)
            l_sc[:, cols] = alpha * l_sc[:, cols] + jnp.sum(p, axis=0, keepdims=True)
            pv = jnp.dot(vt_ref[t * unroll + j], p.astype(BF16), preferred_element_type=F32)
            acc_sc[:, cols] = alpha * acc_sc[:, cols] + pv
            m_sc[:, cols] = m_new
        return carry

    lax.fori_loop(0, n_chunks // unroll, body, 0)
    o = acc_sc[...] / l_sc[...]
    dv = acc_sc.shape[0]
    for gi in range(group):
        o_ref[:, gi * dv:(gi + 1) * dv] = o[:, gi * bq:(gi + 1) * bq].T.astype(o_ref.dtype)


def _attention(q, k, vt, *, group, bq, name):
    b, h, s, dqk = q.shape
    hkv = k.shape[1]
    n_chunks, dv, chunk = vt.shape[2:]
    n = group * bq
    gw = min(ATTN_COL_GROUP, bq)
    unroll = max(1, ATTN_TRIP_ITEMS // (n // gw))
    while n_chunks % unroll:
        unroll -= 1
    items = unroll * (n // gw)
    ring = ATTN_RING if items % ATTN_RING == 0 else items
    kern = functools.partial(_attn_kernel, group=group, bq=bq, chunk=chunk, n_chunks=n_chunks,
                             unroll=unroll)
    return pl.pallas_call(
        kern,
        grid=(b, hkv, s // bq),
        in_specs=[
            pl.BlockSpec((None, group, bq, dqk), lambda bi, hi, qi: (bi, hi, qi, 0)),
            pl.BlockSpec((None, None, s, dqk), lambda bi, hi, qi: (bi, hi, 0, 0)),
            pl.BlockSpec((None, None, n_chunks, dv, chunk), lambda bi, hi, qi: (bi, hi, 0, 0, 0)),
        ],
        out_specs=pl.BlockSpec((None, bq, group * dv), lambda bi, hi, qi: (bi, qi, hi)),
        out_shape=jax.ShapeDtypeStruct((b, s, h * dv), BF16),
        scratch_shapes=[
            pltpu.VMEM((1, n), F32),
            pltpu.VMEM((1, n), F32),
            pltpu.VMEM((dv, n), F32),
            pltpu.VMEM((ring, chunk, gw), F32),
            pltpu.VMEM((ring, 1, gw), F32),
            pltpu.VMEM((dqk, n), BF16),
        ],
        compiler_params=_params(("parallel", "parallel", "arbitrary")),
        name=name,
    )(q, k, vt)


def _out0_kernel(h_ref, a_ref, gb_ref, cin_ref, cprev_ref, cnext_ref, cw_ref, w_ref, o_ref):
    i = pl.program_id(1)
    bm = cin_ref.shape[0]
    c = cin_ref[...].astype(F32)
    prev_row = cprev_ref[BF16_SUBLANES - 1:BF16_SUBLANES, :].astype(F32)
    next_row = cnext_ref[0:1, :].astype(F32)
    prev_row = jnp.where(i == 0, 0.0, prev_row)
    next_row = jnp.where(i == pl.num_programs(1) - 1, 0.0, next_row)
    rows = lax.broadcasted_iota(jnp.int32, (bm, 1), 0)
    c_m1 = jnp.where(rows == 0, prev_row, pltpu.roll(c, 1, 0))
    c_p1 = jnp.where(rows == bm - 1, next_row, pltpu.roll(c, bm - 1, 0))
    conv = cw_ref[0:1, :] * c_m1 + cw_ref[1:2, :] * c + cw_ref[2:3, :] * c_p1
    sconv = (gb_ref[...].astype(F32) * conv).astype(BF16)
    aw = a_ref.shape[1]
    y = jnp.dot(a_ref[...], w_ref[:aw, :], preferred_element_type=F32)
    y = y + jnp.dot(sconv, w_ref[aw:, :], preferred_element_type=F32)
    o_ref[...] = h_ref[...] + y


def _out0(h, attn, gb, cin, conv_w, w, bm):
    b, s, d = h.shape
    nb = s // bm
    r = bm // BF16_SUBLANES
    last = s // BF16_SUBLANES - 1
    tile = lambda width: pl.BlockSpec((None, bm, width), lambda bi, i: (bi, i, 0))
    return pl.pallas_call(
        _out0_kernel,
        grid=(b, nb),
        in_specs=[
            tile(d), tile(attn.shape[2]), tile(CONV_WIDTH), tile(CONV_WIDTH),
            pl.BlockSpec((None, BF16_SUBLANES, CONV_WIDTH),
                         lambda bi, i: (bi, jnp.maximum(i * r - 1, 0), 0)),
            pl.BlockSpec((None, BF16_SUBLANES, CONV_WIDTH),
                         lambda bi, i: (bi, jnp.minimum((i + 1) * r, last), 0)),
            pl.BlockSpec(conv_w.shape, lambda bi, i: (0, 0)),
            pl.BlockSpec(w.shape, lambda bi, i: (0, 0)),
        ],
        out_specs=tile(d),
        out_shape=jax.ShapeDtypeStruct(h.shape, F32),
        compiler_params=_params(("parallel", "parallel")),
        name="out0",
    )(h, attn, gb, cin, cin, cin, conv_w, w)


def _out1_kernel(h_ref, a_ref, w_ref, o_ref):
    o_ref[...] = h_ref[...] + jnp.dot(a_ref[...], w_ref[...], preferred_element_type=F32)


def _out1(h, attn, w, bm):
    t, d = h.shape
    return pl.pallas_call(
        _out1_kernel,
        grid=(t // bm,),
        in_specs=[
            pl.BlockSpec((bm, d), lambda i: (i, 0)),
            pl.BlockSpec((bm, attn.shape[1]), lambda i: (i, 0)),
            pl.BlockSpec(w.shape, lambda i: (0, 0)),
        ],
        out_specs=pl.BlockSpec((bm, d), lambda i: (i, 0)),
        out_shape=jax.ShapeDtypeStruct(h.shape, F32),
        compiler_params=_params(("parallel",)),
        name="out1",
    )(h, attn, w)


def _mlp_kernel(h_ref, g_ref, wup_ref, wdn_ref, fg_ref, o_ref, xn_sc, *, final_norm):
    f = pl.program_id(1)

    @pl.when(f == 0)
    def _():
        x = h_ref[...]
        xn_sc[...] = _rms(x, g_ref[...]).astype(BF16)
        o_ref[...] = x

    u = jnp.dot(xn_sc[...], wup_ref[...], preferred_element_type=F32)
    a = jnp.square(jnp.maximum(u, 0.0)).astype(BF16)
    o_ref[...] += jnp.dot(a, wdn_ref[...], preferred_element_type=F32)

    if final_norm:
        @pl.when(f == pl.num_programs(1) - 1)
        def _():
            o_ref[...] = _rms(o_ref[...], fg_ref[...])


def _mlp(h, g, wup, wdn, fg, *, layer, bm, bf, final_norm):
    t, d = h.shape
    dff = wup.shape[2]
    return pl.pallas_call(
        functools.partial(_mlp_kernel, final_norm=final_norm),
        grid=(t // bm, dff // bf),
        in_specs=[
            pl.BlockSpec((bm, d), lambda i, f: (i, 0)),
            pl.BlockSpec((1, d), lambda i, f: (0, 0)),
            pl.BlockSpec((None, d, bf), lambda i, f: (layer, 0, f)),
            pl.BlockSpec((None, bf, d), lambda i, f: (layer, f, 0)),
            pl.BlockSpec((1, d), lambda i, f: (0, 0)),
        ],
        out_specs=pl.BlockSpec((bm, d), lambda i, f: (i, 0)),
        out_shape=jax.ShapeDtypeStruct(h.shape, F32),
        scratch_shapes=[pltpu.VMEM((bm, d), BF16)],
        compiler_params=_params(("parallel", "arbitrary")),
        name="mlp_final" if final_norm else "mlp",
    )(h, g, wup, wdn, fg)


def _rope_angles(seq, rot_dim):
    rows = seq // GRID_W
    row = jnp.repeat(jnp.arange(rows, dtype=F32), GRID_W)
    col = jnp.tile(jnp.arange(GRID_W, dtype=F32), rows)
    axis_dim = rot_dim // 2
    inv_freq = ROPE_THETA ** (-jnp.arange(0, axis_dim, 2, dtype=F32) / axis_dim)
    return jnp.concatenate([row[:, None] * inv_freq, col[:, None] * inv_freq], axis=-1)


def _rope_tables(seq, rot_dim):
    ang = _rope_angles(seq, rot_dim)
    cos, sin = jnp.cos(ang), jnp.sin(ang)
    pad = jnp.zeros((seq, LANES // 2 - rot_dim // 2), F32)
    cos_t = jnp.concatenate([cos, pad, cos, pad], axis=-1)
    sin_t = jnp.concatenate([-sin, pad, sin, pad], axis=-1)
    return cos_t, sin_t


def _pair_split(w):
    return w[..., 0::2], w[..., 1::2]


def _pair_split_sources(rot_dim, pad):
    blank = [-1] * pad
    return list(range(0, rot_dim, 2)) + blank + list(range(1, rot_dim, 2)) + blank


_PAIR_SPLIT_128 = _pair_split_sources(HEAD_DIM, 0)
_ROPE_PAD_64 = _pair_split_sources(MLA_ROPE, LANES // 2 - MLA_ROPE // 2)
_MLA_Q_HEAD = list(range(MLA_NOPE)) + [MLA_NOPE + c if c >= 0 else -1 for c in _ROPE_PAD_64]


def _relayout(w, sources):
    sel = [[1.0 if src == k else 0.0 for src in sources] for k in range(w.shape[-1])]
    return jnp.dot(w, jnp.array(sel, BF16), preferred_element_type=BF16)


def kernel(x, even_norm_g, even_w_in, even_q_norm_g, even_k_norm_g, even_conv_w, even_w_out,
           odd_norm_g, odd_w_down, odd_q_lat_g, odd_kv_lat_g, odd_w_uq, odd_w_ukv, odd_w_o,
           mlp_norm_g, mlp_w_up, mlp_w_down, final_norm_g):
    b, s, d = x.shape
    t = _tiles(s)
    depth = mlp_norm_g.shape[0]
    cos_a, sin_a = _rope_tables(s, HEAD_DIM)
    cos_c, sin_c = _rope_tables(s, MLA_ROPE)
    fg = final_norm_g.reshape(1, d)
    w_up = mlp_w_up.astype(BF16)
    w_dn = mlp_w_down.astype(BF16)

    h = x
    for layer in range(depth):
        i = layer // 2
        if layer % 2 == 0:
            w_in = even_w_in[i].astype(BF16)
            wq = w_in[:, :ATTN_WIDTH].reshape(d, ATTN_HEADS, HEAD_DIM)
            wk = w_in[:, ATTN_WIDTH:ATTN_WIDTH + KV_WIDTH].reshape(d, ATTN_KV_HEADS, HEAD_DIM)
            wq = _relayout(wq, _PAIR_SPLIT_128).reshape(d, ATTN_WIDTH)
            wk = _relayout(wk, _PAIR_SPLIT_128).reshape(d, KV_WIDTH)
            w0 = jnp.concatenate([wq, wk, w_in[:, ATTN_WIDTH + KV_WIDTH:]], axis=-1)
            qg = jnp.concatenate(_pair_split(even_q_norm_g[i]), axis=-1).reshape(1, HEAD_DIM)
            kg = jnp.concatenate(_pair_split(even_k_norm_g[i]), axis=-1).reshape(1, HEAD_DIM)
            q, k, vt, gb, cin = _proj0(h, even_norm_g[i].reshape(1, d), w0, qg, kg,
                                       cos_a, sin_a, t["chunk"], t["kv_chunk"])
            attn = _attention(q, k, vt, group=ATTN_GROUP, bq=t["bq_gqa"], name="gqa_attn")
            h = _out0(h, attn, gb, cin, even_conv_w[i], even_w_out[i].astype(BF16), t["bm_out"])
        else:
            w_down = odd_w_down[i].astype(BF16)
            wd = jnp.concatenate([w_down[:, :Q_LORA + KV_LORA],
                                  _relayout(w_down[:, Q_LORA + KV_LORA:], _ROPE_PAD_64)], axis=-1)
            wuq = odd_w_uq[i].astype(BF16).reshape(Q_LORA, MLA_HEADS, MLA_NOPE + MLA_ROPE)
            wuq = _relayout(wuq, _MLA_Q_HEAD).reshape(Q_LORA, MLA_HEADS * MLA_QK_PAD)
            q, k, vt = _proj1(h, odd_norm_g[i].reshape(1, d), wd,
                              odd_q_lat_g[i].reshape(1, Q_LORA), odd_kv_lat_g[i].reshape(1, KV_LORA),
                              wuq, odd_w_ukv[i].astype(BF16), cos_c, sin_c, t["chunk"],
                              t["kv_chunk"])
            attn = _attention(q, k, vt, group=1, bq=t["bq_mla"], name="mla_attn")
            h = _out1(h.reshape(b * s, d), attn.reshape(b * s, -1), odd_w_o[i].astype(BF16),
                      t["bm_out"]).reshape(b, s, d)
        h = _mlp(h.reshape(b * s, d), mlp_norm_g[layer].reshape(1, d), w_up, w_dn, fg,
                 layer=layer, bm=t["bm_mlp"], bf=t["bf_mlp"],
                 final_norm=(layer == depth - 1)).reshape(b, s, d)
    return h
```

```python
import functools

import jax
import jax.numpy as jnp
from jax import lax
from jax.experimental import pallas as pl
from jax.experimental.pallas import tpu as pltpu

F32 = jnp.float32
BF16 = jnp.bfloat16

NORM_EPS = 1e-6
ROPE_THETA = 10000.0
GRID_W = 64

HEAD_DIM = 128
ATTN_HEADS = 8
ATTN_KV_HEADS = 2
ATTN_GROUP = ATTN_HEADS // ATTN_KV_HEADS
ATTN_WIDTH = ATTN_HEADS * HEAD_DIM
KV_WIDTH = ATTN_KV_HEADS * HEAD_DIM
CONV_WIDTH = 1024

MLA_HEADS = 16
MLA_NOPE = 128
MLA_ROPE = 64
MLA_V = 128
Q_LORA = 512
KV_LORA = 512
MLA_QK_PAD = 256
MLA_DOWN_PAD = Q_LORA + KV_LORA + 128

LANES = 128
BF16_SUBLANES = 16
VMEM_LIMIT_BYTES = 56 * 1024 * 1024
NEG_BIG = -1e30
LOG2_E = 1.4426950408889634
ATTN_COL_GROUP = 512
ATTN_TRIP_ITEMS = 32
ATTN_LOOKAHEAD = 2
ATTN_RING = 4


def _tiles(seq):
    chunk = min(512, seq)
    return dict(
        chunk=chunk,
        kv_chunk=min(512, seq),
        bm_mlp=min(1024, seq),
        bf_mlp=512,
        bm_out=min(512, seq),
        bq_gqa=min(2048, seq),
        bq_mla=min(4096, seq),
    )


def _params(sem):
    return pltpu.CompilerParams(dimension_semantics=sem, vmem_limit_bytes=VMEM_LIMIT_BYTES)


def _rms(x, g):
    return x * lax.rsqrt(jnp.mean(x * x, axis=-1, keepdims=True) + NORM_EPS) * g


def _store_vt(vt_ref, head, v):
    vt = v.T.astype(BF16)
    for i in range(vt_ref.shape[1]):
        vt_ref[head, i] = vt[:, i * LANES:(i + 1) * LANES]


def _rope(x, cos, sin):
    return x * cos + pltpu.roll(x, LANES // 2, 1) * sin


def _proj0_kernel(h_ref, g_ref, w_ref, qg_ref, kg_ref, cos_ref, sin_ref,
                  q_ref, k_ref, vt_ref, gb_ref, cin_ref):
    xn = _rms(h_ref[...], g_ref[...]).astype(BF16)
    y = jnp.dot(xn, w_ref[...], preferred_element_type=F32)
    cos = cos_ref[...]
    sin = sin_ref[...]
    scale = HEAD_DIM ** -0.5 * LOG2_E
    for hh in range(ATTN_HEADS):
        yh = _rms(y[:, hh * HEAD_DIM:(hh + 1) * HEAD_DIM], qg_ref[...])
        q_ref[hh] = (_rope(yh, cos, sin) * scale).astype(BF16)
    o = ATTN_WIDTH
    for hh in range(ATTN_KV_HEADS):
        yh = _rms(y[:, o + hh * HEAD_DIM:o + (hh + 1) * HEAD_DIM], kg_ref[...])
        k_ref[hh, 0] = _rope(yh, cos, sin).astype(BF16)
    o += KV_WIDTH
    for hh in range(ATTN_KV_HEADS):
        _store_vt(vt_ref, hh, y[:, o + hh * HEAD_DIM:o + (hh + 1) * HEAD_DIM])
    o += KV_WIDTH
    gb_ref[...] = y[:, o:o + CONV_WIDTH].astype(BF16)
    o += CONV_WIDTH
    cin_ref[...] = (y[:, o:o + CONV_WIDTH] * y[:, o + CONV_WIDTH:o + 2 * CONV_WIDTH]).astype(BF16)


def _proj0(h, g, w, qg, kg, cos, sin, chunk, kv_chunk):
    b, s, d = h.shape
    n = w.shape[1]
    nc = s // chunk
    r = kv_chunk // chunk
    return pl.pallas_call(
        _proj0_kernel,
        grid=(b, nc),
        in_specs=[
            pl.BlockSpec((None, chunk, d), lambda bi, i: (bi, i, 0)),
            pl.BlockSpec((1, d), lambda bi, i: (0, 0)),
            pl.BlockSpec((d, n), lambda bi, i: (0, 0)),
            pl.BlockSpec((1, HEAD_DIM), lambda bi, i: (0, 0)),
            pl.BlockSpec((1, HEAD_DIM), lambda bi, i: (0, 0)),
            pl.BlockSpec((chunk, HEAD_DIM), lambda bi, i: (i, 0)),
            pl.BlockSpec((chunk, HEAD_DIM), lambda bi, i: (i, 0)),
        ],
        out_specs=[
            pl.BlockSpec((None, ATTN_HEADS, chunk, HEAD_DIM), lambda bi, i: (bi, 0, i, 0)),
            pl.BlockSpec((None, ATTN_KV_HEADS, 1, chunk, LANES), lambda bi, i: (bi, 0, 0, i, 0)),
            pl.BlockSpec((None, ATTN_KV_HEADS, None, chunk // LANES, HEAD_DIM, LANES),
                         lambda bi, i: (bi, 0, i // r, i % r, 0, 0)),
            pl.BlockSpec((None, chunk, CONV_WIDTH), lambda bi, i: (bi, i, 0)),
            pl.BlockSpec((None, chunk, CONV_WIDTH), lambda bi, i: (bi, i, 0)),
        ],
        out_shape=[
            jax.ShapeDtypeStruct((b, ATTN_HEADS, s, HEAD_DIM), BF16),
            jax.ShapeDtypeStruct((b, ATTN_KV_HEADS, 1, s, LANES), BF16),
            jax.ShapeDtypeStruct((b, ATTN_KV_HEADS, s // kv_chunk, kv_chunk // LANES, HEAD_DIM, LANES),
                                 BF16),
            jax.ShapeDtypeStruct((b, s, CONV_WIDTH), BF16),
            jax.ShapeDtypeStruct((b, s, CONV_WIDTH), BF16),
        ],
        compiler_params=_params(("parallel", "parallel")),
        name="proj0",
    )(h, g, w, qg, kg, cos, sin)


def _proj1_kernel(h_ref, g_ref, wd_ref, qg_ref, kvg_ref, wuq_ref, wukv_ref, cos_ref, sin_ref,
                  q_ref, k_ref, vt_ref):
    xn = _rms(h_ref[...], g_ref[...]).astype(BF16)
    lat = jnp.dot(xn, wd_ref[...], preferred_element_type=F32)
    cq = _rms(lat[:, :Q_LORA], qg_ref[...]).astype(BF16)
    ckv = _rms(lat[:, Q_LORA:Q_LORA + KV_LORA], kvg_ref[...]).astype(BF16)
    cos = cos_ref[...]
    sin = sin_ref[...]
    kr = _rope(lat[:, Q_LORA + KV_LORA:], cos, sin).astype(BF16)
    q = jnp.dot(cq, wuq_ref[...], preferred_element_type=F32)
    kv = jnp.dot(ckv, wukv_ref[...], preferred_element_type=F32)
    scale = (MLA_NOPE + MLA_ROPE) ** -0.5 * LOG2_E
    for hh in range(MLA_HEADS):
        o = hh * MLA_QK_PAD
        q_ref[hh, :, :MLA_NOPE] = (q[:, o:o + MLA_NOPE] * scale).astype(BF16)
        qr = _rope(q[:, o + MLA_NOPE:o + MLA_QK_PAD], cos, sin)
        q_ref[hh, :, MLA_NOPE:] = (qr * scale).astype(BF16)
        o = hh * (MLA_NOPE + MLA_V)
        k_ref[hh, 0] = kv[:, o:o + MLA_NOPE].astype(BF16)
        k_ref[hh, 1] = kr
        _store_vt(vt_ref, hh, kv[:, o + MLA_NOPE:o + MLA_NOPE + MLA_V])


def _proj1(h, g, wd, qg, kvg, wuq, wukv, cos, sin, chunk, kv_chunk):
    b, s, d = h.shape
    nc = s // chunk
    r = kv_chunk // chunk
    const = lambda bi, i: (0, 0)
    return pl.pallas_call(
        _proj1_kernel,
        grid=(b, nc),
        in_specs=[
            pl.BlockSpec((None, chunk, d), lambda bi, i: (bi, i, 0)),
            pl.BlockSpec((1, d), const),
            pl.BlockSpec(wd.shape, const),
            pl.BlockSpec((1, Q_LORA), const),
            pl.BlockSpec((1, KV_LORA), const),
            pl.BlockSpec(wuq.shape, const),
            pl.BlockSpec(wukv.shape, const),
            pl.BlockSpec((chunk, LANES), lambda bi, i: (i, 0)),
            pl.BlockSpec((chunk, LANES), lambda bi, i: (i, 0)),
        ],
        out_specs=[
            pl.BlockSpec((None, MLA_HEADS, chunk, MLA_QK_PAD), lambda bi, i: (bi, 0, i, 0)),
            pl.BlockSpec((None, MLA_HEADS, MLA_QK_PAD // LANES, chunk, LANES),
                         lambda bi, i: (bi, 0, 0, i, 0)),
            pl.BlockSpec((None, MLA_HEADS, None, chunk // LANES, MLA_V, LANES),
                         lambda bi, i: (bi, 0, i // r, i % r, 0, 0)),
        ],
        out_shape=[
            jax.ShapeDtypeStruct((b, MLA_HEADS, s, MLA_QK_PAD), BF16),
            jax.ShapeDtypeStruct((b, MLA_HEADS, MLA_QK_PAD // LANES, s, LANES), BF16),
            jax.ShapeDtypeStruct((b, MLA_HEADS, s // kv_chunk, kv_chunk // LANES, MLA_V, LANES), BF16),
        ],
        compiler_params=_params(("parallel", "parallel")),
        name="proj1",
    )(h, g, wd, qg, kvg, wuq, wukv, cos, sin)


def _attn_kernel(q_ref, k_ref, vt_ref, o_ref, m_sc, l_sc, acc_sc, s_sc, mx_sc, qt_sc, *, group, bq,
                 chunk, n_chunks, unroll):
    n = group * bq
    gw = s_sc.shape[2]
    n_groups = n // gw
    ring = s_sc.shape[0]
    m_sc[...] = jnp.full(m_sc.shape, NEG_BIG, F32)
    l_sc[...] = jnp.zeros(l_sc.shape, F32)
    acc_sc[...] = jnp.zeros(acc_sc.shape, F32)
    for gi in range(n_groups):
        qg = q_ref[(gi * gw) // bq, pl.ds((gi * gw) % bq, gw), :]
        qt_sc[:, gi * gw:(gi + 1) * gw] = qg.astype(F32).T.astype(BF16)

    items = [(j, gi) for j in range(unroll) for gi in range(n_groups)]
    assert len(items) % ring == 0
    lookahead = min(ATTN_LOOKAHEAD, ring - 1)

    def scores(c, gi, slot):
        rows = pl.ds(pl.multiple_of(c * chunk, chunk), chunk)
        kc = jnp.concatenate([k_ref[i, rows, :] for i in range(k_ref.shape[0])], axis=1)
        s = jnp.dot(kc, qt_sc[:, gi * gw:(gi + 1) * gw], preferred_element_type=F32)
        s_sc[slot] = s
        mx_sc[slot] = jnp.max(s, axis=0, keepdims=True)

    for idx in range(lookahead):
        scores(items[idx][0], items[idx][1], idx)

    def body(t, carry):
        for idx, (j, gi) in enumerate(items):
            ahead = idx + lookahead
            ja, ga = items[ahead % len(items)]
            ca = jnp.minimum((t + ahead // len(items)) * unroll + ja, n_chunks - 1)
            scores(ca, ga, ahead % ring)

            cols = slice(gi * gw, (gi + 1) * gw)
            slot = idx % ring
            m_prev = m_sc[:, cols]
            m_new = jnp.maximum(m_prev, mx_sc[slot])
            alpha = jnp.exp2(m_prev - m_new)
            p = jnp.exp2(s_sc[slot] - m_new)
            l_sc[:, cols] = alpha * l_sc[:, cols] + jnp.sum(p, axis=0, keepdims=True)
            vt = jnp.concatenate([vt_ref[t * unroll + j, i] for i in range(vt_ref.shape[1])], axis=1)
            pv = jnp.dot(vt, p.astype(BF16), preferred_element_type=F32)
            acc_sc[:, cols] = alpha * acc_sc[:, cols] + pv
            m_sc[:, cols] = m_new
        return carry

    lax.fori_loop(0, n_chunks // unroll, body, 0)
    o = acc_sc[...] / l_sc[...]
    dv = acc_sc.shape[0]
    for gi in range(group):
        o_ref[:, gi * dv:(gi + 1) * dv] = o[:, gi * bq:(gi + 1) * bq].T.astype(o_ref.dtype)


def _attention(q, k, vt, *, group, bq, name):
    b, h, s, dqk = q.shape
    hkv = k.shape[1]
    n_chunks, slabs, dv, _ = vt.shape[2:]
    chunk = slabs * LANES
    n = group * bq
    gw = min(ATTN_COL_GROUP, bq)
    unroll = max(1, ATTN_TRIP_ITEMS // (n // gw))
    while n_chunks % unroll:
        unroll -= 1
    items = unroll * (n // gw)
    ring = ATTN_RING if items % ATTN_RING == 0 else items
    kern = functools.partial(_attn_kernel, group=group, bq=bq, chunk=chunk, n_chunks=n_chunks,
                             unroll=unroll)
    return pl.pallas_call(
        kern,
        grid=(b, hkv, s // bq),
        in_specs=[
            pl.BlockSpec((None, group, bq, dqk), lambda bi, hi, qi: (bi, hi, qi, 0)),
            pl.BlockSpec((None, None, dqk // LANES, s, LANES), lambda bi, hi, qi: (bi, hi, 0, 0, 0)),
            pl.BlockSpec((None, None, n_chunks, slabs, dv, LANES),
                         lambda bi, hi, qi: (bi, hi, 0, 0, 0, 0)),
        ],
        out_specs=pl.BlockSpec((None, bq, group * dv), lambda bi, hi, qi: (bi, qi, hi)),
        out_shape=jax.ShapeDtypeStruct((b, s, h * dv), BF16),
        scratch_shapes=[
            pltpu.VMEM((1, n), F32),
            pltpu.VMEM((1, n), F32),
            pltpu.VMEM((dv, n), F32),
            pltpu.VMEM((ring, chunk, gw), F32),
            pltpu.VMEM((ring, 1, gw), F32),
            pltpu.VMEM((dqk, n), BF16),
        ],
        compiler_params=_params(("parallel", "parallel", "arbitrary")),
        name=name,
    )(q, k, vt)


def _out0_kernel(h_ref, a_ref, gb_ref, cin_ref, cprev_ref, cnext_ref, cw_ref, w_ref, o_ref):
    i = pl.program_id(1)
    bm = cin_ref.shape[0]
    c = cin_ref[...].astype(F32)
    prev_row = cprev_ref[BF16_SUBLANES - 1:BF16_SUBLANES, :].astype(F32)
    next_row = cnext_ref[0:1, :].astype(F32)
    prev_row = jnp.where(i == 0, 0.0, prev_row)
    next_row = jnp.where(i == pl.num_programs(1) - 1, 0.0, next_row)
    rows = lax.broadcasted_iota(jnp.int32, (bm, 1), 0)
    c_m1 = jnp.where(rows == 0, prev_row, pltpu.roll(c, 1, 0))
    c_p1 = jnp.where(rows == bm - 1, next_row, pltpu.roll(c, bm - 1, 0))
    conv = cw_ref[0:1, :] * c_m1 + cw_ref[1:2, :] * c + cw_ref[2:3, :] * c_p1
    sconv = (gb_ref[...].astype(F32) * conv).astype(BF16)
    aw = a_ref.shape[1]
    y = jnp.dot(a_ref[...], w_ref[:aw, :], preferred_element_type=F32)
    y = y + jnp.dot(sconv, w_ref[aw:, :], preferred_element_type=F32)
    o_ref[...] = h_ref[...] + y


def _out0(h, attn, gb, cin, conv_w, w, bm):
    b, s, d = h.shape
    nb = s // bm
    r = bm // BF16_SUBLANES
    last = s // BF16_SUBLANES - 1
    tile = lambda width: pl.BlockSpec((None, bm, width), lambda bi, i: (bi, i, 0))
    return pl.pallas_call(
        _out0_kernel,
        grid=(b, nb),
        in_specs=[
            tile(d), tile(attn.shape[2]), tile(CONV_WIDTH), tile(CONV_WIDTH),
            pl.BlockSpec((None, BF16_SUBLANES, CONV_WIDTH),
                         lambda bi, i: (bi, jnp.maximum(i * r - 1, 0), 0)),
            pl.BlockSpec((None, BF16_SUBLANES, CONV_WIDTH),
                         lambda bi, i: (bi, jnp.minimum((i + 1) * r, last), 0)),
            pl.BlockSpec(conv_w.shape, lambda bi, i: (0, 0)),
            pl.BlockSpec(w.shape, lambda bi, i: (0, 0)),
        ],
        out_specs=tile(d),
        out_shape=jax.ShapeDtypeStruct(h.shape, F32),
        compiler_params=_params(("parallel", "parallel")),
        name="out0",
    )(h, attn, gb, cin, cin, cin, conv_w, w)


def _out1_kernel(h_ref, a_ref, w_ref, o_ref):
    o_ref[...] = h_ref[...] + jnp.dot(a_ref[...], w_ref[...], preferred_element_type=F32)


def _out1(h, attn, w, bm):
    t, d = h.shape
    return pl.pallas_call(
        _out1_kernel,
        grid=(t // bm,),
        in_specs=[
            pl.BlockSpec((bm, d), lambda i: (i, 0)),
            pl.BlockSpec((bm, attn.shape[1]), lambda i: (i, 0)),
            pl.BlockSpec(w.shape, lambda i: (0, 0)),
        ],
        out_specs=pl.BlockSpec((bm, d), lambda i: (i, 0)),
        out_shape=jax.ShapeDtypeStruct(h.shape, F32),
        compiler_params=_params(("parallel",)),
        name="out1",
    )(h, attn, w)


def _mlp_kernel(h_ref, g_ref, wup_ref, wdn_ref, fg_ref, o_ref, xn_sc, *, final_norm):
    f = pl.program_id(1)

    @pl.when(f == 0)
    def _():
        x = h_ref[...]
        xn_sc[...] = _rms(x, g_ref[...]).astype(BF16)
        o_ref[...] = x

    u = jnp.dot(xn_sc[...], wup_ref[...], preferred_element_type=F32)
    a = jnp.square(jnp.maximum(u, 0.0)).astype(BF16)
    o_ref[...] += jnp.dot(a, wdn_ref[...], preferred_element_type=F32)

    if final_norm:
        @pl.when(f == pl.num_programs(1) - 1)
        def _():
            o_ref[...] = _rms(o_ref[...], fg_ref[...])


def _mlp(h, g, wup, wdn, fg, *, layer, bm, bf, final_norm):
    t, d = h.shape
    dff = wup.shape[2]
    return pl.pallas_call(
        functools.partial(_mlp_kernel, final_norm=final_norm),
        grid=(t // bm, dff // bf),
        in_specs=[
            pl.BlockSpec((bm, d), lambda i, f: (i, 0)),
            pl.BlockSpec((1, d), lambda i, f: (0, 0)),
            pl.BlockSpec((None, d, bf), lambda i, f: (layer, 0, f)),
            pl.BlockSpec((None, bf, d), lambda i, f: (layer, f, 0)),
            pl.BlockSpec((1, d), lambda i, f: (0, 0)),
        ],
        out_specs=pl.BlockSpec((bm, d), lambda i, f: (i, 0)),
        out_shape=jax.ShapeDtypeStruct(h.shape, F32),
        scratch_shapes=[pltpu.VMEM((bm, d), BF16)],
        compiler_params=_params(("parallel", "arbitrary")),
        name="mlp_final" if final_norm else "mlp",
    )(h, g, wup, wdn, fg)


def _rope_angles(seq, rot_dim):
    rows = seq // GRID_W
    row = jnp.repeat(jnp.arange(rows, dtype=F32), GRID_W)
    col = jnp.tile(jnp.arange(GRID_W, dtype=F32), rows)
    axis_dim = rot_dim // 2
    inv_freq = ROPE_THETA ** (-jnp.arange(0, axis_dim, 2, dtype=F32) / axis_dim)
    return jnp.concatenate([row[:, None] * inv_freq, col[:, None] * inv_freq], axis=-1)


def _rope_tables(seq, rot_dim):
    ang = _rope_angles(seq, rot_dim)
    cos, sin = jnp.cos(ang), jnp.sin(ang)
    pad = jnp.zeros((seq, LANES // 2 - rot_dim // 2), F32)
    cos_t = jnp.concatenate([cos, pad, cos, pad], axis=-1)
    sin_t = jnp.concatenate([-sin, pad, sin, pad], axis=-1)
    return cos_t, sin_t


def _pair_split(w):
    return w[..., 0::2], w[..., 1::2]


def _pair_split_sources(rot_dim, pad):
    blank = [-1] * pad
    return list(range(0, rot_dim, 2)) + blank + list(range(1, rot_dim, 2)) + blank


_PAIR_SPLIT_128 = _pair_split_sources(HEAD_DIM, 0)
_ROPE_PAD_64 = _pair_split_sources(MLA_ROPE, LANES // 2 - MLA_ROPE // 2)
_MLA_Q_HEAD = list(range(MLA_NOPE)) + [MLA_NOPE + c if c >= 0 else -1 for c in _ROPE_PAD_64]


def _relayout(w, sources):
    sel = [[1.0 if src == k else 0.0 for src in sources] for k in range(w.shape[-1])]
    return jnp.dot(w, jnp.array(sel, BF16), preferred_element_type=BF16)


def kernel(x, even_norm_g, even_w_in, even_q_norm_g, even_k_norm_g, even_conv_w, even_w_out,
           odd_norm_g, odd_w_down, odd_q_lat_g, odd_kv_lat_g, odd_w_uq, odd_w_ukv, odd_w_o,
           mlp_norm_g, mlp_w_up, mlp_w_down, final_norm_g):
    b, s, d = x.shape
    t = _tiles(s)
    depth = mlp_norm_g.shape[0]
    cos_a, sin_a = _rope_tables(s, HEAD_DIM)
    cos_c, sin_c = _rope_tables(s, MLA_ROPE)
    fg = final_norm_g.reshape(1, d)
    w_up = mlp_w_up.astype(BF16)
    w_dn = mlp_w_down.astype(BF16)

    h = x
    for layer in range(depth):
        i = layer // 2
        if layer % 2 == 0:
            w_in = even_w_in[i].astype(BF16)
            wq = w_in[:, :ATTN_WIDTH].reshape(d, ATTN_HEADS, HEAD_DIM)
            wk = w_in[:, ATTN_WIDTH:ATTN_WIDTH + KV_WIDTH].reshape(d, ATTN_KV_HEADS, HEAD_DIM)
            wq = _relayout(wq, _PAIR_SPLIT_128).reshape(d, ATTN_WIDTH)
            wk = _relayout(wk, _PAIR_SPLIT_128).reshape(d, KV_WIDTH)
            w0 = jnp.concatenate([wq, wk, w_in[:, ATTN_WIDTH + KV_WIDTH:]], axis=-1)
            qg = jnp.concatenate(_pair_split(even_q_norm_g[i]), axis=-1).reshape(1, HEAD_DIM)
            kg = jnp.concatenate(_pair_split(even_k_norm_g[i]), axis=-1).reshape(1, HEAD_DIM)
            q, k, vt, gb, cin = _proj0(h, even_norm_g[i].reshape(1, d), w0, qg, kg,
                                       cos_a, sin_a, t["chunk"], t["kv_chunk"])
            attn = _attention(q, k, vt, group=ATTN_GROUP, bq=t["bq_gqa"], name="gqa_attn")
            h = _out0(h, attn, gb, cin, even_conv_w[i], even_w_out[i].astype(BF16), t["bm_out"])
        else:
            w_down = odd_w_down[i].astype(BF16)
            wd = jnp.concatenate([w_down[:, :Q_LORA + KV_LORA],
                                  _relayout(w_down[:, Q_LORA + KV_LORA:], _ROPE_PAD_64)], axis=-1)
            wuq = odd_w_uq[i].astype(BF16).reshape(Q_LORA, MLA_HEADS, MLA_NOPE + MLA_ROPE)
            wuq = _relayout(wuq, _MLA_Q_HEAD).reshape(Q_LORA, MLA_HEADS * MLA_QK_PAD)
            q, k, vt = _proj1(h, odd_norm_g[i].reshape(1, d), wd,
                              odd_q_lat_g[i].reshape(1, Q_LORA), odd_kv_lat_g[i].reshape(1, KV_LORA),
                              wuq, odd_w_ukv[i].astype(BF16), cos_c, sin_c, t["chunk"],
                              t["kv_chunk"])
            attn = _attention(q, k, vt, group=1, bq=t["bq_mla"], name="mla_attn")
            h = _out1(h.reshape(b * s, d), attn.reshape(b * s, -1), odd_w_o[i].astype(BF16),
                      t["bm_out"]).reshape(b, s, d)
        h = _mlp(h.reshape(b * s, d), mlp_norm_g[layer].reshape(1, d), w_up, w_dn, fg,
                 layer=layer, bm=t["bm_mlp"], bf=t["bf_mlp"],
                 final_norm=(layer == depth - 1)).reshape(b, s, d)
    return h
```

```python
import functools

import jax
import jax.numpy as jnp
from jax import lax
from jax.experimental import pallas as pl
from jax.experimental.pallas import tpu as pltpu

F32 = jnp.float32
BF16 = jnp.bfloat16

NORM_EPS = 1e-6
ROPE_THETA = 10000.0
GRID_W = 64

HEAD_DIM = 128
ATTN_HEADS = 8
ATTN_KV_HEADS = 2
ATTN_GROUP = ATTN_HEADS // ATTN_KV_HEADS
ATTN_WIDTH = ATTN_HEADS * HEAD_DIM
KV_WIDTH = ATTN_KV_HEADS * HEAD_DIM
CONV_WIDTH = 1024

MLA_HEADS = 16
MLA_NOPE = 128
MLA_ROPE = 64
MLA_V = 128
Q_LORA = 512
KV_LORA = 512
MLA_QK_PAD = 256
MLA_DOWN_PAD = Q_LORA + KV_LORA + 128

LANES = 128
BF16_SUBLANES = 16
VMEM_LIMIT_BYTES = 56 * 1024 * 1024
NEG_BIG = -1e30
LOG2_E = 1.4426950408889634
ATTN_COL_GROUP = 512
ATTN_TRIP_ITEMS = 32
ATTN_LOOKAHEAD = 2
ATTN_RING = 4


def _tiles(seq):
    chunk = min(512, seq)
    return dict(
        chunk=chunk,
        kv_chunk=min(512, seq),
        bm_mlp=min(1024, seq),
        bf_mlp=512,
        bm_out=min(512, seq),
        bq_gqa=min(1024, seq),
        bq_mla=min(4096, seq),
    )


def _params(sem):
    return pltpu.CompilerParams(dimension_semantics=sem, vmem_limit_bytes=VMEM_LIMIT_BYTES)


def _rms(x, g):
    return x * lax.rsqrt(jnp.mean(x * x, axis=-1, keepdims=True) + NORM_EPS) * g


def _store_vt(vt_ref, head, v):
    vt = v.T.astype(BF16)
    for i in range(vt_ref.shape[1]):
        vt_ref[head, i] = vt[:, i * LANES:(i + 1) * LANES]


def _rope(x, cos, sin):
    return x * cos + pltpu.roll(x, LANES // 2, 1) * sin


def _proj0_kernel(h_ref, g_ref, w_ref, qg_ref, kg_ref, cos_ref, sin_ref,
                  q_ref, k_ref, vt_ref, gb_ref, cin_ref):
    xn = _rms(h_ref[...], g_ref[...]).astype(BF16)
    y = jnp.dot(xn, w_ref[...], preferred_element_type=F32)
    cos = cos_ref[...]
    sin = sin_ref[...]
    scale = HEAD_DIM ** -0.5 * LOG2_E
    for hh in range(ATTN_HEADS):
        yh = _rms(y[:, hh * HEAD_DIM:(hh + 1) * HEAD_DIM], qg_ref[...])
        q_ref[hh] = (_rope(yh, cos, sin) * scale).astype(BF16)
    o = ATTN_WIDTH
    for hh in range(ATTN_KV_HEADS):
        yh = _rms(y[:, o + hh * HEAD_DIM:o + (hh + 1) * HEAD_DIM], kg_ref[...])
        k_ref[hh, 0] = _rope(yh, cos, sin).astype(BF16)
    o += KV_WIDTH
    for hh in range(ATTN_KV_HEADS):
        _store_vt(vt_ref, hh, y[:, o + hh * HEAD_DIM:o + (hh + 1) * HEAD_DIM])
    o += KV_WIDTH
    gb_ref[...] = y[:, o:o + CONV_WIDTH].astype(BF16)
    o += CONV_WIDTH
    cin_ref[...] = (y[:, o:o + CONV_WIDTH] * y[:, o + CONV_WIDTH:o + 2 * CONV_WIDTH]).astype(BF16)


def _proj0(h, g, w, qg, kg, cos, sin, chunk, kv_chunk):
    b, s, d = h.shape
    n = w.shape[1]
    nc = s // chunk
    r = kv_chunk // chunk
    return pl.pallas_call(
        _proj0_kernel,
        grid=(b, nc),
        in_specs=[
            pl.BlockSpec((None, chunk, d), lambda bi, i: (bi, i, 0)),
            pl.BlockSpec((1, d), lambda bi, i: (0, 0)),
            pl.BlockSpec((d, n), lambda bi, i: (0, 0)),
            pl.BlockSpec((1, HEAD_DIM), lambda bi, i: (0, 0)),
            pl.BlockSpec((1, HEAD_DIM), lambda bi, i: (0, 0)),
            pl.BlockSpec((chunk, HEAD_DIM), lambda bi, i: (i, 0)),
            pl.BlockSpec((chunk, HEAD_DIM), lambda bi, i: (i, 0)),
        ],
        out_specs=[
            pl.BlockSpec((None, ATTN_HEADS, chunk, HEAD_DIM), lambda bi, i: (bi, 0, i, 0)),
            pl.BlockSpec((None, ATTN_KV_HEADS, 1, chunk, LANES), lambda bi, i: (bi, 0, 0, i, 0)),
            pl.BlockSpec((None, ATTN_KV_HEADS, None, chunk // LANES, HEAD_DIM, LANES),
                         lambda bi, i: (bi, 0, i // r, i % r, 0, 0)),
            pl.BlockSpec((None, chunk, CONV_WIDTH), lambda bi, i: (bi, i, 0)),
            pl.BlockSpec((None, chunk, CONV_WIDTH), lambda bi, i: (bi, i, 0)),
        ],
        out_shape=[
            jax.ShapeDtypeStruct((b, ATTN_HEADS, s, HEAD_DIM), BF16),
            jax.ShapeDtypeStruct((b, ATTN_KV_HEADS, 1, s, LANES), BF16),
            jax.ShapeDtypeStruct((b, ATTN_KV_HEADS, s // kv_chunk, kv_chunk // LANES, HEAD_DIM, LANES),
                                 BF16),
            jax.ShapeDtypeStruct((b, s, CONV_WIDTH), BF16),
            jax.ShapeDtypeStruct((b, s, CONV_WIDTH), BF16),
        ],
        compiler_params=_params(("parallel", "parallel")),
        name="proj0",
    )(h, g, w, qg, kg, cos, sin)


def _proj1_kernel(h_ref, g_ref, wd_ref, qg_ref, kvg_ref, wuq_ref, wukv_ref, cos_ref, sin_ref,
                  q_ref, k_ref, vt_ref):
    xn = _rms(h_ref[...], g_ref[...]).astype(BF16)
    lat = jnp.dot(xn, wd_ref[...], preferred_element_type=F32)
    cq = _rms(lat[:, :Q_LORA], qg_ref[...]).astype(BF16)
    ckv = _rms(lat[:, Q_LORA:Q_LORA + KV_LORA], kvg_ref[...]).astype(BF16)
    cos = cos_ref[...]
    sin = sin_ref[...]
    kr = _rope(lat[:, Q_LORA + KV_LORA:], cos, sin).astype(BF16)
    q = jnp.dot(cq, wuq_ref[...], preferred_element_type=F32)
    kv = jnp.dot(ckv, wukv_ref[...], preferred_element_type=F32)
    scale = (MLA_NOPE + MLA_ROPE) ** -0.5 * LOG2_E
    for hh in range(MLA_HEADS):
        o = hh * MLA_QK_PAD
        q_ref[hh, :, :MLA_NOPE] = (q[:, o:o + MLA_NOPE] * scale).astype(BF16)
        qr = _rope(q[:, o + MLA_NOPE:o + MLA_QK_PAD], cos, sin)
        q_ref[hh, :, MLA_NOPE:] = (qr * scale).astype(BF16)
        o = hh * (MLA_NOPE + MLA_V)
        k_ref[hh, 0] = kv[:, o:o + MLA_NOPE].astype(BF16)
        k_ref[hh, 1] = kr
        _store_vt(vt_ref, hh, kv[:, o + MLA_NOPE:o + MLA_NOPE + MLA_V])


def _proj1(h, g, wd, qg, kvg, wuq, wukv, cos, sin, chunk, kv_chunk):
    b, s, d = h.shape
    nc = s // chunk
    r = kv_chunk // chunk
    const = lambda bi, i: (0, 0)
    return pl.pallas_call(
        _proj1_kernel,
        grid=(b, nc),
        in_specs=[
            pl.BlockSpec((None, chunk, d), lambda bi, i: (bi, i, 0)),
            pl.BlockSpec((1, d), const),
            pl.BlockSpec(wd.shape, const),
            pl.BlockSpec((1, Q_LORA), const),
            pl.BlockSpec((1, KV_LORA), const),
            pl.BlockSpec(wuq.shape, const),
            pl.BlockSpec(wukv.shape, const),
            pl.BlockSpec((chunk, LANES), lambda bi, i: (i, 0)),
            pl.BlockSpec((chunk, LANES), lambda bi, i: (i, 0)),
        ],
        out_specs=[
            pl.BlockSpec((None, MLA_HEADS, chunk, MLA_QK_PAD), lambda bi, i: (bi, 0, i, 0)),
            pl.BlockSpec((None, MLA_HEADS, MLA_QK_PAD // LANES, chunk, LANES),
                         lambda bi, i: (bi, 0, 0, i, 0)),
            pl.BlockSpec((None, MLA_HEADS, None, chunk // LANES, MLA_V, LANES),
                         lambda bi, i: (bi, 0, i // r, i % r, 0, 0)),
        ],
        out_shape=[
            jax.ShapeDtypeStruct((b, MLA_HEADS, s, MLA_QK_PAD), BF16),
            jax.ShapeDtypeStruct((b, MLA_HEADS, MLA_QK_PAD // LANES, s, LANES), BF16),
            jax.ShapeDtypeStruct((b, MLA_HEADS, s // kv_chunk, kv_chunk // LANES, MLA_V, LANES), BF16),
        ],
        compiler_params=_params(("parallel", "parallel")),
        name="proj1",
    )(h, g, wd, qg, kvg, wuq, wukv, cos, sin)


def _attn_kernel(q_ref, k_ref, vt_ref, o_ref, m_sc, acc_sc, s_sc, mx_sc, qt_sc, *, group, bq,
                 chunk, n_chunks, unroll):
    n = group * bq
    gw = s_sc.shape[2]
    n_groups = n // gw
    ring = s_sc.shape[0]
    m_sc[...] = jnp.full(m_sc.shape, NEG_BIG, F32)
    acc_sc[...] = jnp.zeros(acc_sc.shape, F32)
    for gi in range(n_groups):
        qg = q_ref[(gi * gw) // bq, pl.ds((gi * gw) % bq, gw), :]
        qt_sc[:, gi * gw:(gi + 1) * gw] = qg.astype(F32).T.astype(BF16)

    ones_rows = (lax.broadcasted_iota(jnp.int32, (BF16_SUBLANES, chunk), 0) == 0).astype(BF16)

    items = [(j, gi) for j in range(unroll) for gi in range(n_groups)]
    assert len(items) % ring == 0
    lookahead = min(ATTN_LOOKAHEAD, ring - 1)

    def scores(c, gi, slot):
        rows = pl.ds(pl.multiple_of(c * chunk, chunk), chunk)
        kc = jnp.concatenate([k_ref[i, rows, :] for i in range(k_ref.shape[0])], axis=1)
        s = jnp.dot(kc, qt_sc[:, gi * gw:(gi + 1) * gw], preferred_element_type=F32)
        s_sc[slot] = s
        mx_sc[slot] = jnp.max(s, axis=0, keepdims=True)

    for idx in range(lookahead):
        scores(items[idx][0], items[idx][1], idx)

    def body(t, carry):
        for idx, (j, gi) in enumerate(items):
            ahead = idx + lookahead
            ja, ga = items[ahead % len(items)]
            ca = jnp.minimum((t + ahead // len(items)) * unroll + ja, n_chunks - 1)
            scores(ca, ga, ahead % ring)

            cols = slice(gi * gw, (gi + 1) * gw)
            slot = idx % ring
            m_prev = m_sc[:, cols]
            m_new = jnp.maximum(m_prev, mx_sc[slot])
            alpha = jnp.exp2(m_prev - m_new)
            p = jnp.exp2(s_sc[slot] - m_new)
            vt = jnp.concatenate([vt_ref[t * unroll + j, i] for i in range(vt_ref.shape[1])], axis=1)
            vt = jnp.concatenate([vt, ones_rows], axis=0)
            pv = jnp.dot(vt, p.astype(BF16), preferred_element_type=F32)
            acc_sc[:, cols] = alpha * acc_sc[:, cols] + pv
            m_sc[:, cols] = m_new
        return carry

    lax.fori_loop(0, n_chunks // unroll, body, 0)
    dv = acc_sc.shape[0] - BF16_SUBLANES
    o = acc_sc[:dv, :] / acc_sc[dv:dv + 1, :]
    for gi in range(group):
        o_ref[:, gi * dv:(gi + 1) * dv] = o[:, gi * bq:(gi + 1) * bq].T.astype(o_ref.dtype)


def _attention(q, k, vt, *, group, bq, name):
    b, h, s, dqk = q.shape
    hkv = k.shape[1]
    n_chunks, slabs, dv, _ = vt.shape[2:]
    chunk = slabs * LANES
    n = group * bq
    gw = min(ATTN_COL_GROUP, bq)
    unroll = max(1, ATTN_TRIP_ITEMS // (n // gw))
    while n_chunks % unroll:
        unroll -= 1
    items = unroll * (n // gw)
    ring = ATTN_RING if items % ATTN_RING == 0 else items
    kern = functools.partial(_attn_kernel, group=group, bq=bq, chunk=chunk, n_chunks=n_chunks,
                             unroll=unroll)
    return pl.pallas_call(
        kern,
        grid=(b, hkv, s // bq),
        in_specs=[
            pl.BlockSpec((None, group, bq, dqk), lambda bi, hi, qi: (bi, hi, qi, 0)),
            pl.BlockSpec((None, None, dqk // LANES, s, LANES), lambda bi, hi, qi: (bi, hi, 0, 0, 0)),
            pl.BlockSpec((None, None, n_chunks, slabs, dv, LANES),
                         lambda bi, hi, qi: (bi, hi, 0, 0, 0, 0)),
        ],
        out_specs=pl.BlockSpec((None, bq, group * dv), lambda bi, hi, qi: (bi, qi, hi)),
        out_shape=jax.ShapeDtypeStruct((b, s, h * dv), BF16),
        scratch_shapes=[
            pltpu.VMEM((1, n), F32),
            pltpu.VMEM((dv + BF16_SUBLANES, n), F32),
            pltpu.VMEM((ring, chunk, gw), F32),
            pltpu.VMEM((ring, 1, gw), F32),
            pltpu.VMEM((dqk, n), BF16),
        ],
        compiler_params=_params(("parallel", "parallel", "arbitrary")),
        name=name,
    )(q, k, vt)


def _out0_kernel(h_ref, a_ref, gb_ref, cin_ref, cprev_ref, cnext_ref, cw_ref, w_ref, o_ref):
    i = pl.program_id(1)
    bm = cin_ref.shape[0]
    c = cin_ref[...].astype(F32)
    prev_row = cprev_ref[BF16_SUBLANES - 1:BF16_SUBLANES, :].astype(F32)
    next_row = cnext_ref[0:1, :].astype(F32)
    prev_row = jnp.where(i == 0, 0.0, prev_row)
    next_row = jnp.where(i == pl.num_programs(1) - 1, 0.0, next_row)
    rows = lax.broadcasted_iota(jnp.int32, (bm, 1), 0)
    c_m1 = jnp.where(rows == 0, prev_row, pltpu.roll(c, 1, 0))
    c_p1 = jnp.where(rows == bm - 1, next_row, pltpu.roll(c, bm - 1, 0))
    conv = cw_ref[0:1, :] * c_m1 + cw_ref[1:2, :] * c + cw_ref[2:3, :] * c_p1
    sconv = (gb_ref[...].astype(F32) * conv).astype(BF16)
    aw = a_ref.shape[1]
    y = jnp.dot(a_ref[...], w_ref[:aw, :], preferred_element_type=F32)
    y = y + jnp.dot(sconv, w_ref[aw:, :], preferred_element_type=F32)
    o_ref[...] = h_ref[...] + y


def _out0(h, attn, gb, cin, conv_w, w, bm):
    b, s, d = h.shape
    nb = s // bm
    r = bm // BF16_SUBLANES
    last = s // BF16_SUBLANES - 1
    tile = lambda width: pl.BlockSpec((None, bm, width), lambda bi, i: (bi, i, 0))
    return pl.pallas_call(
        _out0_kernel,
        grid=(b, nb),
        in_specs=[
            tile(d), tile(attn.shape[2]), tile(CONV_WIDTH), tile(CONV_WIDTH),
            pl.BlockSpec((None, BF16_SUBLANES, CONV_WIDTH),
                         lambda bi, i: (bi, jnp.maximum(i * r - 1, 0), 0)),
            pl.BlockSpec((None, BF16_SUBLANES, CONV_WIDTH),
                         lambda bi, i: (bi, jnp.minimum((i + 1) * r, last), 0)),
            pl.BlockSpec(conv_w.shape, lambda bi, i: (0, 0)),
            pl.BlockSpec(w.shape, lambda bi, i: (0, 0)),
        ],
        out_specs=tile(d),
        out_shape=jax.ShapeDtypeStruct(h.shape, F32),
        compiler_params=_params(("parallel", "parallel")),
        name="out0",
    )(h, attn, gb, cin, cin, cin, conv_w, w)


def _out1_kernel(h_ref, a_ref, w_ref, o_ref):
    o_ref[...] = h_ref[...] + jnp.dot(a_ref[...], w_ref[...], preferred_element_type=F32)


def _out1(h, attn, w, bm):
    t, d = h.shape
    return pl.pallas_call(
        _out1_kernel,
        grid=(t // bm,),
        in_specs=[
            pl.BlockSpec((bm, d), lambda i: (i, 0)),
            pl.BlockSpec((bm, attn.shape[1]), lambda i: (i, 0)),
            pl.BlockSpec(w.shape, lambda i: (0, 0)),
        ],
        out_specs=pl.BlockSpec((bm, d), lambda i: (i, 0)),
        out_shape=jax.ShapeDtypeStruct(h.shape, F32),
        compiler_params=_params(("parallel",)),
        name="out1",
    )(h, attn, w)


def _mlp_kernel(h_ref, g_ref, wup_ref, wdn_ref, fg_ref, o_ref, xn_sc, *, final_norm):
    f = pl.program_id(1)

    @pl.when(f == 0)
    def _():
        x = h_ref[...]
        xn_sc[...] = _rms(x, g_ref[...]).astype(BF16)
        o_ref[...] = x

    u = jnp.dot(xn_sc[...], wup_ref[...], preferred_element_type=F32)
    a = jnp.square(jnp.maximum(u, 0.0)).astype(BF16)
    o_ref[...] += jnp.dot(a, wdn_ref[...], preferred_element_type=F32)

    if final_norm:
        @pl.when(f == pl.num_programs(1) - 1)
        def _():
            o_ref[...] = _rms(o_ref[...], fg_ref[...])


def _mlp(h, g, wup, wdn, fg, *, layer, bm, bf, final_norm):
    t, d = h.shape
    dff = wup.shape[2]
    return pl.pallas_call(
        functools.partial(_mlp_kernel, final_norm=final_norm),
        grid=(t // bm, dff // bf),
        in_specs=[
            pl.BlockSpec((bm, d), lambda i, f: (i, 0)),
            pl.BlockSpec((1, d), lambda i, f: (0, 0)),
            pl.BlockSpec((None, d, bf), lambda i, f: (layer, 0, f)),
            pl.BlockSpec((None, bf, d), lambda i, f: (layer, f, 0)),
            pl.BlockSpec((1, d), lambda i, f: (0, 0)),
        ],
        out_specs=pl.BlockSpec((bm, d), lambda i, f: (i, 0)),
        out_shape=jax.ShapeDtypeStruct(h.shape, F32),
        scratch_shapes=[pltpu.VMEM((bm, d), BF16)],
        compiler_params=_params(("parallel", "arbitrary")),
        name="mlp_final" if final_norm else "mlp",
    )(h, g, wup, wdn, fg)


def _rope_angles(seq, rot_dim):
    rows = seq // GRID_W
    row = jnp.repeat(jnp.arange(rows, dtype=F32), GRID_W)
    col = jnp.tile(jnp.arange(GRID_W, dtype=F32), rows)
    axis_dim = rot_dim // 2
    inv_freq = ROPE_THETA ** (-jnp.arange(0, axis_dim, 2, dtype=F32) / axis_dim)
    return jnp.concatenate([row[:, None] * inv_freq, col[:, None] * inv_freq], axis=-1)


def _rope_tables(seq, rot_dim):
    ang = _rope_angles(seq, rot_dim)
    cos, sin = jnp.cos(ang), jnp.sin(ang)
    pad = jnp.zeros((seq, LANES // 2 - rot_dim // 2), F32)
    cos_t = jnp.concatenate([cos, pad, cos, pad], axis=-1)
    sin_t = jnp.concatenate([-sin, pad, sin, pad], axis=-1)
    return cos_t, sin_t


def _pair_split(w):
    return w[..., 0::2], w[..., 1::2]


def _pair_split_sources(rot_dim, pad):
    blank = [-1] * pad
    return list(range(0, rot_dim, 2)) + blank + list(range(1, rot_dim, 2)) + blank


_PAIR_SPLIT_128 = _pair_split_sources(HEAD_DIM, 0)
_ROPE_PAD_64 = _pair_split_sources(MLA_ROPE, LANES // 2 - MLA_ROPE // 2)
_MLA_Q_HEAD = list(range(MLA_NOPE)) + [MLA_NOPE + c if c >= 0 else -1 for c in _ROPE_PAD_64]


def _relayout(w, sources):
    sel = [[1.0 if src == k else 0.0 for src in sources] for k in range(w.shape[-1])]
    return jnp.dot(w, jnp.array(sel, BF16), preferred_element_type=BF16)


def kernel(x, even_norm_g, even_w_in, even_q_norm_g, even_k_norm_g, even_conv_w, even_w_out,
           odd_norm_g, odd_w_down, odd_q_lat_g, odd_kv_lat_g, odd_w_uq, odd_w_ukv, odd_w_o,
           mlp_norm_g, mlp_w_up, mlp_w_down, final_norm_g):
    b, s, d = x.shape
    t = _tiles(s)
    depth = mlp_norm_g.shape[0]
    cos_a, sin_a = _rope_tables(s, HEAD_DIM)
    cos_c, sin_c = _rope_tables(s, MLA_ROPE)
    fg = final_norm_g.reshape(1, d)
    w_up = mlp_w_up.astype(BF16)
    w_dn = mlp_w_down.astype(BF16)

    h = x
    for layer in range(depth):
        i = layer // 2
        if layer % 2 == 0:
            w_in = even_w_in[i].astype(BF16)
            wq = w_in[:, :ATTN_WIDTH].reshape(d, ATTN_HEADS, HEAD_DIM)
            wk = w_in[:, ATTN_WIDTH:ATTN_WIDTH + KV_WIDTH].reshape(d, ATTN_KV_HEADS, HEAD_DIM)
            wq = _relayout(wq, _PAIR_SPLIT_128).reshape(d, ATTN_WIDTH)
            wk = _relayout(wk, _PAIR_SPLIT_128).reshape(d, KV_WIDTH)
            w0 = jnp.concatenate([wq, wk, w_in[:, ATTN_WIDTH + KV_WIDTH:]], axis=-1)
            qg = jnp.concatenate(_pair_split(even_q_norm_g[i]), axis=-1).reshape(1, HEAD_DIM)
            kg = jnp.concatenate(_pair_split(even_k_norm_g[i]), axis=-1).reshape(1, HEAD_DIM)
            q, k, vt, gb, cin = _proj0(h, even_norm_g[i].reshape(1, d), w0, qg, kg,
                                       cos_a, sin_a, t["chunk"], t["kv_chunk"])
            attn = _attention(q, k, vt, group=ATTN_GROUP, bq=t["bq_gqa"], name="gqa_attn")
            h = _out0(h, attn, gb, cin, even_conv_w[i], even_w_out[i].astype(BF16), t["bm_out"])
        else:
            w_down = odd_w_down[i].astype(BF16)
            wd = jnp.concatenate([w_down[:, :Q_LORA + KV_LORA],
                                  _relayout(w_down[:, Q_LORA + KV_LORA:], _ROPE_PAD_64)], axis=-1)
            wuq = odd_w_uq[i].astype(BF16).reshape(Q_LORA, MLA_HEADS, MLA_NOPE + MLA_ROPE)
            wuq = _relayout(wuq, _MLA_Q_HEAD).reshape(Q_LORA, MLA_HEADS * MLA_QK_PAD)
            q, k, vt = _proj1(h, odd_norm_g[i].reshape(1, d), wd,
                              odd_q_lat_g[i].reshape(1, Q_LORA), odd_kv_lat_g[i].reshape(1, KV_LORA),
                              wuq, odd_w_ukv[i].astype(BF16), cos_c, sin_c, t["chunk"],
                              t["kv_chunk"])
            attn = _attention(q, k, vt, group=1, bq=t["bq_mla"], name="mla_attn")
            h = _out1(h.reshape(b * s, d), attn.reshape(b * s, -1), odd_w_o[i].astype(BF16),
                      t["bm_out"]).reshape(b, s, d)
        h = _mlp(h.reshape(b * s, d), mlp_norm_g[layer].reshape(1, d), w_up, w_dn, fg,
                 layer=layer, bm=t["bm_mlp"], bf=t["bf_mlp"],
                 final_norm=(layer == depth - 1)).reshape(b, s, d)
    return h
```

```python
import functools

import jax
import jax.numpy as jnp
from jax import lax
from jax.experimental import pallas as pl
from jax.experimental.pallas import tpu as pltpu

F32 = jnp.float32
BF16 = jnp.bfloat16

NORM_EPS = 1e-6
ROPE_THETA = 10000.0
GRID_W = 64

HEAD_DIM = 128
ATTN_HEADS = 8
ATTN_KV_HEADS = 2
ATTN_GROUP = ATTN_HEADS // ATTN_KV_HEADS
ATTN_WIDTH = ATTN_HEADS * HEAD_DIM
KV_WIDTH = ATTN_KV_HEADS * HEAD_DIM
CONV_WIDTH = 1024

MLA_HEADS = 16
MLA_NOPE = 128
MLA_ROPE = 64
MLA_V = 128
Q_LORA = 512
KV_LORA = 512
MLA_QK_PAD = 256

LANES = 128
BF16_SUBLANES = 16
VMEM_LIMIT_BYTES = 56 * 1024 * 1024
NEG_BIG = -1e30
LOG2_E = 1.4426950408889634
ATTN_COL_GROUP = 512
ATTN_TRIP_ITEMS = 32
ATTN_LOOKAHEAD = 2
ATTN_RING = 4


def _tiles(seq):
    return dict(
        chunk=min(512, seq),
        kv_chunk=min(512, seq),
        bm_mlp=min(1024, seq),
        bf_mlp=512,
        bm_out=min(512, seq),
        bq_gqa=min(1024, seq),
        bq_mla=min(4096, seq),
    )


def _params(sem):
    return pltpu.CompilerParams(dimension_semantics=sem, vmem_limit_bytes=VMEM_LIMIT_BYTES)


def _rms(x, g):
    return x * lax.rsqrt(jnp.mean(x * x, axis=-1, keepdims=True) + NORM_EPS) * g


def _store_vt(vt_ref, head, v):
    vt = v.T.astype(BF16)
    for i in range(vt_ref.shape[1]):
        vt_ref[head, i] = vt[:, i * LANES:(i + 1) * LANES]


def _rope(x, cos, sin):
    return x * cos + pltpu.roll(x, LANES // 2, 1) * sin


def _proj0_kernel(h_ref, g_ref, w_ref, qg_ref, kg_ref, cos_ref, sin_ref,
                  q_ref, k_ref, vt_ref, gb_ref, cin_ref):
    xn = _rms(h_ref[...], g_ref[...]).astype(BF16)
    y = jnp.dot(xn, w_ref[...], preferred_element_type=F32)
    cos = cos_ref[...]
    sin = sin_ref[...]
    scale = HEAD_DIM ** -0.5 * LOG2_E
    for hh in range(ATTN_HEADS):
        yh = _rms(y[:, hh * HEAD_DIM:(hh + 1) * HEAD_DIM], qg_ref[...])
        q_ref[hh] = (_rope(yh, cos, sin) * scale).astype(BF16)
    o = ATTN_WIDTH
    for hh in range(ATTN_KV_HEADS):
        yh = _rms(y[:, o + hh * HEAD_DIM:o + (hh + 1) * HEAD_DIM], kg_ref[...])
        k_ref[hh, 0] = _rope(yh, cos, sin).astype(BF16)
    o += KV_WIDTH
    for hh in range(ATTN_KV_HEADS):
        _store_vt(vt_ref, hh, y[:, o + hh * HEAD_DIM:o + (hh + 1) * HEAD_DIM])
    o += KV_WIDTH
    gb_ref[...] = y[:, o:o + CONV_WIDTH].astype(BF16)
    o += CONV_WIDTH
    cin_ref[...] = (y[:, o:o + CONV_WIDTH] * y[:, o + CONV_WIDTH:o + 2 * CONV_WIDTH]).astype(BF16)


def _proj0(h, g, w, qg, kg, cos, sin, chunk, kv_chunk):
    b, s, d = h.shape
    n = w.shape[1]
    nc = s // chunk
    r = kv_chunk // chunk
    return pl.pallas_call(
        _proj0_kernel,
        grid=(b, nc),
        in_specs=[
            pl.BlockSpec((None, chunk, d), lambda bi, i: (bi, i, 0)),
            pl.BlockSpec((1, d), lambda bi, i: (0, 0)),
            pl.BlockSpec((d, n), lambda bi, i: (0, 0)),
            pl.BlockSpec((1, HEAD_DIM), lambda bi, i: (0, 0)),
            pl.BlockSpec((1, HEAD_DIM), lambda bi, i: (0, 0)),
            pl.BlockSpec((chunk, HEAD_DIM), lambda bi, i: (i, 0)),
            pl.BlockSpec((chunk, HEAD_DIM), lambda bi, i: (i, 0)),
        ],
        out_specs=[
            pl.BlockSpec((None, ATTN_HEADS, chunk, HEAD_DIM), lambda bi, i: (bi, 0, i, 0)),
            pl.BlockSpec((None, ATTN_KV_HEADS, 1, chunk, LANES), lambda bi, i: (bi, 0, 0, i, 0)),
            pl.BlockSpec((None, ATTN_KV_HEADS, None, chunk // LANES, HEAD_DIM, LANES),
                         lambda bi, i: (bi, 0, i // r, i % r, 0, 0)),
            pl.BlockSpec((None, chunk, CONV_WIDTH), lambda bi, i: (bi, i, 0)),
            pl.BlockSpec((None, chunk, CONV_WIDTH), lambda bi, i: (bi, i, 0)),
        ],
        out_shape=[
            jax.ShapeDtypeStruct((b, ATTN_HEADS, s, HEAD_DIM), BF16),
            jax.ShapeDtypeStruct((b, ATTN_KV_HEADS, 1, s, LANES), BF16),
            jax.ShapeDtypeStruct((b, ATTN_KV_HEADS, s // kv_chunk, kv_chunk // LANES, HEAD_DIM, LANES),
                                 BF16),
            jax.ShapeDtypeStruct((b, s, CONV_WIDTH), BF16),
            jax.ShapeDtypeStruct((b, s, CONV_WIDTH), BF16),
        ],
        compiler_params=_params(("parallel", "parallel")),
        name="proj0",
    )(h, g, w, qg, kg, cos, sin)


def _proj1_kernel(h_ref, g_ref, wd_ref, qg_ref, kvg_ref, wuq_ref, wukv_ref, cos_ref, sin_ref,
                  q_ref, k_ref, vt_ref):
    xn = _rms(h_ref[...], g_ref[...]).astype(BF16)
    lat = jnp.dot(xn, wd_ref[...], preferred_element_type=F32)
    cq = _rms(lat[:, :Q_LORA], qg_ref[...]).astype(BF16)
    ckv = _rms(lat[:, Q_LORA:Q_LORA + KV_LORA], kvg_ref[...]).astype(BF16)
    cos = cos_ref[...]
    sin = sin_ref[...]
    kr = _rope(lat[:, Q_LORA + KV_LORA:], cos, sin).astype(BF16)
    q = jnp.dot(cq, wuq_ref[...], preferred_element_type=F32)
    kv = jnp.dot(ckv, wukv_ref[...], preferred_element_type=F32)
    scale = (MLA_NOPE + MLA_ROPE) ** -0.5 * LOG2_E
    for hh in range(MLA_HEADS):
        o = hh * MLA_QK_PAD
        q_ref[hh, :, :MLA_NOPE] = (q[:, o:o + MLA_NOPE] * scale).astype(BF16)
        qr = _rope(q[:, o + MLA_NOPE:o + MLA_QK_PAD], cos, sin)
        q_ref[hh, :, MLA_NOPE:] = (qr * scale).astype(BF16)
        o = hh * (MLA_NOPE + MLA_V)
        k_ref[hh, 0] = kv[:, o:o + MLA_NOPE].astype(BF16)
        k_ref[hh, 1] = kr
        _store_vt(vt_ref, hh, kv[:, o + MLA_NOPE:o + MLA_NOPE + MLA_V])


def _proj1(h, g, wd, qg, kvg, wuq, wukv, cos, sin, chunk, kv_chunk):
    b, s, d = h.shape
    nc = s // chunk
    r = kv_chunk // chunk
    const = lambda bi, i: (0, 0)
    return pl.pallas_call(
        _proj1_kernel,
        grid=(b, nc),
        in_specs=[
            pl.BlockSpec((None, chunk, d), lambda bi, i: (bi, i, 0)),
            pl.BlockSpec((1, d), const),
            pl.BlockSpec(wd.shape, const),
            pl.BlockSpec((1, Q_LORA), const),
            pl.BlockSpec((1, KV_LORA), const),
            pl.BlockSpec(wuq.shape, const),
            pl.BlockSpec(wukv.shape, const),
            pl.BlockSpec((chunk, LANES), lambda bi, i: (i, 0)),
            pl.BlockSpec((chunk, LANES), lambda bi, i: (i, 0)),
        ],
        out_specs=[
            pl.BlockSpec((None, MLA_HEADS, chunk, MLA_QK_PAD), lambda bi, i: (bi, 0, i, 0)),
            pl.BlockSpec((None, MLA_HEADS, MLA_QK_PAD // LANES, chunk, LANES),
                         lambda bi, i: (bi, 0, 0, i, 0)),
            pl.BlockSpec((None, MLA_HEADS, None, chunk // LANES, MLA_V, LANES),
                         lambda bi, i: (bi, 0, i // r, i % r, 0, 0)),
        ],
        out_shape=[
            jax.ShapeDtypeStruct((b, MLA_HEADS, s, MLA_QK_PAD), BF16),
            jax.ShapeDtypeStruct((b, MLA_HEADS, MLA_QK_PAD // LANES, s, LANES), BF16),
            jax.ShapeDtypeStruct((b, MLA_HEADS, s // kv_chunk, kv_chunk // LANES, MLA_V, LANES), BF16),
        ],
        compiler_params=_params(("parallel", "parallel")),
        name="proj1",
    )(h, g, wd, qg, kvg, wuq, wukv, cos, sin)


def _attn_kernel(q_ref, k_ref, vt_ref, o_ref, m_sc, acc_sc, s_sc, mx_sc, qt_sc, *, group, bq,
                 chunk, n_chunks, unroll):
    n = group * bq
    gw = s_sc.shape[2]
    n_groups = n // gw
    ring = s_sc.shape[0]
    m_sc[...] = jnp.full(m_sc.shape, NEG_BIG, F32)
    acc_sc[...] = jnp.zeros(acc_sc.shape, F32)
    for gi in range(n_groups):
        qg = q_ref[(gi * gw) // bq, pl.ds((gi * gw) % bq, gw), :]
        qt_sc[:, gi * gw:(gi + 1) * gw] = qg.T

    ones_rows = (lax.broadcasted_iota(jnp.int32, (BF16_SUBLANES, chunk), 0) == 0).astype(BF16)

    items = [(j, gi) for j in range(unroll) for gi in range(n_groups)]
    assert len(items) % ring == 0
    lookahead = min(ATTN_LOOKAHEAD, ring - 1)

    def scores(c, gi, slot):
        rows = pl.ds(pl.multiple_of(c * chunk, chunk), chunk)
        kc = jnp.concatenate([k_ref[i, rows, :] for i in range(k_ref.shape[0])], axis=1)
        s = jnp.dot(kc, qt_sc[:, gi * gw:(gi + 1) * gw], preferred_element_type=F32)
        s_sc[slot] = s
        mx_sc[slot] = jnp.max(s, axis=0, keepdims=True)

    for idx in range(lookahead):
        scores(items[idx][0], items[idx][1], idx)

    def body(t, carry):
        for idx, (j, gi) in enumerate(items):
            ahead = idx + lookahead
            ja, ga = items[ahead % len(items)]
            ca = jnp.minimum((t + ahead // len(items)) * unroll + ja, n_chunks - 1)
            scores(ca, ga, ahead % ring)

            cols = slice(gi * gw, (gi + 1) * gw)
            slot = idx % ring
            m_prev = m_sc[:, cols]
            m_new = jnp.maximum(m_prev, mx_sc[slot])
            alpha = jnp.exp2(m_prev - m_new)
            p = jnp.exp2(s_sc[slot] - m_new)
            vt = jnp.concatenate([vt_ref[t * unroll + j, i] for i in range(vt_ref.shape[1])], axis=1)
            vt = jnp.concatenate([vt, ones_rows], axis=0)
            pv = jnp.dot(vt, p.astype(BF16), preferred_element_type=F32)
            acc_sc[:, cols] = alpha * acc_sc[:, cols] + pv
            m_sc[:, cols] = m_new
        return carry

    lax.fori_loop(0, n_chunks // unroll, body, 0)
    dv = acc_sc.shape[0] - BF16_SUBLANES
    o = acc_sc[:dv, :] / acc_sc[dv:dv + 1, :]
    for gi in range(group):
        o_ref[:, gi * dv:(gi + 1) * dv] = o[:, gi * bq:(gi + 1) * bq].T.astype(o_ref.dtype)


def _attention(q, k, vt, *, group, bq, name):
    b, h, s, dqk = q.shape
    hkv = k.shape[1]
    n_chunks, slabs, dv, _ = vt.shape[2:]
    chunk = slabs * LANES
    n = group * bq
    gw = min(ATTN_COL_GROUP, bq)
    unroll = max(1, ATTN_TRIP_ITEMS // (n // gw))
    while n_chunks % unroll:
        unroll -= 1
    items = unroll * (n // gw)
    ring = ATTN_RING if items % ATTN_RING == 0 else items
    kern = functools.partial(_attn_kernel, group=group, bq=bq, chunk=chunk, n_chunks=n_chunks,
                             unroll=unroll)
    return pl.pallas_call(
        kern,
        grid=(b, hkv, s // bq),
        in_specs=[
            pl.BlockSpec((None, group, bq, dqk), lambda bi, hi, qi: (bi, hi, qi, 0)),
            pl.BlockSpec((None, None, dqk // LANES, s, LANES), lambda bi, hi, qi: (bi, hi, 0, 0, 0)),
            pl.BlockSpec((None, None, n_chunks, slabs, dv, LANES),
                         lambda bi, hi, qi: (bi, hi, 0, 0, 0, 0)),
        ],
        out_specs=pl.BlockSpec((None, bq, group * dv), lambda bi, hi, qi: (bi, qi, hi)),
        out_shape=jax.ShapeDtypeStruct((b, s, h * dv), BF16),
        scratch_shapes=[
            pltpu.VMEM((1, n), F32),
            pltpu.VMEM((dv + BF16_SUBLANES, n), F32),
            pltpu.VMEM((ring, chunk, gw), F32),
            pltpu.VMEM((ring, 1, gw), F32),
            pltpu.VMEM((dqk, n), BF16),
        ],
        compiler_params=_params(("parallel", "parallel", "arbitrary")),
        name=name,
    )(q, k, vt)


def _out0_kernel(h_ref, a_ref, gb_ref, cin_ref, cprev_ref, cnext_ref, cw_ref, w_ref, o_ref):
    i = pl.program_id(1)
    bm = cin_ref.shape[0]
    c = cin_ref[...].astype(F32)
    prev_row = cprev_ref[BF16_SUBLANES - 1:BF16_SUBLANES, :].astype(F32)
    next_row = cnext_ref[0:1, :].astype(F32)
    prev_row = jnp.where(i == 0, 0.0, prev_row)
    next_row = jnp.where(i == pl.num_programs(1) - 1, 0.0, next_row)
    rows = lax.broadcasted_iota(jnp.int32, (bm, 1), 0)
    c_m1 = jnp.where(rows == 0, prev_row, pltpu.roll(c, 1, 0))
    c_p1 = jnp.where(rows == bm - 1, next_row, pltpu.roll(c, bm - 1, 0))
    conv = cw_ref[0:1, :] * c_m1 + cw_ref[1:2, :] * c + cw_ref[2:3, :] * c_p1
    sconv = (gb_ref[...].astype(F32) * conv).astype(BF16)
    aw = a_ref.shape[1]
    y = jnp.dot(a_ref[...], w_ref[:aw, :], preferred_element_type=F32)
    y = y + jnp.dot(sconv, w_ref[aw:, :], preferred_element_type=F32)
    o_ref[...] = h_ref[...] + y


def _out0(h, attn, gb, cin, conv_w, w, bm):
    b, s, d = h.shape
    nb = s // bm
    r = bm // BF16_SUBLANES
    last = s // BF16_SUBLANES - 1
    tile = lambda width: pl.BlockSpec((None, bm, width), lambda bi, i: (bi, i, 0))
    return pl.pallas_call(
        _out0_kernel,
        grid=(b, nb),
        in_specs=[
            tile(d), tile(attn.shape[2]), tile(CONV_WIDTH), tile(CONV_WIDTH),
            pl.BlockSpec((None, BF16_SUBLANES, CONV_WIDTH),
                         lambda bi, i: (bi, jnp.maximum(i * r - 1, 0), 0)),
            pl.BlockSpec((None, BF16_SUBLANES, CONV_WIDTH),
                         lambda bi, i: (bi, jnp.minimum((i + 1) * r, last), 0)),
            pl.BlockSpec(conv_w.shape, lambda bi, i: (0, 0)),
            pl.BlockSpec(w.shape, lambda bi, i: (0, 0)),
        ],
        out_specs=tile(d),
        out_shape=jax.ShapeDtypeStruct(h.shape, F32),
        compiler_params=_params(("parallel", "parallel")),
        name="out0",
    )(h, attn, gb, cin, cin, cin, conv_w, w)


def _out1_kernel(h_ref, a_ref, w_ref, o_ref):
    o_ref[...] = h_ref[...] + jnp.dot(a_ref[...], w_ref[...], preferred_element_type=F32)


def _out1(h, attn, w, bm):
    t, d = h.shape
    return pl.pallas_call(
        _out1_kernel,
        grid=(t // bm,),
        in_specs=[
            pl.BlockSpec((bm, d), lambda i: (i, 0)),
            pl.BlockSpec((bm, attn.shape[1]), lambda i: (i, 0)),
            pl.BlockSpec(w.shape, lambda i: (0, 0)),
        ],
        out_specs=pl.BlockSpec((bm, d), lambda i: (i, 0)),
        out_shape=jax.ShapeDtypeStruct(h.shape, F32),
        compiler_params=_params(("parallel",)),
        name="out1",
    )(h, attn, w)


def _mlp_kernel(h_ref, g_ref, wup_ref, wdn_ref, fg_ref, o_ref, xn_sc, *, final_norm):
    f = pl.program_id(1)

    @pl.when(f == 0)
    def _():
        x = h_ref[...]
        xn_sc[...] = _rms(x, g_ref[...]).astype(BF16)
        o_ref[...] = x

    u = jnp.dot(xn_sc[...], wup_ref[...], preferred_element_type=F32)
    a = jnp.square(jnp.maximum(u, 0.0)).astype(BF16)
    o_ref[...] += jnp.dot(a, wdn_ref[...], preferred_element_type=F32)

    if final_norm:
        @pl.when(f == pl.num_programs(1) - 1)
        def _():
            o_ref[...] = _rms(o_ref[...], fg_ref[...])


def _mlp(h, g, wup, wdn, fg, *, layer, bm, bf, final_norm):
    t, d = h.shape
    dff = wup.shape[2]
    return pl.pallas_call(
        functools.partial(_mlp_kernel, final_norm=final_norm),
        grid=(t // bm, dff // bf),
        in_specs=[
            pl.BlockSpec((bm, d), lambda i, f: (i, 0)),
            pl.BlockSpec((1, d), lambda i, f: (0, 0)),
            pl.BlockSpec((None, d, bf), lambda i, f: (layer, 0, f)),
            pl.BlockSpec((None, bf, d), lambda i, f: (layer, f, 0)),
            pl.BlockSpec((1, d), lambda i, f: (0, 0)),
        ],
        out_specs=pl.BlockSpec((bm, d), lambda i, f: (i, 0)),
        out_shape=jax.ShapeDtypeStruct(h.shape, F32),
        scratch_shapes=[pltpu.VMEM((bm, d), BF16)],
        compiler_params=_params(("parallel", "arbitrary")),
        name="mlp_final" if final_norm else "mlp",
    )(h, g, wup, wdn, fg)


def _rope_angles(seq, rot_dim):
    rows = seq // GRID_W
    row = jnp.repeat(jnp.arange(rows, dtype=F32), GRID_W)
    col = jnp.tile(jnp.arange(GRID_W, dtype=F32), rows)
    axis_dim = rot_dim // 2
    inv_freq = ROPE_THETA ** (-jnp.arange(0, axis_dim, 2, dtype=F32) / axis_dim)
    return jnp.concatenate([row[:, None] * inv_freq, col[:, None] * inv_freq], axis=-1)


def _rope_tables(seq, rot_dim):
    ang = _rope_angles(seq, rot_dim)
    cos, sin = jnp.cos(ang), jnp.sin(ang)
    pad = jnp.zeros((seq, LANES // 2 - rot_dim // 2), F32)
    cos_t = jnp.concatenate([cos, pad, cos, pad], axis=-1)
    sin_t = jnp.concatenate([-sin, pad, sin, pad], axis=-1)
    return cos_t, sin_t


def _pair_split(w):
    return w[..., 0::2], w[..., 1::2]


def _pair_split_sources(rot_dim, pad):
    blank = [-1] * pad
    return list(range(0, rot_dim, 2)) + blank + list(range(1, rot_dim, 2)) + blank


_PAIR_SPLIT_128 = _pair_split_sources(HEAD_DIM, 0)
_ROPE_PAD_64 = _pair_split_sources(MLA_ROPE, LANES // 2 - MLA_ROPE // 2)
_MLA_Q_HEAD = list(range(MLA_NOPE)) + [MLA_NOPE + c if c >= 0 else -1 for c in _ROPE_PAD_64]


def _relayout(w, sources):
    sel = [[1.0 if src == k else 0.0 for src in sources] for k in range(w.shape[-1])]
    return jnp.dot(w, jnp.array(sel, BF16), preferred_element_type=BF16)


def kernel(x, even_norm_g, even_w_in, even_q_norm_g, even_k_norm_g, even_conv_w, even_w_out,
           odd_norm_g, odd_w_down, odd_q_lat_g, odd_kv_lat_g, odd_w_uq, odd_w_ukv, odd_w_o,
           mlp_norm_g, mlp_w_up, mlp_w_down, final_norm_g):
    b, s, d = x.shape
    t = _tiles(s)
    depth = mlp_norm_g.shape[0]
    cos_a, sin_a = _rope_tables(s, HEAD_DIM)
    cos_c, sin_c = _rope_tables(s, MLA_ROPE)
    fg = final_norm_g.reshape(1, d)
    w_up = mlp_w_up.astype(BF16)
    w_dn = mlp_w_down.astype(BF16)

    h = x
    for layer in range(depth):
        i = layer // 2
        if layer % 2 == 0:
            w_in = even_w_in[i].astype(BF16)
            wq = w_in[:, :ATTN_WIDTH].reshape(d, ATTN_HEADS, HEAD_DIM)
            wk = w_in[:, ATTN_WIDTH:ATTN_WIDTH + KV_WIDTH].reshape(d, ATTN_KV_HEADS, HEAD_DIM)
            wq = _relayout(wq, _PAIR_SPLIT_128).reshape(d, ATTN_WIDTH)
            wk = _relayout(wk, _PAIR_SPLIT_128).reshape(d, KV_WIDTH)
            w0 = jnp.concatenate([wq, wk, w_in[:, ATTN_WIDTH + KV_WIDTH:]], axis=-1)
            qg = jnp.concatenate(_pair_split(even_q_norm_g[i]), axis=-1).reshape(1, HEAD_DIM)
            kg = jnp.concatenate(_pair_split(even_k_norm_g[i]), axis=-1).reshape(1, HEAD_DIM)
            q, k, vt, gb, cin = _proj0(h, even_norm_g[i].reshape(1, d), w0, qg, kg,
                                       cos_a, sin_a, t["chunk"], t["kv_chunk"])
            attn = _attention(q, k, vt, group=ATTN_GROUP, bq=t["bq_gqa"], name="gqa_attn")
            h = _out0(h, attn, gb, cin, even_conv_w[i], even_w_out[i].astype(BF16), t["bm_out"])
        else:
            w_down = odd_w_down[i].astype(BF16)
            wd = jnp.concatenate([w_down[:, :Q_LORA + KV_LORA],
                                  _relayout(w_down[:, Q_LORA + KV_LORA:], _ROPE_PAD_64)], axis=-1)
            wuq = odd_w_uq[i].astype(BF16).reshape(Q_LORA, MLA_HEADS, MLA_NOPE + MLA_ROPE)
            wuq = _relayout(wuq, _MLA_Q_HEAD).reshape(Q_LORA, MLA_HEADS * MLA_QK_PAD)
            q, k, vt = _proj1(h, odd_norm_g[i].reshape(1, d), wd,
                              odd_q_lat_g[i].reshape(1, Q_LORA), odd_kv_lat_g[i].reshape(1, KV_LORA),
                              wuq, odd_w_ukv[i].astype(BF16), cos_c, sin_c, t["chunk"],
                              t["kv_chunk"])
            attn = _attention(q, k, vt, group=1, bq=t["bq_mla"], name="mla_attn")
            h = _out1(h.reshape(b * s, d), attn.reshape(b * s, -1), odd_w_o[i].astype(BF16),
                      t["bm_out"]).reshape(b, s, d)
        h = _mlp(h.reshape(b * s, d), mlp_norm_g[layer].reshape(1, d), w_up, w_dn, fg,
                 layer=layer, bm=t["bm_mlp"], bf=t["bf_mlp"],
                 final_norm=(layer == depth - 1)).reshape(b, s, d)
    return h
```

```python
import functools

import jax
import jax.numpy as jnp
from jax import lax
from jax.experimental import pallas as pl
from jax.experimental.pallas import tpu as pltpu

F32 = jnp.float32
BF16 = jnp.bfloat16

NORM_EPS = 1e-6
ROPE_THETA = 10000.0
GRID_W = 64

HEAD_DIM = 128
ATTN_HEADS = 8
ATTN_KV_HEADS = 2
ATTN_GROUP = ATTN_HEADS // ATTN_KV_HEADS
ATTN_WIDTH = ATTN_HEADS * HEAD_DIM
KV_WIDTH = ATTN_KV_HEADS * HEAD_DIM
CONV_WIDTH = 1024

MLA_HEADS = 16
MLA_NOPE = 128
MLA_ROPE = 64
MLA_V = 128
Q_LORA = 512
KV_LORA = 512
MLA_QK_PAD = 256

LANES = 128
BF16_SUBLANES = 16
VMEM_LIMIT_BYTES = 56 * 1024 * 1024
NEG_BIG = -1e30
LOG2_E = 1.4426950408889634
ATTN_COL_GROUP = 512
ATTN_TRIP_ITEMS = 32
ATTN_LOOKAHEAD = 2
ATTN_RING = 4


def _tiles(seq):
    return dict(
        chunk=min(512, seq),
        kv_chunk=min(512, seq),
        bm_mlp=min(1024, seq),
        bf_mlp=512,
        bm_out=min(512, seq),
        bq_gqa=min(1024, seq),
        bq_mla=min(4096, seq),
    )


def _params(sem):
    return pltpu.CompilerParams(dimension_semantics=sem, vmem_limit_bytes=VMEM_LIMIT_BYTES)


def _rms(x, g):
    return x * lax.rsqrt(jnp.mean(x * x, axis=-1, keepdims=True) + NORM_EPS) * g


def _store_vt(vt_ref, head, v):
    vt = v.T.astype(BF16)
    for i in range(vt_ref.shape[1]):
        vt_ref[head, i] = vt[:, i * LANES:(i + 1) * LANES]


def _rope(x, cos, sin):
    return x * cos + pltpu.roll(x, LANES // 2, 1) * sin


def _proj0_kernel(h_ref, g_ref, w_ref, qg_ref, kg_ref, cos_ref, sin_ref,
                  q_ref, k_ref, vt_ref, gb_ref, cin_ref):
    xn = _rms(h_ref[...], g_ref[...]).astype(BF16)
    y = jnp.dot(xn, w_ref[...], preferred_element_type=F32)
    cos = cos_ref[...]
    sin = sin_ref[...]
    scale = HEAD_DIM ** -0.5 * LOG2_E
    for hh in range(ATTN_HEADS):
        yh = _rms(y[:, hh * HEAD_DIM:(hh + 1) * HEAD_DIM], qg_ref[...])
        q_ref[hh] = (_rope(yh, cos, sin) * scale).astype(BF16)
    o = ATTN_WIDTH
    for hh in range(ATTN_KV_HEADS):
        yh = _rms(y[:, o + hh * HEAD_DIM:o + (hh + 1) * HEAD_DIM], kg_ref[...])
        k_ref[hh, 0] = _rope(yh, cos, sin).astype(BF16)
    o += KV_WIDTH
    for hh in range(ATTN_KV_HEADS):
        _store_vt(vt_ref, hh, y[:, o + hh * HEAD_DIM:o + (hh + 1) * HEAD_DIM])
    o += KV_WIDTH
    gb_ref[...] = y[:, o:o + CONV_WIDTH].astype(BF16)
    o += CONV_WIDTH
    cin_ref[...] = (y[:, o:o + CONV_WIDTH] * y[:, o + CONV_WIDTH:o + 2 * CONV_WIDTH]).astype(BF16)


def _proj0(h, g, w, qg, kg, cos, sin, chunk, kv_chunk):
    b, s, d = h.shape
    n = w.shape[1]
    nc = s // chunk
    r = kv_chunk // chunk
    return pl.pallas_call(
        _proj0_kernel,
        grid=(b, nc),
        in_specs=[
            pl.BlockSpec((None, chunk, d), lambda bi, i: (bi, i, 0)),
            pl.BlockSpec((1, d), lambda bi, i: (0, 0)),
            pl.BlockSpec((d, n), lambda bi, i: (0, 0)),
            pl.BlockSpec((1, HEAD_DIM), lambda bi, i: (0, 0)),
            pl.BlockSpec((1, HEAD_DIM), lambda bi, i: (0, 0)),
            pl.BlockSpec((chunk, HEAD_DIM), lambda bi, i: (i, 0)),
            pl.BlockSpec((chunk, HEAD_DIM), lambda bi, i: (i, 0)),
        ],
        out_specs=[
            pl.BlockSpec((None, ATTN_HEADS, chunk, HEAD_DIM), lambda bi, i: (bi, 0, i, 0)),
            pl.BlockSpec((None, ATTN_KV_HEADS, 1, chunk, LANES), lambda bi, i: (bi, 0, 0, i, 0)),
            pl.BlockSpec((None, ATTN_KV_HEADS, None, chunk // LANES, HEAD_DIM, LANES),
                         lambda bi, i: (bi, 0, i // r, i % r, 0, 0)),
            pl.BlockSpec((None, chunk, CONV_WIDTH), lambda bi, i: (bi, i, 0)),
            pl.BlockSpec((None, chunk, CONV_WIDTH), lambda bi, i: (bi, i, 0)),
        ],
        out_shape=[
            jax.ShapeDtypeStruct((b, ATTN_HEADS, s, HEAD_DIM), BF16),
            jax.ShapeDtypeStruct((b, ATTN_KV_HEADS, 1, s, LANES), BF16),
            jax.ShapeDtypeStruct((b, ATTN_KV_HEADS, s // kv_chunk, kv_chunk // LANES, HEAD_DIM, LANES),
                                 BF16),
            jax.ShapeDtypeStruct((b, s, CONV_WIDTH), BF16),
            jax.ShapeDtypeStruct((b, s, CONV_WIDTH), BF16),
        ],
        compiler_params=_params(("parallel", "parallel")),
        name="proj0",
    )(h, g, w, qg, kg, cos, sin)


def _proj1_kernel(h_ref, g_ref, wd_ref, qg_ref, kvg_ref, wuq_ref, wukv_ref, cos_ref, sin_ref,
                  q_ref, k_ref, vt_ref):
    xn = _rms(h_ref[...], g_ref[...]).astype(BF16)
    lat = jnp.dot(xn, wd_ref[...], preferred_element_type=F32)
    cq = _rms(lat[:, :Q_LORA], qg_ref[...]).astype(BF16)
    ckv = _rms(lat[:, Q_LORA:Q_LORA + KV_LORA], kvg_ref[...]).astype(BF16)
    cos = cos_ref[...]
    sin = sin_ref[...]
    kr = _rope(lat[:, Q_LORA + KV_LORA:], cos, sin).astype(BF16)
    q = jnp.dot(cq, wuq_ref[...], preferred_element_type=F32)
    kv = jnp.dot(ckv, wukv_ref[...], preferred_element_type=F32)
    scale = (MLA_NOPE + MLA_ROPE) ** -0.5 * LOG2_E
    for hh in range(MLA_HEADS):
        o = hh * MLA_QK_PAD
        q_ref[hh, :, :MLA_NOPE] = (q[:, o:o + MLA_NOPE] * scale).astype(BF16)
        qr = _rope(q[:, o + MLA_NOPE:o + MLA_QK_PAD], cos, sin)
        q_ref[hh, :, MLA_NOPE:] = (qr * scale).astype(BF16)
        o = hh * (MLA_NOPE + MLA_V)
        k_ref[hh, 0] = kv[:, o:o + MLA_NOPE].astype(BF16)
        k_ref[hh, 1] = kr
        _store_vt(vt_ref, hh, kv[:, o + MLA_NOPE:o + MLA_NOPE + MLA_V])


def _proj1(h, g, wd, qg, kvg, wuq, wukv, cos, sin, chunk, kv_chunk):
    b, s, d = h.shape
    nc = s // chunk
    r = kv_chunk // chunk
    const = lambda bi, i: (0, 0)
    return pl.pallas_call(
        _proj1_kernel,
        grid=(b, nc),
        in_specs=[
            pl.BlockSpec((None, chunk, d), lambda bi, i: (bi, i, 0)),
            pl.BlockSpec((1, d), const),
            pl.BlockSpec(wd.shape, const),
            pl.BlockSpec((1, Q_LORA), const),
            pl.BlockSpec((1, KV_LORA), const),
            pl.BlockSpec(wuq.shape, const),
            pl.BlockSpec(wukv.shape, const),
            pl.BlockSpec((chunk, LANES), lambda bi, i: (i, 0)),
            pl.BlockSpec((chunk, LANES), lambda bi, i: (i, 0)),
        ],
        out_specs=[
            pl.BlockSpec((None, MLA_HEADS, chunk, MLA_QK_PAD), lambda bi, i: (bi, 0, i, 0)),
            pl.BlockSpec((None, MLA_HEADS, MLA_QK_PAD // LANES, chunk, LANES),
                         lambda bi, i: (bi, 0, 0, i, 0)),
            pl.BlockSpec((None, MLA_HEADS, None, chunk // LANES, MLA_V, LANES),
                         lambda bi, i: (bi, 0, i // r, i % r, 0, 0)),
        ],
        out_shape=[
            jax.ShapeDtypeStruct((b, MLA_HEADS, s, MLA_QK_PAD), BF16),
            jax.ShapeDtypeStruct((b, MLA_HEADS, MLA_QK_PAD // LANES, s, LANES), BF16),
            jax.ShapeDtypeStruct((b, MLA_HEADS, s // kv_chunk, kv_chunk // LANES, MLA_V, LANES), BF16),
        ],
        compiler_params=_params(("parallel", "parallel")),
        name="proj1",
    )(h, g, wd, qg, kvg, wuq, wukv, cos, sin)


def _attn_kernel(q_ref, k_ref, vt_ref, o_ref, m_sc, acc_sc, s_sc, mx_sc, qt_sc, *, group, bq,
                 chunk, n_chunks, unroll):
    n = group * bq
    gw = s_sc.shape[2]
    n_groups = n // gw
    ring = s_sc.shape[0]
    m_sc[...] = jnp.full(m_sc.shape, NEG_BIG, F32)
    acc_sc[...] = jnp.zeros(acc_sc.shape, F32)
    for gi in range(n_groups):
        qg = q_ref[(gi * gw) // bq, pl.ds((gi * gw) % bq, gw), :]
        qt_sc[:, gi * gw:(gi + 1) * gw] = qg.T

    ones_rows = (lax.broadcasted_iota(jnp.int32, (BF16_SUBLANES, chunk), 0) == 0).astype(BF16)

    items = [(j, gi) for j in range(unroll) for gi in range(n_groups)]
    assert len(items) % ring == 0
    lookahead = min(ATTN_LOOKAHEAD, ring - 1)

    def scores(c, gi, slot):
        rows = pl.ds(pl.multiple_of(c * chunk, chunk), chunk)
        kc = jnp.concatenate([k_ref[i, rows, :] for i in range(k_ref.shape[0])], axis=1)
        s = jnp.dot(kc, qt_sc[:, gi * gw:(gi + 1) * gw], preferred_element_type=F32)
        s_sc[slot] = s
        mx_sc[slot] = jnp.max(s, axis=0, keepdims=True)

    for idx in range(lookahead):
        scores(items[idx][0], items[idx][1], idx)

    def body(t, carry):
        for idx, (j, gi) in enumerate(items):
            ahead = idx + lookahead
            ja, ga = items[ahead % len(items)]
            ca = jnp.minimum((t + ahead // len(items)) * unroll + ja, n_chunks - 1)
            scores(ca, ga, ahead % ring)

            cols = slice(gi * gw, (gi + 1) * gw)
            slot = idx % ring
            m_prev = m_sc[:, cols]
            m_new = jnp.maximum(m_prev, mx_sc[slot])
            alpha = jnp.exp2(m_prev - m_new)
            p = jnp.exp2(s_sc[slot] - m_new)
            vt = jnp.concatenate([vt_ref[t * unroll + j, i] for i in range(vt_ref.shape[1])], axis=1)
            vt = jnp.concatenate([vt, ones_rows], axis=0)
            pv = jnp.dot(vt, p.astype(BF16), preferred_element_type=F32)
            acc_sc[:, cols] = alpha * acc_sc[:, cols] + pv
            m_sc[:, cols] = m_new
        return carry

    lax.fori_loop(0, n_chunks // unroll, body, 0)
    dv = acc_sc.shape[0] - BF16_SUBLANES
    o = acc_sc[:dv, :] / acc_sc[dv:dv + 1, :]
    for gi in range(group):
        o_ref[:, gi * dv:(gi + 1) * dv] = o[:, gi * bq:(gi + 1) * bq].T.astype(o_ref.dtype)


def _attention(q, k, vt, *, group, bq, name):
    b, h, s, dqk = q.shape
    hkv = k.shape[1]
    n_chunks, slabs, dv, _ = vt.shape[2:]
    chunk = slabs * LANES
    n = group * bq
    gw = min(ATTN_COL_GROUP, bq)
    unroll = max(1, ATTN_TRIP_ITEMS // (n // gw))
    while n_chunks % unroll:
        unroll -= 1
    items = unroll * (n // gw)
    ring = ATTN_RING if items % ATTN_RING == 0 else items
    kern = functools.partial(_attn_kernel, group=group, bq=bq, chunk=chunk, n_chunks=n_chunks,
                             unroll=unroll)
    return pl.pallas_call(
        kern,
        grid=(b, hkv, s // bq),
        in_specs=[
            pl.BlockSpec((None, group, bq, dqk), lambda bi, hi, qi: (bi, hi, qi, 0)),
            pl.BlockSpec((None, None, dqk // LANES, s, LANES), lambda bi, hi, qi: (bi, hi, 0, 0, 0)),
            pl.BlockSpec((None, None, n_chunks, slabs, dv, LANES),
                         lambda bi, hi, qi: (bi, hi, 0, 0, 0, 0)),
        ],
        out_specs=pl.BlockSpec((None, bq, group * dv), lambda bi, hi, qi: (bi, qi, hi)),
        out_shape=jax.ShapeDtypeStruct((b, s, h * dv), BF16),
        scratch_shapes=[
            pltpu.VMEM((1, n), F32),
            pltpu.VMEM((dv + BF16_SUBLANES, n), F32),
            pltpu.VMEM((ring, chunk, gw), F32),
            pltpu.VMEM((ring, 1, gw), F32),
            pltpu.VMEM((dqk, n), BF16),
        ],
        compiler_params=_params(("parallel", "parallel", "arbitrary")),
        name=name,
    )(q, k, vt)


def _out0_kernel(h_ref, a_ref, gb_ref, cin_ref, cprev_ref, cnext_ref, cw_ref, w_ref, o_ref):
    i = pl.program_id(1)
    bm = cin_ref.shape[0]
    c = cin_ref[...].astype(F32)
    prev_row = cprev_ref[BF16_SUBLANES - 1:BF16_SUBLANES, :].astype(F32)
    next_row = cnext_ref[0:1, :].astype(F32)
    prev_row = jnp.where(i == 0, 0.0, prev_row)
    next_row = jnp.where(i == pl.num_programs(1) - 1, 0.0, next_row)
    rows = lax.broadcasted_iota(jnp.int32, (bm, 1), 0)
    c_m1 = jnp.where(rows == 0, prev_row, pltpu.roll(c, 1, 0))
    c_p1 = jnp.where(rows == bm - 1, next_row, pltpu.roll(c, bm - 1, 0))
    conv = cw_ref[0:1, :] * c_m1 + cw_ref[1:2, :] * c + cw_ref[2:3, :] * c_p1
    sconv = (gb_ref[...].astype(F32) * conv).astype(BF16)
    aw = a_ref.shape[1]
    y = jnp.dot(a_ref[...], w_ref[:aw, :], preferred_element_type=F32)
    y = y + jnp.dot(sconv, w_ref[aw:, :], preferred_element_type=F32)
    o_ref[...] = h_ref[...] + y


def _out0(h, attn, gb, cin, conv_w, w, bm):
    b, s, d = h.shape
    nb = s // bm
    r = bm // BF16_SUBLANES
    last = s // BF16_SUBLANES - 1
    tile = lambda width: pl.BlockSpec((None, bm, width), lambda bi, i: (bi, i, 0))
    return pl.pallas_call(
        _out0_kernel,
        grid=(b, nb),
        in_specs=[
            tile(d), tile(attn.shape[2]), tile(CONV_WIDTH), tile(CONV_WIDTH),
            pl.BlockSpec((None, BF16_SUBLANES, CONV_WIDTH),
                         lambda bi, i: (bi, jnp.maximum(i * r - 1, 0), 0)),
            pl.BlockSpec((None, BF16_SUBLANES, CONV_WIDTH),
                         lambda bi, i: (bi, jnp.minimum((i + 1) * r, last), 0)),
            pl.BlockSpec(conv_w.shape, lambda bi, i: (0, 0)),
            pl.BlockSpec(w.shape, lambda bi, i: (0, 0)),
        ],
        out_specs=tile(d),
        out_shape=jax.ShapeDtypeStruct(h.shape, F32),
        compiler_params=_params(("parallel", "parallel")),
        name="out0",
    )(h, attn, gb, cin, cin, cin, conv_w, w)


def _out1_kernel(h_ref, a_ref, w_ref, o_ref):
    o_ref[...] = h_ref[...] + jnp.dot(a_ref[...], w_ref[...], preferred_element_type=F32)


def _out1(h, attn, w, bm):
    t, d = h.shape
    return pl.pallas_call(
        _out1_kernel,
        grid=(t // bm,),
        in_specs=[
            pl.BlockSpec((bm, d), lambda i: (i, 0)),
            pl.BlockSpec((bm, attn.shape[1]), lambda i: (i, 0)),
            pl.BlockSpec(w.shape, lambda i: (0, 0)),
        ],
        out_specs=pl.BlockSpec((bm, d), lambda i: (i, 0)),
        out_shape=jax.ShapeDtypeStruct(h.shape, F32),
        compiler_params=_params(("parallel",)),
        name="out1",
    )(h, attn, w)


def _mlp_kernel(h_ref, g_ref, wup_ref, wdn_ref, fg_ref, o_ref, xn_sc, *, final_norm):
    f = pl.program_id(1)

    def mlp_chunk(xn):
        u = jnp.dot(xn, wup_ref[...], preferred_element_type=F32)
        a = jnp.square(jnp.maximum(u, 0.0)).astype(BF16)
        return jnp.dot(a, wdn_ref[...], preferred_element_type=F32)

    @pl.when(f == 0)
    def _():
        x = h_ref[...]
        xn = _rms(x, g_ref[...]).astype(BF16)
        xn_sc[...] = xn
        o_ref[...] = x + mlp_chunk(xn)

    @pl.when(f > 0)
    def _():
        o_ref[...] += mlp_chunk(xn_sc[...])

    if final_norm:
        @pl.when(f == pl.num_programs(1) - 1)
        def _():
            o_ref[...] = _rms(o_ref[...], fg_ref[...])


def _mlp(h, g, wup, wdn, fg, *, layer, bm, bf, final_norm):
    t, d = h.shape
    dff = wup.shape[2]
    return pl.pallas_call(
        functools.partial(_mlp_kernel, final_norm=final_norm),
        grid=(t // bm, dff // bf),
        in_specs=[
            pl.BlockSpec((bm, d), lambda i, f: (i, 0)),
            pl.BlockSpec((1, d), lambda i, f: (0, 0)),
            pl.BlockSpec((None, d, bf), lambda i, f: (layer, 0, f)),
            pl.BlockSpec((None, bf, d), lambda i, f: (layer, f, 0)),
            pl.BlockSpec((1, d), lambda i, f: (0, 0)),
        ],
        out_specs=pl.BlockSpec((bm, d), lambda i, f: (i, 0)),
        out_shape=jax.ShapeDtypeStruct(h.shape, F32),
        scratch_shapes=[pltpu.VMEM((bm, d), BF16)],
        compiler_params=_params(("parallel", "arbitrary")),
        name="mlp_final" if final_norm else "mlp",
    )(h, g, wup, wdn, fg)


def _rope_cos_sin(seq, rot_dim):
    rows = seq // GRID_W
    axis_dim = rot_dim // 2
    inv_freq = ROPE_THETA ** (-jnp.arange(0, axis_dim, 2, dtype=F32) / axis_dim)
    ang_row = jnp.arange(rows, dtype=F32)[:, None] * inv_freq
    ang_col = jnp.arange(GRID_W, dtype=F32)[:, None] * inv_freq
    shape = (rows, GRID_W, inv_freq.shape[0])

    def per_token(fn):
        by_row = jnp.broadcast_to(fn(ang_row)[:, None, :], shape)
        by_col = jnp.broadcast_to(fn(ang_col)[None, :, :], shape)
        return jnp.concatenate([by_row, by_col], axis=-1).reshape(seq, rot_dim // 2)

    return per_token(jnp.cos), per_token(jnp.sin)


def _rope_tables(seq, rot_dim):
    cos, sin = _rope_cos_sin(seq, rot_dim)
    pad = jnp.zeros((seq, LANES // 2 - rot_dim // 2), F32)
    cos_t = jnp.concatenate([cos, pad, cos, pad], axis=-1)
    sin_t = jnp.concatenate([-sin, pad, sin, pad], axis=-1)
    return cos_t, sin_t


def _pair_split(w):
    return w[..., 0::2], w[..., 1::2]


def _pair_split_sources(rot_dim, pad):
    blank = [-1] * pad
    return list(range(0, rot_dim, 2)) + blank + list(range(1, rot_dim, 2)) + blank


_PAIR_SPLIT_128 = _pair_split_sources(HEAD_DIM, 0)
_ROPE_PAD_64 = _pair_split_sources(MLA_ROPE, LANES // 2 - MLA_ROPE // 2)
_MLA_Q_HEAD = list(range(MLA_NOPE)) + [MLA_NOPE + c if c >= 0 else -1 for c in _ROPE_PAD_64]


def _relayout(w, sources):
    sel = [[1.0 if src == k else 0.0 for src in sources] for k in range(w.shape[-1])]
    return jnp.dot(w, jnp.array(sel, BF16), preferred_element_type=BF16)


def kernel(x, even_norm_g, even_w_in, even_q_norm_g, even_k_norm_g, even_conv_w, even_w_out,
           odd_norm_g, odd_w_down, odd_q_lat_g, odd_kv_lat_g, odd_w_uq, odd_w_ukv, odd_w_o,
           mlp_norm_g, mlp_w_up, mlp_w_down, final_norm_g):
    b, s, d = x.shape
    t = _tiles(s)
    depth = mlp_norm_g.shape[0]
    cos_a, sin_a = _rope_tables(s, HEAD_DIM)
    cos_c, sin_c = _rope_tables(s, MLA_ROPE)
    fg = final_norm_g.reshape(1, d)
    w_up = mlp_w_up.astype(BF16)
    w_dn = mlp_w_down.astype(BF16)

    h = x
    for layer in range(depth):
        i = layer // 2
        if layer % 2 == 0:
            w_in = even_w_in[i].astype(BF16)
            wq = w_in[:, :ATTN_WIDTH].reshape(d, ATTN_HEADS, HEAD_DIM)
            wk = w_in[:, ATTN_WIDTH:ATTN_WIDTH + KV_WIDTH].reshape(d, ATTN_KV_HEADS, HEAD_DIM)
            wq = _relayout(wq, _PAIR_SPLIT_128).reshape(d, ATTN_WIDTH)
            wk = _relayout(wk, _PAIR_SPLIT_128).reshape(d, KV_WIDTH)
            w0 = jnp.concatenate([wq, wk, w_in[:, ATTN_WIDTH + KV_WIDTH:]], axis=-1)
            qg = jnp.concatenate(_pair_split(even_q_norm_g[i]), axis=-1).reshape(1, HEAD_DIM)
            kg = jnp.concatenate(_pair_split(even_k_norm_g[i]), axis=-1).reshape(1, HEAD_DIM)
            q, k, vt, gb, cin = _proj0(h, even_norm_g[i].reshape(1, d), w0, qg, kg,
                                       cos_a, sin_a, t["chunk"], t["kv_chunk"])
            attn = _attention(q, k, vt, group=ATTN_GROUP, bq=t["bq_gqa"], name="gqa_attn")
            h = _out0(h, attn, gb, cin, even_conv_w[i], even_w_out[i].astype(BF16), t["bm_out"])
        else:
            w_down = odd_w_down[i].astype(BF16)
            wd = jnp.concatenate([w_down[:, :Q_LORA + KV_LORA],
                                  _relayout(w_down[:, Q_LORA + KV_LORA:], _ROPE_PAD_64)], axis=-1)
            wuq = odd_w_uq[i].astype(BF16).reshape(Q_LORA, MLA_HEADS, MLA_NOPE + MLA_ROPE)
            wuq = _relayout(wuq, _MLA_Q_HEAD).reshape(Q_LORA, MLA_HEADS * MLA_QK_PAD)
            q, k, vt = _proj1(h, odd_norm_g[i].reshape(1, d), wd,
                              odd_q_lat_g[i].reshape(1, Q_LORA), odd_kv_lat_g[i].reshape(1, KV_LORA),
                              wuq, odd_w_ukv[i].astype(BF16), cos_c, sin_c, t["chunk"],
                              t["kv_chunk"])
            attn = _attention(q, k, vt, group=1, bq=t["bq_mla"], name="mla_attn")
            h = _out1(h.reshape(b * s, d), attn.reshape(b * s, -1), odd_w_o[i].astype(BF16),
                      t["bm_out"]).reshape(b, s, d)
        h = _mlp(h.reshape(b * s, d), mlp_norm_g[layer].reshape(1, d), w_up, w_dn, fg,
                 layer=layer, bm=t["bm_mlp"], bf=t["bf_mlp"],
                 final_norm=(layer == depth - 1)).reshape(b, s, d)
    return h
```

```python
import functools

import jax
import jax.numpy as jnp
from jax import lax
from jax.experimental import pallas as pl
from jax.experimental.pallas import tpu as pltpu

F32 = jnp.float32
BF16 = jnp.bfloat16

NORM_EPS = 1e-6
ROPE_THETA = 10000.0
GRID_W = 64

HEAD_DIM = 128
ATTN_HEADS = 8
ATTN_KV_HEADS = 2
ATTN_GROUP = ATTN_HEADS // ATTN_KV_HEADS
ATTN_WIDTH = ATTN_HEADS * HEAD_DIM
KV_WIDTH = ATTN_KV_HEADS * HEAD_DIM
CONV_WIDTH = 1024

MLA_HEADS = 16
MLA_NOPE = 128
MLA_ROPE = 64
MLA_V = 128
Q_LORA = 512
KV_LORA = 512
MLA_QK_PAD = 256

LANES = 128
BF16_SUBLANES = 16
VMEM_LIMIT_BYTES = 56 * 1024 * 1024
NEG_BIG = -1e30
LOG2_E = 1.4426950408889634
ATTN_COL_GROUP = 512
ATTN_TRIP_ITEMS = 32
ATTN_LOOKAHEAD = 2
ATTN_RING = 4


def _tiles(seq):
    return dict(
        chunk=min(512, seq),
        kv_chunk=min(512, seq),
        bm_mlp=min(1024, seq),
        bf_mlp=512,
        bm_out=min(512, seq),
        bq_gqa=min(1024, seq),
        bq_mla=min(4096, seq),
    )


def _params(sem):
    return pltpu.CompilerParams(dimension_semantics=sem, vmem_limit_bytes=VMEM_LIMIT_BYTES)


def _rms(x, g):
    return x * lax.rsqrt(jnp.mean(x * x, axis=-1, keepdims=True) + NORM_EPS) * g


def _store_vt(vt_ref, head, v):
    vt = v.T.astype(BF16)
    for i in range(vt_ref.shape[1]):
        vt_ref[head, i] = vt[:, i * LANES:(i + 1) * LANES]


def _rope(x, cos, sin):
    return x * cos + pltpu.roll(x, LANES // 2, 1) * sin


def _proj0_kernel(h_ref, g_ref, w_ref, qg_ref, kg_ref, cos_ref, sin_ref,
                  q_ref, k_ref, vt_ref, gb_ref, cin_ref):
    xn = _rms(h_ref[...], g_ref[...]).astype(BF16)
    y = jnp.dot(xn, w_ref[...], preferred_element_type=F32)
    cos = cos_ref[...]
    sin = sin_ref[...]
    scale = HEAD_DIM ** -0.5 * LOG2_E
    for hh in range(ATTN_HEADS):
        yh = _rms(y[:, hh * HEAD_DIM:(hh + 1) * HEAD_DIM], qg_ref[...])
        q_ref[hh] = (_rope(yh, cos, sin) * scale).astype(BF16)
    o = ATTN_WIDTH
    for hh in range(ATTN_KV_HEADS):
        yh = _rms(y[:, o + hh * HEAD_DIM:o + (hh + 1) * HEAD_DIM], kg_ref[...])
        k_ref[hh, 0] = _rope(yh, cos, sin).astype(BF16)
    o += KV_WIDTH
    for hh in range(ATTN_KV_HEADS):
        _store_vt(vt_ref, hh, y[:, o + hh * HEAD_DIM:o + (hh + 1) * HEAD_DIM])
    o += KV_WIDTH
    gb_ref[...] = y[:, o:o + CONV_WIDTH].astype(BF16)
    o += CONV_WIDTH
    cin_ref[...] = (y[:, o:o + CONV_WIDTH] * y[:, o + CONV_WIDTH:o + 2 * CONV_WIDTH]).astype(BF16)


def _proj0(h, g, w, qg, kg, cos, sin, chunk, kv_chunk):
    b, s, d = h.shape
    n = w.shape[1]
    nc = s // chunk
    r = kv_chunk // chunk
    return pl.pallas_call(
        _proj0_kernel,
        grid=(b, nc),
        in_specs=[
            pl.BlockSpec((None, chunk, d), lambda bi, i: (bi, i, 0)),
            pl.BlockSpec((1, d), lambda bi, i: (0, 0)),
            pl.BlockSpec((d, n), lambda bi, i: (0, 0)),
            pl.BlockSpec((1, HEAD_DIM), lambda bi, i: (0, 0)),
            pl.BlockSpec((1, HEAD_DIM), lambda bi, i: (0, 0)),
            pl.BlockSpec((chunk, HEAD_DIM), lambda bi, i: (i, 0)),
            pl.BlockSpec((chunk, HEAD_DIM), lambda bi, i: (i, 0)),
        ],
        out_specs=[
            pl.BlockSpec((None, ATTN_HEADS, chunk, HEAD_DIM), lambda bi, i: (bi, 0, i, 0)),
            pl.BlockSpec((None, ATTN_KV_HEADS, 1, chunk, LANES), lambda bi, i: (bi, 0, 0, i, 0)),
            pl.BlockSpec((None, ATTN_KV_HEADS, None, chunk // LANES, HEAD_DIM, LANES),
                         lambda bi, i: (bi, 0, i // r, i % r, 0, 0)),
            pl.BlockSpec((None, chunk, CONV_WIDTH), lambda bi, i: (bi, i, 0)),
            pl.BlockSpec((None, chunk, CONV_WIDTH), lambda bi, i: (bi, i, 0)),
        ],
        out_shape=[
            jax.ShapeDtypeStruct((b, ATTN_HEADS, s, HEAD_DIM), BF16),
            jax.ShapeDtypeStruct((b, ATTN_KV_HEADS, 1, s, LANES), BF16),
            jax.ShapeDtypeStruct((b, ATTN_KV_HEADS, s // kv_chunk, kv_chunk // LANES, HEAD_DIM, LANES),
                                 BF16),
            jax.ShapeDtypeStruct((b, s, CONV_WIDTH), BF16),
            jax.ShapeDtypeStruct((b, s, CONV_WIDTH), BF16),
        ],
        compiler_params=_params(("parallel", "parallel")),
        name="proj0",
    )(h, g, w, qg, kg, cos, sin)


def _proj1_kernel(h_ref, g_ref, wd_ref, qg_ref, kvg_ref, wuq_ref, wukv_ref, cos_ref, sin_ref,
                  q_ref, k_ref, vt_ref):
    xn = _rms(h_ref[...], g_ref[...]).astype(BF16)
    lat = jnp.dot(xn, wd_ref[...], preferred_element_type=F32)
    cq = _rms(lat[:, :Q_LORA], qg_ref[...]).astype(BF16)
    ckv = _rms(lat[:, Q_LORA:Q_LORA + KV_LORA], kvg_ref[...]).astype(BF16)
    cos = cos_ref[...]
    sin = sin_ref[...]
    kr = _rope(lat[:, Q_LORA + KV_LORA:], cos, sin).astype(BF16)
    q = jnp.dot(cq, wuq_ref[...], preferred_element_type=F32)
    kv = jnp.dot(ckv, wukv_ref[...], preferred_element_type=F32)
    scale = (MLA_NOPE + MLA_ROPE) ** -0.5 * LOG2_E
    for hh in range(MLA_HEADS):
        o = hh * MLA_QK_PAD
        q_ref[hh, :, :MLA_NOPE] = (q[:, o:o + MLA_NOPE] * scale).astype(BF16)
        qr = _rope(q[:, o + MLA_NOPE:o + MLA_QK_PAD], cos, sin)
        q_ref[hh, :, MLA_NOPE:] = (qr * scale).astype(BF16)
        o = hh * (MLA_NOPE + MLA_V)
        k_ref[hh, 0] = kv[:, o:o + MLA_NOPE].astype(BF16)
        k_ref[hh, 1] = kr
        _store_vt(vt_ref, hh, kv[:, o + MLA_NOPE:o + MLA_NOPE + MLA_V])


def _proj1(h, g, wd, qg, kvg, wuq, wukv, cos, sin, chunk, kv_chunk):
    b, s, d = h.shape
    nc = s // chunk
    r = kv_chunk // chunk
    const = lambda bi, i: (0, 0)
    return pl.pallas_call(
        _proj1_kernel,
        grid=(b, nc),
        in_specs=[
            pl.BlockSpec((None, chunk, d), lambda bi, i: (bi, i, 0)),
            pl.BlockSpec((1, d), const),
            pl.BlockSpec(wd.shape, const),
            pl.BlockSpec((1, Q_LORA), const),
            pl.BlockSpec((1, KV_LORA), const),
            pl.BlockSpec(wuq.shape, const),
            pl.BlockSpec(wukv.shape, const),
            pl.BlockSpec((chunk, LANES), lambda bi, i: (i, 0)),
            pl.BlockSpec((chunk, LANES), lambda bi, i: (i, 0)),
        ],
        out_specs=[
            pl.BlockSpec((None, MLA_HEADS, chunk, MLA_QK_PAD), lambda bi, i: (bi, 0, i, 0)),
            pl.BlockSpec((None, MLA_HEADS, MLA_QK_PAD // LANES, chunk, LANES),
                         lambda bi, i: (bi, 0, 0, i, 0)),
            pl.BlockSpec((None, MLA_HEADS, None, chunk // LANES, MLA_V, LANES),
                         lambda bi, i: (bi, 0, i // r, i % r, 0, 0)),
        ],
        out_shape=[
            jax.ShapeDtypeStruct((b, MLA_HEADS, s, MLA_QK_PAD), BF16),
            jax.ShapeDtypeStruct((b, MLA_HEADS, MLA_QK_PAD // LANES, s, LANES), BF16),
            jax.ShapeDtypeStruct((b, MLA_HEADS, s // kv_chunk, kv_chunk // LANES, MLA_V, LANES), BF16),
        ],
        compiler_params=_params(("parallel", "parallel")),
        name="proj1",
    )(h, g, wd, qg, kvg, wuq, wukv, cos, sin)


def _attn_kernel(q_ref, k_ref, vt_ref, o_ref, m_sc, acc_sc, s_sc, mx_sc, qt_sc, *, group, bq,
                 chunk, n_chunks, unroll):
    n = group * bq
    gw = s_sc.shape[2]
    n_groups = n // gw
    ring = s_sc.shape[0]
    m_sc[...] = jnp.full(m_sc.shape, NEG_BIG, F32)
    acc_sc[...] = jnp.zeros(acc_sc.shape, F32)
    for gi in range(n_groups):
        qg = q_ref[(gi * gw) // bq, pl.ds((gi * gw) % bq, gw), :]
        qt_sc[:, gi * gw:(gi + 1) * gw] = qg.T

    ones_rows = (lax.broadcasted_iota(jnp.int32, (BF16_SUBLANES, chunk), 0) == 0).astype(BF16)

    items = [(j, gi) for j in range(unroll) for gi in range(n_groups)]
    assert len(items) % ring == 0
    lookahead = min(ATTN_LOOKAHEAD, ring - 1)

    def scores(c, gi, slot):
        rows = pl.ds(pl.multiple_of(c * chunk, chunk), chunk)
        kc = jnp.concatenate([k_ref[i, rows, :] for i in range(k_ref.shape[0])], axis=1)
        s = jnp.dot(kc, qt_sc[:, gi * gw:(gi + 1) * gw], preferred_element_type=F32)
        s_sc[slot] = s
        mx_sc[slot] = jnp.max(s, axis=0, keepdims=True)

    for idx in range(lookahead):
        scores(items[idx][0], items[idx][1], idx)

    def body(t, carry):
        for idx, (j, gi) in enumerate(items):
            ahead = idx + lookahead
            ja, ga = items[ahead % len(items)]
            ca = jnp.minimum((t + ahead // len(items)) * unroll + ja, n_chunks - 1)
            scores(ca, ga, ahead % ring)

            cols = slice(gi * gw, (gi + 1) * gw)
            slot = idx % ring
            m_prev = m_sc[:, cols]
            m_new = jnp.maximum(m_prev, mx_sc[slot])
            alpha = jnp.exp2(m_prev - m_new)
            p = jnp.exp2(s_sc[slot] - m_new)
            vt = jnp.concatenate([vt_ref[t * unroll + j, i] for i in range(vt_ref.shape[1])], axis=1)
            vt = jnp.concatenate([vt, ones_rows], axis=0)
            pv = jnp.dot(vt, p.astype(BF16), preferred_element_type=F32)
            acc_sc[:, cols] = alpha * acc_sc[:, cols] + pv
            m_sc[:, cols] = m_new
        return carry

    lax.fori_loop(0, n_chunks // unroll, body, 0)
    dv = acc_sc.shape[0] - BF16_SUBLANES
    o = acc_sc[:dv, :] / acc_sc[dv:dv + 1, :]
    for gi in range(group):
        o_ref[:, gi * dv:(gi + 1) * dv] = o[:, gi * bq:(gi + 1) * bq].T.astype(o_ref.dtype)


def _attention(q, k, vt, *, group, bq, name):
    b, h, s, dqk = q.shape
    hkv = k.shape[1]
    n_chunks, slabs, dv, _ = vt.shape[2:]
    chunk = slabs * LANES
    n = group * bq
    gw = min(ATTN_COL_GROUP, bq)
    unroll = max(1, ATTN_TRIP_ITEMS // (n // gw))
    while n_chunks % unroll:
        unroll -= 1
    items = unroll * (n // gw)
    ring = ATTN_RING if items % ATTN_RING == 0 else items
    kern = functools.partial(_attn_kernel, group=group, bq=bq, chunk=chunk, n_chunks=n_chunks,
                             unroll=unroll)
    return pl.pallas_call(
        kern,
        grid=(b, hkv, s // bq),
        in_specs=[
            pl.BlockSpec((None, group, bq, dqk), lambda bi, hi, qi: (bi, hi, qi, 0)),
            pl.BlockSpec((None, None, dqk // LANES, s, LANES), lambda bi, hi, qi: (bi, hi, 0, 0, 0)),
            pl.BlockSpec((None, None, n_chunks, slabs, dv, LANES),
                         lambda bi, hi, qi: (bi, hi, 0, 0, 0, 0)),
        ],
        out_specs=pl.BlockSpec((None, bq, group * dv), lambda bi, hi, qi: (bi, qi, hi)),
        out_shape=jax.ShapeDtypeStruct((b, s, h * dv), BF16),
        scratch_shapes=[
            pltpu.VMEM((1, n), F32),
            pltpu.VMEM((dv + BF16_SUBLANES, n), F32),
            pltpu.VMEM((ring, chunk, gw), F32),
            pltpu.VMEM((ring, 1, gw), F32),
            pltpu.VMEM((dqk, n), BF16),
        ],
        compiler_params=_params(("parallel", "parallel", "arbitrary")),
        name=name,
    )(q, k, vt)


def _cast_weight_slices(wup_ref, wdn_ref, wup_o_ref, wdn_o_ref):
    wup_o_ref[...] = wup_ref[...].astype(BF16)
    wdn_o_ref[...] = wdn_ref[...].astype(BF16)


def _weight_cast_specs(mlp_w, layer, steps, step_index):
    in_specs, out_specs, out_shapes = [], [], []
    for w in mlp_w:
        rows, cols = w.shape[1:]
        assert rows % (steps * BF16_SUBLANES) == 0
        blk = rows // steps
        in_specs.append(pl.BlockSpec((None, blk, cols),
                                     lambda *g: (layer, step_index(*g), 0)))
        out_specs.append(pl.BlockSpec((blk, cols), lambda *g: (step_index(*g), 0)))
        out_shapes.append(jax.ShapeDtypeStruct((rows, cols), BF16))
    return in_specs, out_specs, out_shapes


def _out0_kernel(h_ref, a_ref, gb_ref, cin_ref, cprev_ref, cnext_ref, cw_ref, w_ref, wup_ref, wdn_ref,
                 o_ref, wup_o_ref, wdn_o_ref):
    _cast_weight_slices(wup_ref, wdn_ref, wup_o_ref, wdn_o_ref)
    i = pl.program_id(1)
    bm = cin_ref.shape[0]
    c = cin_ref[...].astype(F32)
    prev_row = cprev_ref[BF16_SUBLANES - 1:BF16_SUBLANES, :].astype(F32)
    next_row = cnext_ref[0:1, :].astype(F32)
    prev_row = jnp.where(i == 0, 0.0, prev_row)
    next_row = jnp.where(i == pl.num_programs(1) - 1, 0.0, next_row)
    rows = lax.broadcasted_iota(jnp.int32, (bm, 1), 0)
    c_m1 = jnp.where(rows == 0, prev_row, pltpu.roll(c, 1, 0))
    c_p1 = jnp.where(rows == bm - 1, next_row, pltpu.roll(c, bm - 1, 0))
    conv = cw_ref[0:1, :] * c_m1 + cw_ref[1:2, :] * c + cw_ref[2:3, :] * c_p1
    sconv = (gb_ref[...].astype(F32) * conv).astype(BF16)
    aw = a_ref.shape[1]
    y = jnp.dot(a_ref[...], w_ref[:aw, :], preferred_element_type=F32)
    y = y + jnp.dot(sconv, w_ref[aw:, :], preferred_element_type=F32)
    o_ref[...] = h_ref[...] + y


def _out0(h, attn, gb, cin, conv_w, w, mlp_w, layer, bm):
    b, s, d = h.shape
    nb = s // bm
    w_specs, w_out_specs, w_out_shapes = _weight_cast_specs(mlp_w, layer, b * nb,
                                                            lambda bi, i: bi * nb + i)
    r = bm // BF16_SUBLANES
    last = s // BF16_SUBLANES - 1
    tile = lambda width: pl.BlockSpec((None, bm, width), lambda bi, i: (bi, i, 0))
    return pl.pallas_call(
        _out0_kernel,
        grid=(b, nb),
        in_specs=[
            tile(d), tile(attn.shape[2]), tile(CONV_WIDTH), tile(CONV_WIDTH),
            pl.BlockSpec((None, BF16_SUBLANES, CONV_WIDTH),
                         lambda bi, i: (bi, jnp.maximum(i * r - 1, 0), 0)),
            pl.BlockSpec((None, BF16_SUBLANES, CONV_WIDTH),
                         lambda bi, i: (bi, jnp.minimum((i + 1) * r, last), 0)),
            pl.BlockSpec(conv_w.shape, lambda bi, i: (0, 0)),
            pl.BlockSpec(w.shape, lambda bi, i: (0, 0)),
        ] + w_specs,
        out_specs=[tile(d)] + w_out_specs,
        out_shape=[jax.ShapeDtypeStruct(h.shape, F32)] + w_out_shapes,
        compiler_params=_params(("parallel", "parallel")),
        name="out0",
    )(h, attn, gb, cin, cin, cin, conv_w, w, *mlp_w)


def _out1_kernel(h_ref, a_ref, w_ref, wup_ref, wdn_ref, o_ref, wup_o_ref, wdn_o_ref):
    _cast_weight_slices(wup_ref, wdn_ref, wup_o_ref, wdn_o_ref)
    o_ref[...] = h_ref[...] + jnp.dot(a_ref[...], w_ref[...], preferred_element_type=F32)


def _out1(h, attn, w, mlp_w, layer, bm):
    t, d = h.shape
    w_specs, w_out_specs, w_out_shapes = _weight_cast_specs(mlp_w, layer, t // bm, lambda i: i)
    return pl.pallas_call(
        _out1_kernel,
        grid=(t // bm,),
        in_specs=[
            pl.BlockSpec((bm, d), lambda i: (i, 0)),
            pl.BlockSpec((bm, attn.shape[1]), lambda i: (i, 0)),
            pl.BlockSpec(w.shape, lambda i: (0, 0)),
        ] + w_specs,
        out_specs=[pl.BlockSpec((bm, d), lambda i: (i, 0))] + w_out_specs,
        out_shape=[jax.ShapeDtypeStruct(h.shape, F32)] + w_out_shapes,
        compiler_params=_params(("parallel",)),
        name="out1",
    )(h, attn, w, *mlp_w)


def _mlp_kernel(h_ref, g_ref, wup_ref, wdn_ref, fg_ref, o_ref, xn_sc, *, final_norm):
    f = pl.program_id(1)

    def mlp_chunk(xn):
        u = jnp.dot(xn, wup_ref[...], preferred_element_type=F32)
        a = jnp.square(jnp.maximum(u, 0.0)).astype(BF16)
        return jnp.dot(a, wdn_ref[...], preferred_element_type=F32)

    @pl.when(f == 0)
    def _():
        x = h_ref[...]
        xn = _rms(x, g_ref[...]).astype(BF16)
        xn_sc[...] = xn
        o_ref[...] = x + mlp_chunk(xn)

    @pl.when(f > 0)
    def _():
        o_ref[...] += mlp_chunk(xn_sc[...])

    if final_norm:
        @pl.when(f == pl.num_programs(1) - 1)
        def _():
            o_ref[...] = _rms(o_ref[...], fg_ref[...])


def _mlp(h, g, wup, wdn, fg, *, bm, bf, final_norm):
    t, d = h.shape
    dff = wup.shape[1]
    return pl.pallas_call(
        functools.partial(_mlp_kernel, final_norm=final_norm),
        grid=(t // bm, dff // bf),
        in_specs=[
            pl.BlockSpec((bm, d), lambda i, f: (i, 0)),
            pl.BlockSpec((1, d), lambda i, f: (0, 0)),
            pl.BlockSpec((d, bf), lambda i, f: (0, f)),
            pl.BlockSpec((bf, d), lambda i, f: (f, 0)),
            pl.BlockSpec((1, d), lambda i, f: (0, 0)),
        ],
        out_specs=pl.BlockSpec((bm, d), lambda i, f: (i, 0)),
        out_shape=jax.ShapeDtypeStruct(h.shape, F32),
        scratch_shapes=[pltpu.VMEM((bm, d), BF16)],
        compiler_params=_params(("parallel", "arbitrary")),
        name="mlp_final" if final_norm else "mlp",
    )(h, g, wup, wdn, fg)


def _rope_cos_sin(seq, rot_dim):
    rows = seq // GRID_W
    axis_dim = rot_dim // 2
    inv_freq = ROPE_THETA ** (-jnp.arange(0, axis_dim, 2, dtype=F32) / axis_dim)
    ang_row = jnp.arange(rows, dtype=F32)[:, None] * inv_freq
    ang_col = jnp.arange(GRID_W, dtype=F32)[:, None] * inv_freq
    shape = (rows, GRID_W, inv_freq.shape[0])

    def per_token(fn):
        by_row = jnp.broadcast_to(fn(ang_row)[:, None, :], shape)
        by_col = jnp.broadcast_to(fn(ang_col)[None, :, :], shape)
        return jnp.concatenate([by_row, by_col], axis=-1).reshape(seq, rot_dim // 2)

    return per_token(jnp.cos), per_token(jnp.sin)


def _rope_tables(seq, rot_dim):
    cos, sin = _rope_cos_sin(seq, rot_dim)
    pad = jnp.zeros((seq, LANES // 2 - rot_dim // 2), F32)
    cos_t = jnp.concatenate([cos, pad, cos, pad], axis=-1)
    sin_t = jnp.concatenate([-sin, pad, sin, pad], axis=-1)
    return cos_t, sin_t


def _pair_split(w):
    return w[..., 0::2], w[..., 1::2]


def _pair_split_sources(rot_dim, pad):
    blank = [-1] * pad
    return list(range(0, rot_dim, 2)) + blank + list(range(1, rot_dim, 2)) + blank


_PAIR_SPLIT_128 = _pair_split_sources(HEAD_DIM, 0)
_ROPE_PAD_64 = _pair_split_sources(MLA_ROPE, LANES // 2 - MLA_ROPE // 2)
_MLA_Q_HEAD = list(range(MLA_NOPE)) + [MLA_NOPE + c if c >= 0 else -1 for c in _ROPE_PAD_64]


def _relayout(w, sources):
    sel = [[1.0 if src == k else 0.0 for src in sources] for k in range(w.shape[-1])]
    return jnp.dot(w, jnp.array(sel, BF16), preferred_element_type=BF16)


def kernel(x, even_norm_g, even_w_in, even_q_norm_g, even_k_norm_g, even_conv_w, even_w_out,
           odd_norm_g, odd_w_down, odd_q_lat_g, odd_kv_lat_g, odd_w_uq, odd_w_ukv, odd_w_o,
           mlp_norm_g, mlp_w_up, mlp_w_down, final_norm_g):
    b, s, d = x.shape
    t = _tiles(s)
    depth = mlp_norm_g.shape[0]
    cos_a, sin_a = _rope_tables(s, HEAD_DIM)
    cos_c, sin_c = _rope_tables(s, MLA_ROPE)
    fg = final_norm_g.reshape(1, d)
    mlp_w = (mlp_w_up, mlp_w_down)

    h = x
    for layer in range(depth):
        i = layer // 2
        if layer % 2 == 0:
            w_in = even_w_in[i].astype(BF16)
            wq = w_in[:, :ATTN_WIDTH].reshape(d, ATTN_HEADS, HEAD_DIM)
            wk = w_in[:, ATTN_WIDTH:ATTN_WIDTH + KV_WIDTH].reshape(d, ATTN_KV_HEADS, HEAD_DIM)
            wq = _relayout(wq, _PAIR_SPLIT_128).reshape(d, ATTN_WIDTH)
            wk = _relayout(wk, _PAIR_SPLIT_128).reshape(d, KV_WIDTH)
            w0 = jnp.concatenate([wq, wk, w_in[:, ATTN_WIDTH + KV_WIDTH:]], axis=-1)
            qg = jnp.concatenate(_pair_split(even_q_norm_g[i]), axis=-1).reshape(1, HEAD_DIM)
            kg = jnp.concatenate(_pair_split(even_k_norm_g[i]), axis=-1).reshape(1, HEAD_DIM)
            q, k, vt, gb, cin = _proj0(h, even_norm_g[i].reshape(1, d), w0, qg, kg,
                                       cos_a, sin_a, t["chunk"], t["kv_chunk"])
            attn = _attention(q, k, vt, group=ATTN_GROUP, bq=t["bq_gqa"], name="gqa_attn")
            h, w_up, w_dn = _out0(h, attn, gb, cin, even_conv_w[i], even_w_out[i].astype(BF16),
                                  mlp_w, layer, t["bm_out"])
        else:
            w_down = odd_w_down[i].astype(BF16)
            wd = jnp.concatenate([w_down[:, :Q_LORA + KV_LORA],
                                  _relayout(w_down[:, Q_LORA + KV_LORA:], _ROPE_PAD_64)], axis=-1)
            wuq = odd_w_uq[i].astype(BF16).reshape(Q_LORA, MLA_HEADS, MLA_NOPE + MLA_ROPE)
            wuq = _relayout(wuq, _MLA_Q_HEAD).reshape(Q_LORA, MLA_HEADS * MLA_QK_PAD)
            q, k, vt = _proj1(h, odd_norm_g[i].reshape(1, d), wd,
                              odd_q_lat_g[i].reshape(1, Q_LORA), odd_kv_lat_g[i].reshape(1, KV_LORA),
                              wuq, odd_w_ukv[i].astype(BF16), cos_c, sin_c, t["chunk"],
                              t["kv_chunk"])
            attn = _attention(q, k, vt, group=1, bq=t["bq_mla"], name="mla_attn")
            h, w_up, w_dn = _out1(h.reshape(b * s, d), attn.reshape(b * s, -1),
                                  odd_w_o[i].astype(BF16), mlp_w, layer, t["bm_out"])
        h = _mlp(h.reshape(b * s, d), mlp_norm_g[layer].reshape(1, d), w_up, w_dn, fg,
                 bm=t["bm_mlp"], bf=t["bf_mlp"],
                 final_norm=(layer == depth - 1)).reshape(b, s, d)
    return h
```

```python
import functools

import jax
import jax.numpy as jnp
from jax import lax
from jax.experimental import pallas as pl
from jax.experimental.pallas import tpu as pltpu

F32 = jnp.float32
BF16 = jnp.bfloat16

NORM_EPS = 1e-6
ROPE_THETA = 10000.0
GRID_W = 64

HEAD_DIM = 128
ATTN_HEADS = 8
ATTN_KV_HEADS = 2
ATTN_GROUP = ATTN_HEADS // ATTN_KV_HEADS
ATTN_WIDTH = ATTN_HEADS * HEAD_DIM
KV_WIDTH = ATTN_KV_HEADS * HEAD_DIM
CONV_WIDTH = 1024

MLA_HEADS = 16
MLA_NOPE = 128
MLA_ROPE = 64
MLA_V = 128
Q_LORA = 512
KV_LORA = 512
MLA_QK_PAD = 256

LANES = 128
BF16_SUBLANES = 16
VMEM_LIMIT_BYTES = 56 * 1024 * 1024
NEG_BIG = -1e30
LOG2_E = 1.4426950408889634
ATTN_COL_GROUP = 512
ATTN_TRIP_ITEMS = 32
ATTN_LOOKAHEAD = 2
ATTN_RING = 4


def _tiles(seq):
    return dict(
        chunk=min(512, seq),
        kv_chunk=min(512, seq),
        bm_mlp=min(1024, seq),
        bf_mlp=512,
        bm_out=min(512, seq),
        bq_gqa=min(1024, seq),
        bq_mla=min(4096, seq),
    )


def _params(sem):
    return pltpu.CompilerParams(dimension_semantics=sem, vmem_limit_bytes=VMEM_LIMIT_BYTES)


def _rms(x, g):
    return x * lax.rsqrt(jnp.mean(x * x, axis=-1, keepdims=True) + NORM_EPS) * g


def _store_vt(vt_ref, head, v):
    vt = v.T.astype(BF16)
    for i in range(vt_ref.shape[1]):
        vt_ref[head, i] = vt[:, i * LANES:(i + 1) * LANES]


def _rope(x, cos, sin):
    return x * cos + pltpu.roll(x, LANES // 2, 1) * sin


def _proj0_kernel(h_ref, g_ref, w_ref, qg_ref, kg_ref, cos_ref, sin_ref,
                  q_ref, k_ref, vt_ref, gb_ref, cin_ref):
    xn = _rms(h_ref[...], g_ref[...]).astype(BF16)
    y = jnp.dot(xn, w_ref[...], preferred_element_type=F32)
    cos = cos_ref[...]
    sin = sin_ref[...]
    scale = HEAD_DIM ** -0.5 * LOG2_E
    for hh in range(ATTN_HEADS):
        yh = _rms(y[:, hh * HEAD_DIM:(hh + 1) * HEAD_DIM], qg_ref[...])
        q_ref[hh] = (_rope(yh, cos, sin) * scale).astype(BF16)
    o = ATTN_WIDTH
    for hh in range(ATTN_KV_HEADS):
        yh = _rms(y[:, o + hh * HEAD_DIM:o + (hh + 1) * HEAD_DIM], kg_ref[...])
        k_ref[hh, 0] = _rope(yh, cos, sin).astype(BF16)
    o += KV_WIDTH
    for hh in range(ATTN_KV_HEADS):
        _store_vt(vt_ref, hh, y[:, o + hh * HEAD_DIM:o + (hh + 1) * HEAD_DIM])
    o += KV_WIDTH
    gb_ref[...] = y[:, o:o + CONV_WIDTH].astype(BF16)
    o += CONV_WIDTH
    cin_ref[...] = (y[:, o:o + CONV_WIDTH] * y[:, o + CONV_WIDTH:o + 2 * CONV_WIDTH]).astype(BF16)


def _proj0(h, g, w, qg, kg, cos, sin, chunk, kv_chunk):
    b, s, d = h.shape
    n = w.shape[1]
    nc = s // chunk
    r = kv_chunk // chunk
    return pl.pallas_call(
        _proj0_kernel,
        grid=(b, nc),
        in_specs=[
            pl.BlockSpec((None, chunk, d), lambda bi, i: (bi, i, 0)),
            pl.BlockSpec((1, d), lambda bi, i: (0, 0)),
            pl.BlockSpec((d, n), lambda bi, i: (0, 0)),
            pl.BlockSpec((1, HEAD_DIM), lambda bi, i: (0, 0)),
            pl.BlockSpec((1, HEAD_DIM), lambda bi, i: (0, 0)),
            pl.BlockSpec((chunk, HEAD_DIM), lambda bi, i: (i, 0)),
            pl.BlockSpec((chunk, HEAD_DIM), lambda bi, i: (i, 0)),
        ],
        out_specs=[
            pl.BlockSpec((None, ATTN_HEADS, chunk, HEAD_DIM), lambda bi, i: (bi, 0, i, 0)),
            pl.BlockSpec((None, ATTN_KV_HEADS, 1, chunk, LANES), lambda bi, i: (bi, 0, 0, i, 0)),
            pl.BlockSpec((None, ATTN_KV_HEADS, None, chunk // LANES, HEAD_DIM, LANES),
                         lambda bi, i: (bi, 0, i // r, i % r, 0, 0)),
            pl.BlockSpec((None, chunk, CONV_WIDTH), lambda bi, i: (bi, i, 0)),
            pl.BlockSpec((None, chunk, CONV_WIDTH), lambda bi, i: (bi, i, 0)),
        ],
        out_shape=[
            jax.ShapeDtypeStruct((b, ATTN_HEADS, s, HEAD_DIM), BF16),
            jax.ShapeDtypeStruct((b, ATTN_KV_HEADS, 1, s, LANES), BF16),
            jax.ShapeDtypeStruct((b, ATTN_KV_HEADS, s // kv_chunk, kv_chunk // LANES, HEAD_DIM, LANES),
                                 BF16),
            jax.ShapeDtypeStruct((b, s, CONV_WIDTH), BF16),
            jax.ShapeDtypeStruct((b, s, CONV_WIDTH), BF16),
        ],
        compiler_params=_params(("parallel", "parallel")),
        name="proj0",
    )(h, g, w, qg, kg, cos, sin)


def _proj1_kernel(h_ref, g_ref, wd_ref, qg_ref, kvg_ref, wuq_ref, wukv_ref, cos_ref, sin_ref,
                  q_ref, k_ref, vt_ref):
    xn = _rms(h_ref[...], g_ref[...]).astype(BF16)
    lat = jnp.dot(xn, wd_ref[...], preferred_element_type=F32)
    cq = _rms(lat[:, :Q_LORA], qg_ref[...]).astype(BF16)
    ckv = _rms(lat[:, Q_LORA:Q_LORA + KV_LORA], kvg_ref[...]).astype(BF16)
    cos = cos_ref[...]
    sin = sin_ref[...]
    kr = _rope(lat[:, Q_LORA + KV_LORA:], cos, sin).astype(BF16)
    q = jnp.dot(cq, wuq_ref[...], preferred_element_type=F32)
    kv = jnp.dot(ckv, wukv_ref[...], preferred_element_type=F32)
    scale = (MLA_NOPE + MLA_ROPE) ** -0.5 * LOG2_E
    for hh in range(MLA_HEADS):
        o = hh * MLA_QK_PAD
        q_ref[hh, :, :MLA_NOPE] = (q[:, o:o + MLA_NOPE] * scale).astype(BF16)
        qr = _rope(q[:, o + MLA_NOPE:o + MLA_QK_PAD], cos, sin)
        q_ref[hh, :, MLA_NOPE:] = (qr * scale).astype(BF16)
        o = hh * (MLA_NOPE + MLA_V)
        k_ref[hh, 0] = kv[:, o:o + MLA_NOPE].astype(BF16)
        k_ref[hh, 1] = kr
        _store_vt(vt_ref, hh, kv[:, o + MLA_NOPE:o + MLA_NOPE + MLA_V])


def _proj1(h, g, wd, qg, kvg, wuq, wukv, cos, sin, chunk, kv_chunk):
    b, s, d = h.shape
    nc = s // chunk
    r = kv_chunk // chunk
    const = lambda bi, i: (0, 0)
    return pl.pallas_call(
        _proj1_kernel,
        grid=(b, nc),
        in_specs=[
            pl.BlockSpec((None, chunk, d), lambda bi, i: (bi, i, 0)),
            pl.BlockSpec((1, d), const),
            pl.BlockSpec(wd.shape, const),
            pl.BlockSpec((1, Q_LORA), const),
            pl.BlockSpec((1, KV_LORA), const),
            pl.BlockSpec(wuq.shape, const),
            pl.BlockSpec(wukv.shape, const),
            pl.BlockSpec((chunk, LANES), lambda bi, i: (i, 0)),
            pl.BlockSpec((chunk, LANES), lambda bi, i: (i, 0)),
        ],
        out_specs=[
            pl.BlockSpec((None, MLA_HEADS, chunk, MLA_QK_PAD), lambda bi, i: (bi, 0, i, 0)),
            pl.BlockSpec((None, MLA_HEADS, MLA_QK_PAD // LANES, chunk, LANES),
                         lambda bi, i: (bi, 0, 0, i, 0)),
            pl.BlockSpec((None, MLA_HEADS, None, chunk // LANES, MLA_V, LANES),
                         lambda bi, i: (bi, 0, i // r, i % r, 0, 0)),
        ],
        out_shape=[
            jax.ShapeDtypeStruct((b, MLA_HEADS, s, MLA_QK_PAD), BF16),
            jax.ShapeDtypeStruct((b, MLA_HEADS, MLA_QK_PAD // LANES, s, LANES), BF16),
            jax.ShapeDtypeStruct((b, MLA_HEADS, s // kv_chunk, kv_chunk // LANES, MLA_V, LANES), BF16),
        ],
        compiler_params=_params(("parallel", "parallel")),
        name="proj1",
    )(h, g, wd, qg, kvg, wuq, wukv, cos, sin)


def _cast_weight_slices(wup_ref, wdn_ref, wup_o_ref, wdn_o_ref):
    wup_o_ref[...] = wup_ref[...].astype(BF16)
    wdn_o_ref[...] = wdn_ref[...].astype(BF16)


def _weight_cast_specs(mlp_w, layer, steps, step_index):
    in_specs, out_specs, out_shapes = [], [], []
    for w in mlp_w:
        rows, cols = w.shape[1:]
        assert rows % (steps * BF16_SUBLANES) == 0
        blk = rows // steps
        in_specs.append(pl.BlockSpec((None, blk, cols),
                                     lambda *g: (layer, step_index(*g), 0)))
        out_specs.append(pl.BlockSpec((blk, cols), lambda *g: (step_index(*g), 0)))
        out_shapes.append(jax.ShapeDtypeStruct((rows, cols), BF16))
    return in_specs, out_specs, out_shapes


def _attn_kernel(q_ref, k_ref, vt_ref, wup_ref, wdn_ref, o_ref, wup_o_ref, wdn_o_ref,
                 m_sc, acc_sc, s_sc, mx_sc, qt_sc, *, group, bq, chunk, n_chunks, unroll):
    _cast_weight_slices(wup_ref, wdn_ref, wup_o_ref, wdn_o_ref)
    n = group * bq
    gw = s_sc.shape[2]
    n_groups = n // gw
    ring = s_sc.shape[0]
    m_sc[...] = jnp.full(m_sc.shape, NEG_BIG, F32)
    acc_sc[...] = jnp.zeros(acc_sc.shape, F32)
    for gi in range(n_groups):
        qg = q_ref[(gi * gw) // bq, pl.ds((gi * gw) % bq, gw), :]
        qt_sc[:, gi * gw:(gi + 1) * gw] = qg.T

    ones_rows = (lax.broadcasted_iota(jnp.int32, (BF16_SUBLANES, chunk), 0) == 0).astype(BF16)

    items = [(j, gi) for j in range(unroll) for gi in range(n_groups)]
    assert len(items) % ring == 0
    lookahead = min(ATTN_LOOKAHEAD, ring - 1)

    def scores(c, gi, slot):
        rows = pl.ds(pl.multiple_of(c * chunk, chunk), chunk)
        kc = jnp.concatenate([k_ref[i, rows, :] for i in range(k_ref.shape[0])], axis=1)
        s = jnp.dot(kc, qt_sc[:, gi * gw:(gi + 1) * gw], preferred_element_type=F32)
        s_sc[slot] = s
        mx_sc[slot] = jnp.max(s, axis=0, keepdims=True)

    for idx in range(lookahead):
        scores(items[idx][0], items[idx][1], idx)

    def body(t, carry):
        for idx, (j, gi) in enumerate(items):
            ahead = idx + lookahead
            ja, ga = items[ahead % len(items)]
            ca = jnp.minimum((t + ahead // len(items)) * unroll + ja, n_chunks - 1)
            scores(ca, ga, ahead % ring)

            cols = slice(gi * gw, (gi + 1) * gw)
            slot = idx % ring
            m_prev = m_sc[:, cols]
            m_new = jnp.maximum(m_prev, mx_sc[slot])
            alpha = jnp.exp2(m_prev - m_new)
            p = jnp.exp2(s_sc[slot] - m_new)
            vt = jnp.concatenate([vt_ref[t * unroll + j, i] for i in range(vt_ref.shape[1])], axis=1)
            vt = jnp.concatenate([vt, ones_rows], axis=0)
            pv = jnp.dot(vt, p.astype(BF16), preferred_element_type=F32)
            acc_sc[:, cols] = alpha * acc_sc[:, cols] + pv
            m_sc[:, cols] = m_new
        return carry

    lax.fori_loop(0, n_chunks // unroll, body, 0)
    dv = acc_sc.shape[0] - BF16_SUBLANES
    o = acc_sc[:dv, :] / acc_sc[dv:dv + 1, :]
    for gi in range(group):
        o_ref[:, gi * dv:(gi + 1) * dv] = o[:, gi * bq:(gi + 1) * bq].T.astype(o_ref.dtype)


def _attention(q, k, vt, mlp_w, layer, *, group, bq, name):
    b, h, s, dqk = q.shape
    hkv = k.shape[1]
    n_chunks, slabs, dv, _ = vt.shape[2:]
    chunk = slabs * LANES
    n = group * bq
    gw = min(ATTN_COL_GROUP, bq)
    unroll = max(1, ATTN_TRIP_ITEMS // (n // gw))
    while n_chunks % unroll:
        unroll -= 1
    items = unroll * (n // gw)
    ring = ATTN_RING if items % ATTN_RING == 0 else items
    kern = functools.partial(_attn_kernel, group=group, bq=bq, chunk=chunk, n_chunks=n_chunks,
                             unroll=unroll)
    nq = s // bq
    w_specs, w_out_specs, w_out_shapes = _weight_cast_specs(
        mlp_w, layer, b * hkv * nq, lambda bi, hi, qi: (bi * hkv + hi) * nq + qi)
    return pl.pallas_call(
        kern,
        grid=(b, hkv, s // bq),
        in_specs=[
            pl.BlockSpec((None, group, bq, dqk), lambda bi, hi, qi: (bi, hi, qi, 0)),
            pl.BlockSpec((None, None, dqk // LANES, s, LANES), lambda bi, hi, qi: (bi, hi, 0, 0, 0)),
            pl.BlockSpec((None, None, n_chunks, slabs, dv, LANES),
                         lambda bi, hi, qi: (bi, hi, 0, 0, 0, 0)),
        ] + w_specs,
        out_specs=[pl.BlockSpec((None, bq, group * dv), lambda bi, hi, qi: (bi, qi, hi))] + w_out_specs,
        out_shape=[jax.ShapeDtypeStruct((b, s, h * dv), BF16)] + w_out_shapes,
        scratch_shapes=[
            pltpu.VMEM((1, n), F32),
            pltpu.VMEM((dv + BF16_SUBLANES, n), F32),
            pltpu.VMEM((ring, chunk, gw), F32),
            pltpu.VMEM((ring, 1, gw), F32),
            pltpu.VMEM((dqk, n), BF16),
        ],
        compiler_params=_params(("parallel", "parallel", "arbitrary")),
        name=name,
    )(q, k, vt, *mlp_w)


def _out0_kernel(h_ref, a_ref, gb_ref, cin_ref, cprev_ref, cnext_ref, cw_ref, w_ref, o_ref):
    i = pl.program_id(1)
    bm = cin_ref.shape[0]
    c = cin_ref[...].astype(F32)
    prev_row = cprev_ref[BF16_SUBLANES - 1:BF16_SUBLANES, :].astype(F32)
    next_row = cnext_ref[0:1, :].astype(F32)
    prev_row = jnp.where(i == 0, 0.0, prev_row)
    next_row = jnp.where(i == pl.num_programs(1) - 1, 0.0, next_row)
    rows = lax.broadcasted_iota(jnp.int32, (bm, 1), 0)
    c_m1 = jnp.where(rows == 0, prev_row, pltpu.roll(c, 1, 0))
    c_p1 = jnp.where(rows == bm - 1, next_row, pltpu.roll(c, bm - 1, 0))
    conv = cw_ref[0:1, :] * c_m1 + cw_ref[1:2, :] * c + cw_ref[2:3, :] * c_p1
    sconv = (gb_ref[...].astype(F32) * conv).astype(BF16)
    aw = a_ref.shape[1]
    y = jnp.dot(a_ref[...], w_ref[:aw, :], preferred_element_type=F32)
    y = y + jnp.dot(sconv, w_ref[aw:, :], preferred_element_type=F32)
    o_ref[...] = h_ref[...] + y


def _out0(h, attn, gb, cin, conv_w, w, bm):
    b, s, d = h.shape
    nb = s // bm
    r = bm // BF16_SUBLANES
    last = s // BF16_SUBLANES - 1
    tile = lambda width: pl.BlockSpec((None, bm, width), lambda bi, i: (bi, i, 0))
    return pl.pallas_call(
        _out0_kernel,
        grid=(b, nb),
        in_specs=[
            tile(d), tile(attn.shape[2]), tile(CONV_WIDTH), tile(CONV_WIDTH),
            pl.BlockSpec((None, BF16_SUBLANES, CONV_WIDTH),
                         lambda bi, i: (bi, jnp.maximum(i * r - 1, 0), 0)),
            pl.BlockSpec((None, BF16_SUBLANES, CONV_WIDTH),
                         lambda bi, i: (bi, jnp.minimum((i + 1) * r, last), 0)),
            pl.BlockSpec(conv_w.shape, lambda bi, i: (0, 0)),
            pl.BlockSpec(w.shape, lambda bi, i: (0, 0)),
        ],
        out_specs=tile(d),
        out_shape=jax.ShapeDtypeStruct(h.shape, F32),
        compiler_params=_params(("parallel", "parallel")),
        name="out0",
    )(h, attn, gb, cin, cin, cin, conv_w, w)


def _out1_kernel(h_ref, a_ref, w_ref, o_ref):
    o_ref[...] = h_ref[...] + jnp.dot(a_ref[...], w_ref[...], preferred_element_type=F32)


def _out1(h, attn, w, bm):
    t, d = h.shape
    return pl.pallas_call(
        _out1_kernel,
        grid=(t // bm,),
        in_specs=[
            pl.BlockSpec((bm, d), lambda i: (i, 0)),
            pl.BlockSpec((bm, attn.shape[1]), lambda i: (i, 0)),
            pl.BlockSpec(w.shape, lambda i: (0, 0)),
        ],
        out_specs=pl.BlockSpec((bm, d), lambda i: (i, 0)),
        out_shape=jax.ShapeDtypeStruct(h.shape, F32),
        compiler_params=_params(("parallel",)),
        name="out1",
    )(h, attn, w)


def _mlp_kernel(h_ref, g_ref, wup_ref, wdn_ref, fg_ref, o_ref, xn_sc, *, final_norm):
    f = pl.program_id(1)

    def mlp_chunk(xn):
        u = jnp.dot(xn, wup_ref[...], preferred_element_type=F32)
        a = jnp.square(jnp.maximum(u, 0.0)).astype(BF16)
        return jnp.dot(a, wdn_ref[...], preferred_element_type=F32)

    @pl.when(f == 0)
    def _():
        x = h_ref[...]
        xn = _rms(x, g_ref[...]).astype(BF16)
        xn_sc[...] = xn
        o_ref[...] = x + mlp_chunk(xn)

    @pl.when(f > 0)
    def _():
        o_ref[...] += mlp_chunk(xn_sc[...])

    if final_norm:
        @pl.when(f == pl.num_programs(1) - 1)
        def _():
            o_ref[...] = _rms(o_ref[...], fg_ref[...])


def _mlp(h, g, wup, wdn, fg, *, bm, bf, final_norm):
    t, d = h.shape
    dff = wup.shape[1]
    return pl.pallas_call(
        functools.partial(_mlp_kernel, final_norm=final_norm),
        grid=(t // bm, dff // bf),
        in_specs=[
            pl.BlockSpec((bm, d), lambda i, f: (i, 0)),
            pl.BlockSpec((1, d), lambda i, f: (0, 0)),
            pl.BlockSpec((d, bf), lambda i, f: (0, f)),
            pl.BlockSpec((bf, d), lambda i, f: (f, 0)),
            pl.BlockSpec((1, d), lambda i, f: (0, 0)),
        ],
        out_specs=pl.BlockSpec((bm, d), lambda i, f: (i, 0)),
        out_shape=jax.ShapeDtypeStruct(h.shape, F32),
        scratch_shapes=[pltpu.VMEM((bm, d), BF16)],
        compiler_params=_params(("parallel", "arbitrary")),
        name="mlp_final" if final_norm else "mlp",
    )(h, g, wup, wdn, fg)


def _rope_cos_sin(seq, rot_dim):
    rows = seq // GRID_W
    axis_dim = rot_dim // 2
    inv_freq = ROPE_THETA ** (-jnp.arange(0, axis_dim, 2, dtype=F32) / axis_dim)
    ang_row = jnp.arange(rows, dtype=F32)[:, None] * inv_freq
    ang_col = jnp.arange(GRID_W, dtype=F32)[:, None] * inv_freq
    shape = (rows, GRID_W, inv_freq.shape[0])

    def per_token(fn):
        by_row = jnp.broadcast_to(fn(ang_row)[:, None, :], shape)
        by_col = jnp.broadcast_to(fn(ang_col)[None, :, :], shape)
        return jnp.concatenate([by_row, by_col], axis=-1).reshape(seq, rot_dim // 2)

    return per_token(jnp.cos), per_token(jnp.sin)


def _rope_tables(seq, rot_dim):
    cos, sin = _rope_cos_sin(seq, rot_dim)
    pad = jnp.zeros((seq, LANES // 2 - rot_dim // 2), F32)
    cos_t = jnp.concatenate([cos, pad, cos, pad], axis=-1)
    sin_t = jnp.concatenate([-sin, pad, sin, pad], axis=-1)
    return cos_t, sin_t


def _pair_split(w):
    return w[..., 0::2], w[..., 1::2]


def _pair_split_sources(rot_dim, pad):
    blank = [-1] * pad
    return list(range(0, rot_dim, 2)) + blank + list(range(1, rot_dim, 2)) + blank


_PAIR_SPLIT_128 = _pair_split_sources(HEAD_DIM, 0)
_ROPE_PAD_64 = _pair_split_sources(MLA_ROPE, LANES // 2 - MLA_ROPE // 2)
_MLA_Q_HEAD = list(range(MLA_NOPE)) + [MLA_NOPE + c if c >= 0 else -1 for c in _ROPE_PAD_64]


def _relayout(w, sources):
    sel = [[1.0 if src == k else 0.0 for src in sources] for k in range(w.shape[-1])]
    return jnp.dot(w, jnp.array(sel, BF16), preferred_element_type=BF16)


def kernel(x, even_norm_g, even_w_in, even_q_norm_g, even_k_norm_g, even_conv_w, even_w_out,
           odd_norm_g, odd_w_down, odd_q_lat_g, odd_kv_lat_g, odd_w_uq, odd_w_ukv, odd_w_o,
           mlp_norm_g, mlp_w_up, mlp_w_down, final_norm_g):
    b, s, d = x.shape
    t = _tiles(s)
    depth = mlp_norm_g.shape[0]
    cos_a, sin_a = _rope_tables(s, HEAD_DIM)
    cos_c, sin_c = _rope_tables(s, MLA_ROPE)
    fg = final_norm_g.reshape(1, d)
    mlp_w = (mlp_w_up, mlp_w_down)

    h = x
    for layer in range(depth):
        i = layer // 2
        if layer % 2 == 0:
            w_in = even_w_in[i].astype(BF16)
            wq = w_in[:, :ATTN_WIDTH].reshape(d, ATTN_HEADS, HEAD_DIM)
            wk = w_in[:, ATTN_WIDTH:ATTN_WIDTH + KV_WIDTH].reshape(d, ATTN_KV_HEADS, HEAD_DIM)
            wq = _relayout(wq, _PAIR_SPLIT_128).reshape(d, ATTN_WIDTH)
            wk = _relayout(wk, _PAIR_SPLIT_128).reshape(d, KV_WIDTH)
            w0 = jnp.concatenate([wq, wk, w_in[:, ATTN_WIDTH + KV_WIDTH:]], axis=-1)
            qg = jnp.concatenate(_pair_split(even_q_norm_g[i]), axis=-1).reshape(1, HEAD_DIM)
            kg = jnp.concatenate(_pair_split(even_k_norm_g[i]), axis=-1).reshape(1, HEAD_DIM)
            q, k, vt, gb, cin = _proj0(h, even_norm_g[i].reshape(1, d), w0, qg, kg,
                                       cos_a, sin_a, t["chunk"], t["kv_chunk"])
            attn, w_up, w_dn = _attention(q, k, vt, mlp_w, layer, group=ATTN_GROUP,
                                          bq=t["bq_gqa"], name="gqa_attn")
            h = _out0(h, attn, gb, cin, even_conv_w[i], even_w_out[i].astype(BF16), t["bm_out"])
        else:
            w_down = odd_w_down[i].astype(BF16)
            wd = jnp.concatenate([w_down[:, :Q_LORA + KV_LORA],
                                  _relayout(w_down[:, Q_LORA + KV_LORA:], _ROPE_PAD_64)], axis=-1)
            wuq = odd_w_uq[i].astype(BF16).reshape(Q_LORA, MLA_HEADS, MLA_NOPE + MLA_ROPE)
            wuq = _relayout(wuq, _MLA_Q_HEAD).reshape(Q_LORA, MLA_HEADS * MLA_QK_PAD)
            q, k, vt = _proj1(h, odd_norm_g[i].reshape(1, d), wd,
                              odd_q_lat_g[i].reshape(1, Q_LORA), odd_kv_lat_g[i].reshape(1, KV_LORA),
                              wuq, odd_w_ukv[i].astype(BF16), cos_c, sin_c, t["chunk"],
                              t["kv_chunk"])
            attn, w_up, w_dn = _attention(q, k, vt, mlp_w, layer, group=1, bq=t["bq_mla"],
                                          name="mla_attn")
            h = _out1(h.reshape(b * s, d), attn.reshape(b * s, -1), odd_w_o[i].astype(BF16),
                      t["bm_out"])
        h = _mlp(h.reshape(b * s, d), mlp_norm_g[layer].reshape(1, d), w_up, w_dn, fg,
                 bm=t["bm_mlp"], bf=t["bf_mlp"],
                 final_norm=(layer == depth - 1)).reshape(b, s, d)
    return h
```

```python
import functools

import jax
import jax.numpy as jnp
from jax import lax
from jax.experimental import pallas as pl
from jax.experimental.pallas import tpu as pltpu

F32 = jnp.float32
BF16 = jnp.bfloat16

NORM_EPS = 1e-6
ROPE_THETA = 10000.0
GRID_W = 64

HEAD_DIM = 128
ATTN_HEADS = 8
ATTN_KV_HEADS = 2
ATTN_GROUP = ATTN_HEADS // ATTN_KV_HEADS
ATTN_WIDTH = ATTN_HEADS * HEAD_DIM
KV_WIDTH = ATTN_KV_HEADS * HEAD_DIM
CONV_WIDTH = 1024

MLA_HEADS = 16
MLA_NOPE = 128
MLA_ROPE = 64
MLA_V = 128
Q_LORA = 512
KV_LORA = 512
MLA_QK_PAD = 256

LANES = 128
BF16_SUBLANES = 16
VMEM_LIMIT_BYTES = 56 * 1024 * 1024
NEG_BIG = -1e30
LOG2_E = 1.4426950408889634
ATTN_COL_GROUP = 512
ATTN_TRIP_ITEMS = 32
ATTN_LOOKAHEAD = 1
ATTN_RING = 4


def _tiles(seq):
    return dict(
        chunk=min(512, seq),
        kv_chunk=min(512, seq),
        bm_mlp=min(1024, seq),
        bf_mlp=512,
        bm_out=min(512, seq),
        bq_gqa=min(1024, seq),
        bq_mla=min(4096, seq),
    )


def _params(sem):
    return pltpu.CompilerParams(dimension_semantics=sem, vmem_limit_bytes=VMEM_LIMIT_BYTES)


def _rms(x, g):
    return x * lax.rsqrt(jnp.mean(x * x, axis=-1, keepdims=True) + NORM_EPS) * g


def _store_vt(vt_ref, head, v):
    vt = v.T.astype(BF16)
    for i in range(vt_ref.shape[1]):
        vt_ref[head, i] = vt[:, i * LANES:(i + 1) * LANES]


def _rope(x, cos, sin):
    return x * cos + pltpu.roll(x, LANES // 2, 1) * sin


def _proj0_kernel(h_ref, g_ref, w_ref, qg_ref, kg_ref, cos_ref, sin_ref,
                  q_ref, k_ref, vt_ref, gb_ref, cin_ref):
    xn = _rms(h_ref[...], g_ref[...]).astype(BF16)
    y = jnp.dot(xn, w_ref[...], preferred_element_type=F32)
    cos = cos_ref[...]
    sin = sin_ref[...]
    scale = HEAD_DIM ** -0.5 * LOG2_E
    for hh in range(ATTN_HEADS):
        yh = _rms(y[:, hh * HEAD_DIM:(hh + 1) * HEAD_DIM], qg_ref[...])
        q_ref[hh] = (_rope(yh, cos, sin) * scale).astype(BF16)
    o = ATTN_WIDTH
    for hh in range(ATTN_KV_HEADS):
        yh = _rms(y[:, o + hh * HEAD_DIM:o + (hh + 1) * HEAD_DIM], kg_ref[...])
        k_ref[hh, 0] = _rope(yh, cos, sin).astype(BF16)
    o += KV_WIDTH
    for hh in range(ATTN_KV_HEADS):
        _store_vt(vt_ref, hh, y[:, o + hh * HEAD_DIM:o + (hh + 1) * HEAD_DIM])
    o += KV_WIDTH
    gb_ref[...] = y[:, o:o + CONV_WIDTH].astype(BF16)
    o += CONV_WIDTH
    cin_ref[...] = (y[:, o:o + CONV_WIDTH] * y[:, o + CONV_WIDTH:o + 2 * CONV_WIDTH]).astype(BF16)


def _proj0(h, g, w, qg, kg, cos, sin, chunk, kv_chunk):
    b, s, d = h.shape
    n = w.shape[1]
    nc = s // chunk
    r = kv_chunk // chunk
    return pl.pallas_call(
        _proj0_kernel,
        grid=(b, nc),
        in_specs=[
            pl.BlockSpec((None, chunk, d), lambda bi, i: (bi, i, 0)),
            pl.BlockSpec((1, d), lambda bi, i: (0, 0)),
            pl.BlockSpec((d, n), lambda bi, i: (0, 0)),
            pl.BlockSpec((1, HEAD_DIM), lambda bi, i: (0, 0)),
            pl.BlockSpec((1, HEAD_DIM), lambda bi, i: (0, 0)),
            pl.BlockSpec((chunk, HEAD_DIM), lambda bi, i: (i, 0)),
            pl.BlockSpec((chunk, HEAD_DIM), lambda bi, i: (i, 0)),
        ],
        out_specs=[
            pl.BlockSpec((None, ATTN_HEADS, chunk, HEAD_DIM), lambda bi, i: (bi, 0, i, 0)),
            pl.BlockSpec((None, ATTN_KV_HEADS, 1, chunk, LANES), lambda bi, i: (bi, 0, 0, i, 0)),
            pl.BlockSpec((None, ATTN_KV_HEADS, None, chunk // LANES, HEAD_DIM, LANES),
                         lambda bi, i: (bi, 0, i // r, i % r, 0, 0)),
            pl.BlockSpec((None, chunk, CONV_WIDTH), lambda bi, i: (bi, i, 0)),
            pl.BlockSpec((None, chunk, CONV_WIDTH), lambda bi, i: (bi, i, 0)),
        ],
        out_shape=[
            jax.ShapeDtypeStruct((b, ATTN_HEADS, s, HEAD_DIM), BF16),
            jax.ShapeDtypeStruct((b, ATTN_KV_HEADS, 1, s, LANES), BF16),
            jax.ShapeDtypeStruct((b, ATTN_KV_HEADS, s // kv_chunk, kv_chunk // LANES, HEAD_DIM, LANES),
                                 BF16),
            jax.ShapeDtypeStruct((b, s, CONV_WIDTH), BF16),
            jax.ShapeDtypeStruct((b, s, CONV_WIDTH), BF16),
        ],
        compiler_params=_params(("parallel", "parallel")),
        name="proj0",
    )(h, g, w, qg, kg, cos, sin)


def _proj1_kernel(h_ref, g_ref, wd_ref, qg_ref, kvg_ref, wuq_ref, wukv_ref, cos_ref, sin_ref,
                  q_ref, k_ref, vt_ref):
    xn = _rms(h_ref[...], g_ref[...]).astype(BF16)
    lat = jnp.dot(xn, wd_ref[...], preferred_element_type=F32)
    cq = _rms(lat[:, :Q_LORA], qg_ref[...]).astype(BF16)
    ckv = _rms(lat[:, Q_LORA:Q_LORA + KV_LORA], kvg_ref[...]).astype(BF16)
    cos = cos_ref[...]
    sin = sin_ref[...]
    kr = _rope(lat[:, Q_LORA + KV_LORA:], cos, sin).astype(BF16)
    q = jnp.dot(cq, wuq_ref[...], preferred_element_type=F32)
    kv = jnp.dot(ckv, wukv_ref[...], preferred_element_type=F32)
    scale = (MLA_NOPE + MLA_ROPE) ** -0.5 * LOG2_E
    for hh in range(MLA_HEADS):
        o = hh * MLA_QK_PAD
        q_ref[hh, :, :MLA_NOPE] = (q[:, o:o + MLA_NOPE] * scale).astype(BF16)
        qr = _rope(q[:, o + MLA_NOPE:o + MLA_QK_PAD], cos, sin)
        q_ref[hh, :, MLA_NOPE:] = (qr * scale).astype(BF16)
        o = hh * (MLA_NOPE + MLA_V)
        k_ref[hh, 0] = kv[:, o:o + MLA_NOPE].astype(BF16)
        k_ref[hh, 1] = kr
        _store_vt(vt_ref, hh, kv[:, o + MLA_NOPE:o + MLA_NOPE + MLA_V])


def _proj1(h, g, wd, qg, kvg, wuq, wukv, cos, sin, chunk, kv_chunk):
    b, s, d = h.shape
    nc = s // chunk
    r = kv_chunk // chunk
    const = lambda bi, i: (0, 0)
    return pl.pallas_call(
        _proj1_kernel,
        grid=(b, nc),
        in_specs=[
            pl.BlockSpec((None, chunk, d), lambda bi, i: (bi, i, 0)),
            pl.BlockSpec((1, d), const),
            pl.BlockSpec(wd.shape, const),
            pl.BlockSpec((1, Q_LORA), const),
            pl.BlockSpec((1, KV_LORA), const),
            pl.BlockSpec(wuq.shape, const),
            pl.BlockSpec(wukv.shape, const),
            pl.BlockSpec((chunk, LANES), lambda bi, i: (i, 0)),
            pl.BlockSpec((chunk, LANES), lambda bi, i: (i, 0)),
        ],
        out_specs=[
            pl.BlockSpec((None, MLA_HEADS, chunk, MLA_QK_PAD), lambda bi, i: (bi, 0, i, 0)),
            pl.BlockSpec((None, MLA_HEADS, MLA_QK_PAD // LANES, chunk, LANES),
                         lambda bi, i: (bi, 0, 0, i, 0)),
            pl.BlockSpec((None, MLA_HEADS, None, chunk // LANES, MLA_V, LANES),
                         lambda bi, i: (bi, 0, i // r, i % r, 0, 0)),
        ],
        out_shape=[
            jax.ShapeDtypeStruct((b, MLA_HEADS, s, MLA_QK_PAD), BF16),
            jax.ShapeDtypeStruct((b, MLA_HEADS, MLA_QK_PAD // LANES, s, LANES), BF16),
            jax.ShapeDtypeStruct((b, MLA_HEADS, s // kv_chunk, kv_chunk // LANES, MLA_V, LANES), BF16),
        ],
        compiler_params=_params(("parallel", "parallel")),
        name="proj1",
    )(h, g, wd, qg, kvg, wuq, wukv, cos, sin)


def _attn_kernel(q_ref, k_ref, vt_ref, o_ref, m_sc, acc_sc, s_sc, mx_sc, qt_sc, *, group, bq,
                 chunk, n_chunks, unroll):
    n = group * bq
    gw = s_sc.shape[2]
    n_groups = n // gw
    ring = s_sc.shape[0]
    m_sc[...] = jnp.full(m_sc.shape, NEG_BIG, F32)
    acc_sc[...] = jnp.zeros(acc_sc.shape, F32)
    for gi in range(n_groups):
        qg = q_ref[(gi * gw) // bq, pl.ds((gi * gw) % bq, gw), :]
        qt_sc[:, gi * gw:(gi + 1) * gw] = qg.T

    ones_rows = (lax.broadcasted_iota(jnp.int32, (BF16_SUBLANES, chunk), 0) == 0).astype(BF16)

    items = [(j, gi) for j in range(unroll) for gi in range(n_groups)]
    assert len(items) % ring == 0
    lookahead = min(ATTN_LOOKAHEAD, ring - 1)

    def scores(c, gi, slot):
        rows = pl.ds(pl.multiple_of(c * chunk, chunk), chunk)
        kc = jnp.concatenate([k_ref[i, rows, :] for i in range(k_ref.shape[0])], axis=1)
        s = jnp.dot(kc, qt_sc[:, gi * gw:(gi + 1) * gw], preferred_element_type=F32)
        s_sc[slot] = s
        mx_sc[slot] = jnp.max(s, axis=0, keepdims=True)

    for idx in range(lookahead):
        scores(items[idx][0], items[idx][1], idx)

    def body(t, carry):
        for idx, (j, gi) in enumerate(items):
            ahead = idx + lookahead
            ja, ga = items[ahead % len(items)]
            ca = jnp.minimum((t + ahead // len(items)) * unroll + ja, n_chunks - 1)
            scores(ca, ga, ahead % ring)

            cols = slice(gi * gw, (gi + 1) * gw)
            slot = idx % ring
            m_prev = m_sc[:, cols]
            m_new = jnp.maximum(m_prev, mx_sc[slot])
            alpha = jnp.exp2(m_prev - m_new)
            p = jnp.exp2(s_sc[slot] - m_new)
            vt = jnp.concatenate([vt_ref[t * unroll + j, i] for i in range(vt_ref.shape[1])], axis=1)
            vt = jnp.concatenate([vt, ones_rows], axis=0)
            pv = jnp.dot(vt, p.astype(BF16), preferred_element_type=F32)
            acc_sc[:, cols] = alpha * acc_sc[:, cols] + pv
            m_sc[:, cols] = m_new
        return carry

    lax.fori_loop(0, n_chunks // unroll, body, 0)
    dv = acc_sc.shape[0] - BF16_SUBLANES
    o = acc_sc[:dv, :] / acc_sc[dv:dv + 1, :]
    for gi in range(group):
        o_ref[:, gi * dv:(gi + 1) * dv] = o[:, gi * bq:(gi + 1) * bq].T.astype(o_ref.dtype)


def _attention(q, k, vt, *, group, bq, name):
    b, h, s, dqk = q.shape
    hkv = k.shape[1]
    n_chunks, slabs, dv, _ = vt.shape[2:]
    chunk = slabs * LANES
    n = group * bq
    gw = min(ATTN_COL_GROUP, bq)
    unroll = max(1, ATTN_TRIP_ITEMS // (n // gw))
    while n_chunks % unroll:
        unroll -= 1
    items = unroll * (n // gw)
    ring = ATTN_RING if items % ATTN_RING == 0 else items
    kern = functools.partial(_attn_kernel, group=group, bq=bq, chunk=chunk, n_chunks=n_chunks,
                             unroll=unroll)
    return pl.pallas_call(
        kern,
        grid=(b, hkv, s // bq),
        in_specs=[
            pl.BlockSpec((None, group, bq, dqk), lambda bi, hi, qi: (bi, hi, qi, 0)),
            pl.BlockSpec((None, None, dqk // LANES, s, LANES), lambda bi, hi, qi: (bi, hi, 0, 0, 0)),
            pl.BlockSpec((None, None, n_chunks, slabs, dv, LANES),
                         lambda bi, hi, qi: (bi, hi, 0, 0, 0, 0)),
        ],
        out_specs=pl.BlockSpec((None, bq, group * dv), lambda bi, hi, qi: (bi, qi, hi)),
        out_shape=jax.ShapeDtypeStruct((b, s, h * dv), BF16),
        scratch_shapes=[
            pltpu.VMEM((1, n), F32),
            pltpu.VMEM((dv + BF16_SUBLANES, n), F32),
            pltpu.VMEM((ring, chunk, gw), F32),
            pltpu.VMEM((ring, 1, gw), F32),
            pltpu.VMEM((dqk, n), BF16),
        ],
        compiler_params=_params(("parallel", "parallel", "arbitrary")),
        name=name,
    )(q, k, vt)


def _cast_weight_slices(wup_ref, wdn_ref, wup_o_ref, wdn_o_ref):
    wup_o_ref[...] = wup_ref[...].astype(BF16)
    wdn_o_ref[...] = wdn_ref[...].astype(BF16)


def _weight_cast_specs(mlp_w, layer, steps, step_index):
    in_specs, out_specs, out_shapes = [], [], []
    for w in mlp_w:
        rows, cols = w.shape[1:]
        assert rows % (steps * BF16_SUBLANES) == 0
        blk = rows // steps
        in_specs.append(pl.BlockSpec((None, blk, cols),
                                     lambda *g: (layer, step_index(*g), 0)))
        out_specs.append(pl.BlockSpec((blk, cols), lambda *g: (step_index(*g), 0)))
        out_shapes.append(jax.ShapeDtypeStruct((rows, cols), BF16))
    return in_specs, out_specs, out_shapes


def _out0_kernel(h_ref, a_ref, gb_ref, cin_ref, cprev_ref, cnext_ref, cw_ref, w_ref, wup_ref, wdn_ref,
                 o_ref, wup_o_ref, wdn_o_ref):
    _cast_weight_slices(wup_ref, wdn_ref, wup_o_ref, wdn_o_ref)
    i = pl.program_id(1)
    bm = cin_ref.shape[0]
    c = cin_ref[...].astype(F32)
    prev_row = cprev_ref[BF16_SUBLANES - 1:BF16_SUBLANES, :].astype(F32)
    next_row = cnext_ref[0:1, :].astype(F32)
    prev_row = jnp.where(i == 0, 0.0, prev_row)
    next_row = jnp.where(i == pl.num_programs(1) - 1, 0.0, next_row)
    rows = lax.broadcasted_iota(jnp.int32, (bm, 1), 0)
    c_m1 = jnp.where(rows == 0, prev_row, pltpu.roll(c, 1, 0))
    c_p1 = jnp.where(rows == bm - 1, next_row, pltpu.roll(c, bm - 1, 0))
    conv = cw_ref[0:1, :] * c_m1 + cw_ref[1:2, :] * c + cw_ref[2:3, :] * c_p1
    sconv = (gb_ref[...].astype(F32) * conv).astype(BF16)
    aw = a_ref.shape[1]
    y = jnp.dot(a_ref[...], w_ref[:aw, :], preferred_element_type=F32)
    y = y + jnp.dot(sconv, w_ref[aw:, :], preferred_element_type=F32)
    o_ref[...] = h_ref[...] + y


def _out0(h, attn, gb, cin, conv_w, w, mlp_w, layer, bm):
    b, s, d = h.shape
    nb = s // bm
    w_specs, w_out_specs, w_out_shapes = _weight_cast_specs(mlp_w, layer, b * nb,
                                                            lambda bi, i: bi * nb + i)
    r = bm // BF16_SUBLANES
    last = s // BF16_SUBLANES - 1
    tile = lambda width: pl.BlockSpec((None, bm, width), lambda bi, i: (bi, i, 0))
    return pl.pallas_call(
        _out0_kernel,
        grid=(b, nb),
        in_specs=[
            tile(d), tile(attn.shape[2]), tile(CONV_WIDTH), tile(CONV_WIDTH),
            pl.BlockSpec((None, BF16_SUBLANES, CONV_WIDTH),
                         lambda bi, i: (bi, jnp.maximum(i * r - 1, 0), 0)),
            pl.BlockSpec((None, BF16_SUBLANES, CONV_WIDTH),
                         lambda bi, i: (bi, jnp.minimum((i + 1) * r, last), 0)),
            pl.BlockSpec(conv_w.shape, lambda bi, i: (0, 0)),
            pl.BlockSpec(w.shape, lambda bi, i: (0, 0)),
        ] + w_specs,
        out_specs=[tile(d)] + w_out_specs,
        out_shape=[jax.ShapeDtypeStruct(h.shape, F32)] + w_out_shapes,
        compiler_params=_params(("parallel", "parallel")),
        name="out0",
    )(h, attn, gb, cin, cin, cin, conv_w, w, *mlp_w)


def _out1_kernel(h_ref, a_ref, w_ref, wup_ref, wdn_ref, o_ref, wup_o_ref, wdn_o_ref):
    _cast_weight_slices(wup_ref, wdn_ref, wup_o_ref, wdn_o_ref)
    o_ref[...] = h_ref[...] + jnp.dot(a_ref[...], w_ref[...], preferred_element_type=F32)


def _out1(h, attn, w, mlp_w, layer, bm):
    t, d = h.shape
    w_specs, w_out_specs, w_out_shapes = _weight_cast_specs(mlp_w, layer, t // bm, lambda i: i)
    return pl.pallas_call(
        _out1_kernel,
        grid=(t // bm,),
        in_specs=[
            pl.BlockSpec((bm, d), lambda i: (i, 0)),
            pl.BlockSpec((bm, attn.shape[1]), lambda i: (i, 0)),
            pl.BlockSpec(w.shape, lambda i: (0, 0)),
        ] + w_specs,
        out_specs=[pl.BlockSpec((bm, d), lambda i: (i, 0))] + w_out_specs,
        out_shape=[jax.ShapeDtypeStruct(h.shape, F32)] + w_out_shapes,
        compiler_params=_params(("parallel",)),
        name="out1",
    )(h, attn, w, *mlp_w)


def _mlp_kernel(h_ref, g_ref, wup_ref, wdn_ref, fg_ref, o_ref, xn_sc, *, final_norm):
    f = pl.program_id(1)

    def mlp_chunk(xn):
        u = jnp.dot(xn, wup_ref[...], preferred_element_type=F32)
        a = jnp.square(jnp.maximum(u, 0.0)).astype(BF16)
        return jnp.dot(a, wdn_ref[...], preferred_element_type=F32)

    @pl.when(f == 0)
    def _():
        x = h_ref[...]
        xn = _rms(x, g_ref[...]).astype(BF16)
        xn_sc[...] = xn
        o_ref[...] = x + mlp_chunk(xn)

    @pl.when(f > 0)
    def _():
        o_ref[...] += mlp_chunk(xn_sc[...])

    if final_norm:
        @pl.when(f == pl.num_programs(1) - 1)
        def _():
            o_ref[...] = _rms(o_ref[...], fg_ref[...])


def _mlp(h, g, wup, wdn, fg, *, bm, bf, final_norm):
    t, d = h.shape
    dff = wup.shape[1]
    return pl.pallas_call(
        functools.partial(_mlp_kernel, final_norm=final_norm),
        grid=(t // bm, dff // bf),
        in_specs=[
            pl.BlockSpec((bm, d), lambda i, f: (i, 0)),
            pl.BlockSpec((1, d), lambda i, f: (0, 0)),
            pl.BlockSpec((d, bf), lambda i, f: (0, f)),
            pl.BlockSpec((bf, d), lambda i, f: (f, 0)),
            pl.BlockSpec((1, d), lambda i, f: (0, 0)),
        ],
        out_specs=pl.BlockSpec((bm, d), lambda i, f: (i, 0)),
        out_shape=jax.ShapeDtypeStruct(h.shape, F32),
        scratch_shapes=[pltpu.VMEM((bm, d), BF16)],
        compiler_params=_params(("parallel", "arbitrary")),
        name="mlp_final" if final_norm else "mlp",
    )(h, g, wup, wdn, fg)


def _rope_cos_sin(seq, rot_dim):
    rows = seq // GRID_W
    axis_dim = rot_dim // 2
    inv_freq = ROPE_THETA ** (-jnp.arange(0, axis_dim, 2, dtype=F32) / axis_dim)
    ang_row = jnp.arange(rows, dtype=F32)[:, None] * inv_freq
    ang_col = jnp.arange(GRID_W, dtype=F32)[:, None] * inv_freq
    shape = (rows, GRID_W, inv_freq.shape[0])

    def per_token(fn):
        by_row = jnp.broadcast_to(fn(ang_row)[:, None, :], shape)
        by_col = jnp.broadcast_to(fn(ang_col)[None, :, :], shape)
        return jnp.concatenate([by_row, by_col], axis=-1).reshape(seq, rot_dim // 2)

    return per_token(jnp.cos), per_token(jnp.sin)


def _rope_tables(seq, rot_dim):
    cos, sin = _rope_cos_sin(seq, rot_dim)
    pad = jnp.zeros((seq, LANES // 2 - rot_dim // 2), F32)
    cos_t = jnp.concatenate([cos, pad, cos, pad], axis=-1)
    sin_t = jnp.concatenate([-sin, pad, sin, pad], axis=-1)
    return cos_t, sin_t


def _pair_split(w):
    return w[..., 0::2], w[..., 1::2]


def _pair_split_sources(rot_dim, pad):
    blank = [-1] * pad
    return list(range(0, rot_dim, 2)) + blank + list(range(1, rot_dim, 2)) + blank


_PAIR_SPLIT_128 = _pair_split_sources(HEAD_DIM, 0)
_ROPE_PAD_64 = _pair_split_sources(MLA_ROPE, LANES // 2 - MLA_ROPE // 2)
_MLA_Q_HEAD = list(range(MLA_NOPE)) + [MLA_NOPE + c if c >= 0 else -1 for c in _ROPE_PAD_64]


def _relayout(w, sources):
    sel = [[1.0 if src == k else 0.0 for src in sources] for k in range(w.shape[-1])]
    return jnp.dot(w, jnp.array(sel, BF16), preferred_element_type=BF16)


def kernel(x, even_norm_g, even_w_in, even_q_norm_g, even_k_norm_g, even_conv_w, even_w_out,
           odd_norm_g, odd_w_down, odd_q_lat_g, odd_kv_lat_g, odd_w_uq, odd_w_ukv, odd_w_o,
           mlp_norm_g, mlp_w_up, mlp_w_down, final_norm_g):
    b, s, d = x.shape
    t = _tiles(s)
    depth = mlp_norm_g.shape[0]
    cos_a, sin_a = _rope_tables(s, HEAD_DIM)
    cos_c, sin_c = _rope_tables(s, MLA_ROPE)
    fg = final_norm_g.reshape(1, d)
    mlp_w = (mlp_w_up, mlp_w_down)

    h = x
    for layer in range(depth):
        i = layer // 2
        if layer % 2 == 0:
            w_in = even_w_in[i].astype(BF16)
            wq = w_in[:, :ATTN_WIDTH].reshape(d, ATTN_HEADS, HEAD_DIM)
            wk = w_in[:, ATTN_WIDTH:ATTN_WIDTH + KV_WIDTH].reshape(d, ATTN_KV_HEADS, HEAD_DIM)
            wq = _relayout(wq, _PAIR_SPLIT_128).reshape(d, ATTN_WIDTH)
            wk = _relayout(wk, _PAIR_SPLIT_128).reshape(d, KV_WIDTH)
            w0 = jnp.concatenate([wq, wk, w_in[:, ATTN_WIDTH + KV_WIDTH:]], axis=-1)
            qg = jnp.concatenate(_pair_split(even_q_norm_g[i]), axis=-1).reshape(1, HEAD_DIM)
            kg = jnp.concatenate(_pair_split(even_k_norm_g[i]), axis=-1).reshape(1, HEAD_DIM)
            q, k, vt, gb, cin = _proj0(h, even_norm_g[i].reshape(1, d), w0, qg, kg,
                                       cos_a, sin_a, t["chunk"], t["kv_chunk"])
            attn = _attention(q, k, vt, group=ATTN_GROUP, bq=t["bq_gqa"], name="gqa_attn")
            h, w_up, w_dn = _out0(h, attn, gb, cin, even_conv_w[i], even_w_out[i].astype(BF16),
                                  mlp_w, layer, t["bm_out"])
        else:
            w_down = odd_w_down[i].astype(BF16)
            wd = jnp.concatenate([w_down[:, :Q_LORA + KV_LORA],
                                  _relayout(w_down[:, Q_LORA + KV_LORA:], _ROPE_PAD_64)], axis=-1)
            wuq = odd_w_uq[i].astype(BF16).reshape(Q_LORA, MLA_HEADS, MLA_NOPE + MLA_ROPE)
            wuq = _relayout(wuq, _MLA_Q_HEAD).reshape(Q_LORA, MLA_HEADS * MLA_QK_PAD)
            q, k, vt = _proj1(h, odd_norm_g[i].reshape(1, d), wd,
                              odd_q_lat_g[i].reshape(1, Q_LORA), odd_kv_lat_g[i].reshape(1, KV_LORA),
                              wuq, odd_w_ukv[i].astype(BF16), cos_c, sin_c, t["chunk"],
                              t["kv_chunk"])
            attn = _attention(q, k, vt, group=1, bq=t["bq_mla"], name="mla_attn")
            h, w_up, w_dn = _out1(h.reshape(b * s, d), attn.reshape(b * s, -1),
                                  odd_w_o[i].astype(BF16), mlp_w, layer, t["bm_out"])
        h = _mlp(h.reshape(b * s, d), mlp_norm_g[layer].reshape(1, d), w_up, w_dn, fg,
                 bm=t["bm_mlp"], bf=t["bf_mlp"],
                 final_norm=(layer == depth - 1)).reshape(b, s, d)
    return h
```

```python
import functools

import jax
import jax.numpy as jnp
from jax import lax
from jax.experimental import pallas as pl
from jax.experimental.pallas import tpu as pltpu

F32 = jnp.float32
BF16 = jnp.bfloat16

NORM_EPS = 1e-6
ROPE_THETA = 10000.0
GRID_W = 64

HEAD_DIM = 128
ATTN_HEADS = 8
ATTN_KV_HEADS = 2
ATTN_GROUP = ATTN_HEADS // ATTN_KV_HEADS
ATTN_WIDTH = ATTN_HEADS * HEAD_DIM
KV_WIDTH = ATTN_KV_HEADS * HEAD_DIM
CONV_WIDTH = 1024

MLA_HEADS = 16
MLA_NOPE = 128
MLA_ROPE = 64
MLA_V = 128
Q_LORA = 512
KV_LORA = 512
MLA_QK_PAD = 256

LANES = 128
BF16_SUBLANES = 16
VMEM_LIMIT_BYTES = 56 * 1024 * 1024
NEG_BIG = -1e30
LOG2_E = 1.4426950408889634
ATTN_COL_GROUP = 512
ATTN_TRIP_ITEMS = 32
ATTN_LOOKAHEAD = 3
ATTN_RING = 4


def _tiles(seq):
    return dict(
        chunk=min(512, seq),
        kv_chunk=min(512, seq),
        bm_mlp=min(1024, seq),
        bf_mlp=512,
        bm_out=min(512, seq),
        bq_gqa=min(1024, seq),
        bq_mla=min(4096, seq),
    )


def _params(sem):
    return pltpu.CompilerParams(dimension_semantics=sem, vmem_limit_bytes=VMEM_LIMIT_BYTES)


def _rms(x, g):
    return x * lax.rsqrt(jnp.mean(x * x, axis=-1, keepdims=True) + NORM_EPS) * g


def _store_vt(vt_ref, head, v):
    vt = v.T.astype(BF16)
    for i in range(vt_ref.shape[1]):
        vt_ref[head, i] = vt[:, i * LANES:(i + 1) * LANES]


def _rope(x, cos, sin):
    return x * cos + pltpu.roll(x, LANES // 2, 1) * sin


def _proj0_kernel(h_ref, g_ref, w_ref, qg_ref, kg_ref, cos_ref, sin_ref,
                  q_ref, k_ref, vt_ref, gb_ref, cin_ref):
    xn = _rms(h_ref[...], g_ref[...]).astype(BF16)
    y = jnp.dot(xn, w_ref[...], preferred_element_type=F32)
    cos = cos_ref[...]
    sin = sin_ref[...]
    scale = HEAD_DIM ** -0.5 * LOG2_E
    for hh in range(ATTN_HEADS):
        yh = _rms(y[:, hh * HEAD_DIM:(hh + 1) * HEAD_DIM], qg_ref[...])
        q_ref[hh] = (_rope(yh, cos, sin) * scale).astype(BF16)
    o = ATTN_WIDTH
    for hh in range(ATTN_KV_HEADS):
        yh = _rms(y[:, o + hh * HEAD_DIM:o + (hh + 1) * HEAD_DIM], kg_ref[...])
        k_ref[hh, 0] = _rope(yh, cos, sin).astype(BF16)
    o += KV_WIDTH
    for hh in range(ATTN_KV_HEADS):
        _store_vt(vt_ref, hh, y[:, o + hh * HEAD_DIM:o + (hh + 1) * HEAD_DIM])
    o += KV_WIDTH
    gb_ref[...] = y[:, o:o + CONV_WIDTH].astype(BF16)
    o += CONV_WIDTH
    cin_ref[...] = (y[:, o:o + CONV_WIDTH] * y[:, o + CONV_WIDTH:o + 2 * CONV_WIDTH]).astype(BF16)


def _proj0(h, g, w, qg, kg, cos, sin, chunk, kv_chunk):
    b, s, d = h.shape
    n = w.shape[1]
    nc = s // chunk
    r = kv_chunk // chunk
    return pl.pallas_call(
        _proj0_kernel,
        grid=(b, nc),
        in_specs=[
            pl.BlockSpec((None, chunk, d), lambda bi, i: (bi, i, 0)),
            pl.BlockSpec((1, d), lambda bi, i: (0, 0)),
            pl.BlockSpec((d, n), lambda bi, i: (0, 0)),
            pl.BlockSpec((1, HEAD_DIM), lambda bi, i: (0, 0)),
            pl.BlockSpec((1, HEAD_DIM), lambda bi, i: (0, 0)),
            pl.BlockSpec((chunk, HEAD_DIM), lambda bi, i: (i, 0)),
            pl.BlockSpec((chunk, HEAD_DIM), lambda bi, i: (i, 0)),
        ],
        out_specs=[
            pl.BlockSpec((None, ATTN_HEADS, chunk, HEAD_DIM), lambda bi, i: (bi, 0, i, 0)),
            pl.BlockSpec((None, ATTN_KV_HEADS, 1, chunk, LANES), lambda bi, i: (bi, 0, 0, i, 0)),
            pl.BlockSpec((None, ATTN_KV_HEADS, None, chunk // LANES, HEAD_DIM, LANES),
                         lambda bi, i: (bi, 0, i // r, i % r, 0, 0)),
            pl.BlockSpec((None, chunk, CONV_WIDTH), lambda bi, i: (bi, i, 0)),
            pl.BlockSpec((None, chunk, CONV_WIDTH), lambda bi, i: (bi, i, 0)),
        ],
        out_shape=[
            jax.ShapeDtypeStruct((b, ATTN_HEADS, s, HEAD_DIM), BF16),
            jax.ShapeDtypeStruct((b, ATTN_KV_HEADS, 1, s, LANES), BF16),
            jax.ShapeDtypeStruct((b, ATTN_KV_HEADS, s // kv_chunk, kv_chunk // LANES, HEAD_DIM, LANES),
                                 BF16),
            jax.ShapeDtypeStruct((b, s, CONV_WIDTH), BF16),
            jax.ShapeDtypeStruct((b, s, CONV_WIDTH), BF16),
        ],
        compiler_params=_params(("parallel", "parallel")),
        name="proj0",
    )(h, g, w, qg, kg, cos, sin)


def _proj1_kernel(h_ref, g_ref, wd_ref, qg_ref, kvg_ref, wuq_ref, wukv_ref, cos_ref, sin_ref,
                  q_ref, k_ref, vt_ref):
    xn = _rms(h_ref[...], g_ref[...]).astype(BF16)
    lat = jnp.dot(xn, wd_ref[...], preferred_element_type=F32)
    cq = _rms(lat[:, :Q_LORA], qg_ref[...]).astype(BF16)
    ckv = _rms(lat[:, Q_LORA:Q_LORA + KV_LORA], kvg_ref[...]).astype(BF16)
    cos = cos_ref[...]
    sin = sin_ref[...]
    kr = _rope(lat[:, Q_LORA + KV_LORA:], cos, sin).astype(BF16)
    q = jnp.dot(cq, wuq_ref[...], preferred_element_type=F32)
    kv = jnp.dot(ckv, wukv_ref[...], preferred_element_type=F32)
    scale = (MLA_NOPE + MLA_ROPE) ** -0.5 * LOG2_E
    for hh in range(MLA_HEADS):
        o = hh * MLA_QK_PAD
        q_ref[hh, :, :MLA_NOPE] = (q[:, o:o + MLA_NOPE] * scale).astype(BF16)
        qr = _rope(q[:, o + MLA_NOPE:o + MLA_QK_PAD], cos, sin)
        q_ref[hh, :, MLA_NOPE:] = (qr * scale).astype(BF16)
        o = hh * (MLA_NOPE + MLA_V)
        k_ref[hh, 0] = kv[:, o:o + MLA_NOPE].astype(BF16)
        k_ref[hh, 1] = kr
        _store_vt(vt_ref, hh, kv[:, o + MLA_NOPE:o + MLA_NOPE + MLA_V])


def _proj1(h, g, wd, qg, kvg, wuq, wukv, cos, sin, chunk, kv_chunk):
    b, s, d = h.shape
    nc = s // chunk
    r = kv_chunk // chunk
    const = lambda bi, i: (0, 0)
    return pl.pallas_call(
        _proj1_kernel,
        grid=(b, nc),
        in_specs=[
            pl.BlockSpec((None, chunk, d), lambda bi, i: (bi, i, 0)),
            pl.BlockSpec((1, d), const),
            pl.BlockSpec(wd.shape, const),
            pl.BlockSpec((1, Q_LORA), const),
            pl.BlockSpec((1, KV_LORA), const),
            pl.BlockSpec(wuq.shape, const),
            pl.BlockSpec(wukv.shape, const),
            pl.BlockSpec((chunk, LANES), lambda bi, i: (i, 0)),
            pl.BlockSpec((chunk, LANES), lambda bi, i: (i, 0)),
        ],
        out_specs=[
            pl.BlockSpec((None, MLA_HEADS, chunk, MLA_QK_PAD), lambda bi, i: (bi, 0, i, 0)),
            pl.BlockSpec((None, MLA_HEADS, MLA_QK_PAD // LANES, chunk, LANES),
                         lambda bi, i: (bi, 0, 0, i, 0)),
            pl.BlockSpec((None, MLA_HEADS, None, chunk // LANES, MLA_V, LANES),
                         lambda bi, i: (bi, 0, i // r, i % r, 0, 0)),
        ],
        out_shape=[
            jax.ShapeDtypeStruct((b, MLA_HEADS, s, MLA_QK_PAD), BF16),
            jax.ShapeDtypeStruct((b, MLA_HEADS, MLA_QK_PAD // LANES, s, LANES), BF16),
            jax.ShapeDtypeStruct((b, MLA_HEADS, s // kv_chunk, kv_chunk // LANES, MLA_V, LANES), BF16),
        ],
        compiler_params=_params(("parallel", "parallel")),
        name="proj1",
    )(h, g, wd, qg, kvg, wuq, wukv, cos, sin)


def _attn_kernel(q_ref, k_ref, vt_ref, o_ref, m_sc, acc_sc, s_sc, mx_sc, qt_sc, *, group, bq,
                 chunk, n_chunks, unroll):
    n = group * bq
    gw = s_sc.shape[2]
    n_groups = n // gw
    ring = s_sc.shape[0]
    m_sc[...] = jnp.full(m_sc.shape, NEG_BIG, F32)
    acc_sc[...] = jnp.zeros(acc_sc.shape, F32)
    for gi in range(n_groups):
        qg = q_ref[(gi * gw) // bq, pl.ds((gi * gw) % bq, gw), :]
        qt_sc[:, gi * gw:(gi + 1) * gw] = qg.T

    ones_rows = (lax.broadcasted_iota(jnp.int32, (BF16_SUBLANES, chunk), 0) == 0).astype(BF16)

    items = [(j, gi) for j in range(unroll) for gi in range(n_groups)]
    assert len(items) % ring == 0
    lookahead = min(ATTN_LOOKAHEAD, ring - 1)

    def scores(c, gi, slot):
        rows = pl.ds(pl.multiple_of(c * chunk, chunk), chunk)
        kc = jnp.concatenate([k_ref[i, rows, :] for i in range(k_ref.shape[0])], axis=1)
        s = jnp.dot(kc, qt_sc[:, gi * gw:(gi + 1) * gw], preferred_element_type=F32)
        s_sc[slot] = s
        mx_sc[slot] = jnp.max(s, axis=0, keepdims=True)

    for idx in range(lookahead):
        scores(items[idx][0], items[idx][1], idx)

    def body(t, carry):
        for idx, (j, gi) in enumerate(items):
            ahead = idx + lookahead
            ja, ga = items[ahead % len(items)]
            ca = jnp.minimum((t + ahead // len(items)) * unroll + ja, n_chunks - 1)
            scores(ca, ga, ahead % ring)

            cols = slice(gi * gw, (gi + 1) * gw)
            slot = idx % ring
            m_prev = m_sc[:, cols]
            m_new = jnp.maximum(m_prev, mx_sc[slot])
            alpha = jnp.exp2(m_prev - m_new)
            p = jnp.exp2(s_sc[slot] - m_new)
            vt = jnp.concatenate([vt_ref[t * unroll + j, i] for i in range(vt_ref.shape[1])], axis=1)
            vt = jnp.concatenate([vt, ones_rows], axis=0)
            pv = jnp.dot(vt, p.astype(BF16), preferred_element_type=F32)
            acc_sc[:, cols] = alpha * acc_sc[:, cols] + pv
            m_sc[:, cols] = m_new
        return carry

    lax.fori_loop(0, n_chunks // unroll, body, 0)
    dv = acc_sc.shape[0] - BF16_SUBLANES
    o = acc_sc[:dv, :] / acc_sc[dv:dv + 1, :]
    for gi in range(group):
        o_ref[:, gi * dv:(gi + 1) * dv] = o[:, gi * bq:(gi + 1) * bq].T.astype(o_ref.dtype)


def _attention(q, k, vt, *, group, bq, name):
    b, h, s, dqk = q.shape
    hkv = k.shape[1]
    n_chunks, slabs, dv, _ = vt.shape[2:]
    chunk = slabs * LANES
    n = group * bq
    gw = min(ATTN_COL_GROUP, bq)
    unroll = max(1, ATTN_TRIP_ITEMS // (n // gw))
    while n_chunks % unroll:
        unroll -= 1
    items = unroll * (n // gw)
    ring = ATTN_RING if items % ATTN_RING == 0 else items
    kern = functools.partial(_attn_kernel, group=group, bq=bq, chunk=chunk, n_chunks=n_chunks,
                             unroll=unroll)
    return pl.pallas_call(
        kern,
        grid=(b, hkv, s // bq),
        in_specs=[
            pl.BlockSpec((None, group, bq, dqk), lambda bi, hi, qi: (bi, hi, qi, 0)),
            pl.BlockSpec((None, None, dqk // LANES, s, LANES), lambda bi, hi, qi: (bi, hi, 0, 0, 0)),
            pl.BlockSpec((None, None, n_chunks, slabs, dv, LANES),
                         lambda bi, hi, qi: (bi, hi, 0, 0, 0, 0)),
        ],
        out_specs=pl.BlockSpec((None, bq, group * dv), lambda bi, hi, qi: (bi, qi, hi)),
        out_shape=jax.ShapeDtypeStruct((b, s, h * dv), BF16),
        scratch_shapes=[
            pltpu.VMEM((1, n), F32),
            pltpu.VMEM((dv + BF16_SUBLANES, n), F32),
            pltpu.VMEM((ring, chunk, gw), F32),
            pltpu.VMEM((ring, 1, gw), F32),
            pltpu.VMEM((dqk, n), BF16),
        ],
        compiler_params=_params(("parallel", "parallel", "arbitrary")),
        name=name,
    )(q, k, vt)


def _cast_weight_slices(wup_ref, wdn_ref, wup_o_ref, wdn_o_ref):
    wup_o_ref[...] = wup_ref[...].astype(BF16)
    wdn_o_ref[...] = wdn_ref[...].astype(BF16)


def _weight_cast_specs(mlp_w, layer, steps, step_index):
    in_specs, out_specs, out_shapes = [], [], []
    for w in mlp_w:
        rows, cols = w.shape[1:]
        assert rows % (steps * BF16_SUBLANES) == 0
        blk = rows // steps
        in_specs.append(pl.BlockSpec((None, blk, cols),
                                     lambda *g: (layer, step_index(*g), 0)))
        out_specs.append(pl.BlockSpec((blk, cols), lambda *g: (step_index(*g), 0)))
        out_shapes.append(jax.ShapeDtypeStruct((rows, cols), BF16))
    return in_specs, out_specs, out_shapes


def _out0_kernel(h_ref, a_ref, gb_ref, cin_ref, cprev_ref, cnext_ref, cw_ref, w_ref, wup_ref, wdn_ref,
                 o_ref, wup_o_ref, wdn_o_ref):
    _cast_weight_slices(wup_ref, wdn_ref, wup_o_ref, wdn_o_ref)
    i = pl.program_id(1)
    bm = cin_ref.shape[0]
    c = cin_ref[...].astype(F32)
    prev_row = cprev_ref[BF16_SUBLANES - 1:BF16_SUBLANES, :].astype(F32)
    next_row = cnext_ref[0:1, :].astype(F32)
    prev_row = jnp.where(i == 0, 0.0, prev_row)
    next_row = jnp.where(i == pl.num_programs(1) - 1, 0.0, next_row)
    rows = lax.broadcasted_iota(jnp.int32, (bm, 1), 0)
    c_m1 = jnp.where(rows == 0, prev_row, pltpu.roll(c, 1, 0))
    c_p1 = jnp.where(rows == bm - 1, next_row, pltpu.roll(c, bm - 1, 0))
    conv = cw_ref[0:1, :] * c_m1 + cw_ref[1:2, :] * c + cw_ref[2:3, :] * c_p1
    sconv = (gb_ref[...].astype(F32) * conv).astype(BF16)
    aw = a_ref.shape[1]
    y = jnp.dot(a_ref[...], w_ref[:aw, :], preferred_element_type=F32)
    y = y + jnp.dot(sconv, w_ref[aw:, :], preferred_element_type=F32)
    o_ref[...] = h_ref[...] + y


def _out0(h, attn, gb, cin, conv_w, w, mlp_w, layer, bm):
    b, s, d = h.shape
    nb = s // bm
    w_specs, w_out_specs, w_out_shapes = _weight_cast_specs(mlp_w, layer, b * nb,
                                                            lambda bi, i: bi * nb + i)
    r = bm // BF16_SUBLANES
    last = s // BF16_SUBLANES - 1
    tile = lambda width: pl.BlockSpec((None, bm, width), lambda bi, i: (bi, i, 0))
    return pl.pallas_call(
        _out0_kernel,
        grid=(b, nb),
        in_specs=[
            tile(d), tile(attn.shape[2]), tile(CONV_WIDTH), tile(CONV_WIDTH),
            pl.BlockSpec((None, BF16_SUBLANES, CONV_WIDTH),
                         lambda bi, i: (bi, jnp.maximum(i * r - 1, 0), 0)),
            pl.BlockSpec((None, BF16_SUBLANES, CONV_WIDTH),
                         lambda bi, i: (bi, jnp.minimum((i + 1) * r, last), 0)),
            pl.BlockSpec(conv_w.shape, lambda bi, i: (0, 0)),
            pl.BlockSpec(w.shape, lambda bi, i: (0, 0)),
        ] + w_specs,
        out_specs=[tile(d)] + w_out_specs,
        out_shape=[jax.ShapeDtypeStruct(h.shape, F32)] + w_out_shapes,
        compiler_params=_params(("parallel", "parallel")),
        name="out0",
    )(h, attn, gb, cin, cin, cin, conv_w, w, *mlp_w)


def _out1_kernel(h_ref, a_ref, w_ref, wup_ref, wdn_ref, o_ref, wup_o_ref, wdn_o_ref):
    _cast_weight_slices(wup_ref, wdn_ref, wup_o_ref, wdn_o_ref)
    o_ref[...] = h_ref[...] + jnp.dot(a_ref[...], w_ref[...], preferred_element_type=F32)


def _out1(h, attn, w, mlp_w, layer, bm):
    t, d = h.shape
    w_specs, w_out_specs, w_out_shapes = _weight_cast_specs(mlp_w, layer, t // bm, lambda i: i)
    return pl.pallas_call(
        _out1_kernel,
        grid=(t // bm,),
        in_specs=[
            pl.BlockSpec((bm, d), lambda i: (i, 0)),
            pl.BlockSpec((bm, attn.shape[1]), lambda i: (i, 0)),
            pl.BlockSpec(w.shape, lambda i: (0, 0)),
        ] + w_specs,
        out_specs=[pl.BlockSpec((bm, d), lambda i: (i, 0))] + w_out_specs,
        out_shape=[jax.ShapeDtypeStruct(h.shape, F32)] + w_out_shapes,
        compiler_params=_params(("parallel",)),
        name="out1",
    )(h, attn, w, *mlp_w)


def _mlp_kernel(h_ref, g_ref, wup_ref, wdn_ref, fg_ref, o_ref, xn_sc, *, final_norm):
    f = pl.program_id(1)

    def mlp_chunk(xn):
        u = jnp.dot(xn, wup_ref[...], preferred_element_type=F32)
        a = jnp.square(jnp.maximum(u, 0.0)).astype(BF16)
        return jnp.dot(a, wdn_ref[...], preferred_element_type=F32)

    @pl.when(f == 0)
    def _():
        x = h_ref[...]
        xn = _rms(x, g_ref[...]).astype(BF16)
        xn_sc[...] = xn
        o_ref[...] = x + mlp_chunk(xn)

    @pl.when(f > 0)
    def _():
        o_ref[...] += mlp_chunk(xn_sc[...])

    if final_norm:
        @pl.when(f == pl.num_programs(1) - 1)
        def _():
            o_ref[...] = _rms(o_ref[...], fg_ref[...])


def _mlp(h, g, wup, wdn, fg, *, bm, bf, final_norm):
    t, d = h.shape
    dff = wup.shape[1]
    return pl.pallas_call(
        functools.partial(_mlp_kernel, final_norm=final_norm),
        grid=(t // bm, dff // bf),
        in_specs=[
            pl.BlockSpec((bm, d), lambda i, f: (i, 0)),
            pl.BlockSpec((1, d), lambda i, f: (0, 0)),
            pl.BlockSpec((d, bf), lambda i, f: (0, f)),
            pl.BlockSpec((bf, d), lambda i, f: (f, 0)),
            pl.BlockSpec((1, d), lambda i, f: (0, 0)),
        ],
        out_specs=pl.BlockSpec((bm, d), lambda i, f: (i, 0)),
        out_shape=jax.ShapeDtypeStruct(h.shape, F32),
        scratch_shapes=[pltpu.VMEM((bm, d), BF16)],
        compiler_params=_params(("parallel", "arbitrary")),
        name="mlp_final" if final_norm else "mlp",
    )(h, g, wup, wdn, fg)


def _rope_cos_sin(seq, rot_dim):
    rows = seq // GRID_W
    axis_dim = rot_dim // 2
    inv_freq = ROPE_THETA ** (-jnp.arange(0, axis_dim, 2, dtype=F32) / axis_dim)
    ang_row = jnp.arange(rows, dtype=F32)[:, None] * inv_freq
    ang_col = jnp.arange(GRID_W, dtype=F32)[:, None] * inv_freq
    shape = (rows, GRID_W, inv_freq.shape[0])

    def per_token(fn):
        by_row = jnp.broadcast_to(fn(ang_row)[:, None, :], shape)
        by_col = jnp.broadcast_to(fn(ang_col)[None, :, :], shape)
        return jnp.concatenate([by_row, by_col], axis=-1).reshape(seq, rot_dim // 2)

    return per_token(jnp.cos), per_token(jnp.sin)


def _rope_tables(seq, rot_dim):
    cos, sin = _rope_cos_sin(seq, rot_dim)
    pad = jnp.zeros((seq, LANES // 2 - rot_dim // 2), F32)
    cos_t = jnp.concatenate([cos, pad, cos, pad], axis=-1)
    sin_t = jnp.concatenate([-sin, pad, sin, pad], axis=-1)
    return cos_t, sin_t


def _pair_split(w):
    return w[..., 0::2], w[..., 1::2]


def _pair_split_sources(rot_dim, pad):
    blank = [-1] * pad
    return list(range(0, rot_dim, 2)) + blank + list(range(1, rot_dim, 2)) + blank


_PAIR_SPLIT_128 = _pair_split_sources(HEAD_DIM, 0)
_ROPE_PAD_64 = _pair_split_sources(MLA_ROPE, LANES // 2 - MLA_ROPE // 2)
_MLA_Q_HEAD = list(range(MLA_NOPE)) + [MLA_NOPE + c if c >= 0 else -1 for c in _ROPE_PAD_64]


def _relayout(w, sources):
    sel = [[1.0 if src == k else 0.0 for src in sources] for k in range(w.shape[-1])]
    return jnp.dot(w, jnp.array(sel, BF16), preferred_element_type=BF16)


def kernel(x, even_norm_g, even_w_in, even_q_norm_g, even_k_norm_g, even_conv_w, even_w_out,
           odd_norm_g, odd_w_down, odd_q_lat_g, odd_kv_lat_g, odd_w_uq, odd_w_ukv, odd_w_o,
           mlp_norm_g, mlp_w_up, mlp_w_down, final_norm_g):
    b, s, d = x.shape
    t = _tiles(s)
    depth = mlp_norm_g.shape[0]
    cos_a, sin_a = _rope_tables(s, HEAD_DIM)
    cos_c, sin_c = _rope_tables(s, MLA_ROPE)
    fg = final_norm_g.reshape(1, d)
    mlp_w = (mlp_w_up, mlp_w_down)

    h = x
    for layer in range(depth):
        i = layer // 2
        if layer % 2 == 0:
            w_in = even_w_in[i].astype(BF16)
            wq = w_in[:, :ATTN_WIDTH].reshape(d, ATTN_HEADS, HEAD_DIM)
            wk = w_in[:, ATTN_WIDTH:ATTN_WIDTH + KV_WIDTH].reshape(d, ATTN_KV_HEADS, HEAD_DIM)
            wq = _relayout(wq, _PAIR_SPLIT_128).reshape(d, ATTN_WIDTH)
            wk = _relayout(wk, _PAIR_SPLIT_128).reshape(d, KV_WIDTH)
            w0 = jnp.concatenate([wq, wk, w_in[:, ATTN_WIDTH + KV_WIDTH:]], axis=-1)
            qg = jnp.concatenate(_pair_split(even_q_norm_g[i]), axis=-1).reshape(1, HEAD_DIM)
            kg = jnp.concatenate(_pair_split(even_k_norm_g[i]), axis=-1).reshape(1, HEAD_DIM)
            q, k, vt, gb, cin = _proj0(h, even_norm_g[i].reshape(1, d), w0, qg, kg,
                                       cos_a, sin_a, t["chunk"], t["kv_chunk"])
            attn = _attention(q, k, vt, group=ATTN_GROUP, bq=t["bq_gqa"], name="gqa_attn")
            h, w_up, w_dn = _out0(h, attn, gb, cin, even_conv_w[i], even_w_out[i].astype(BF16),
                                  mlp_w, layer, t["bm_out"])
        else:
            w_down = odd_w_down[i].astype(BF16)
            wd = jnp.concatenate([w_down[:, :Q_LORA + KV_LORA],
                                  _relayout(w_down[:, Q_LORA + KV_LORA:], _ROPE_PAD_64)], axis=-1)
            wuq = odd_w_uq[i].astype(BF16).reshape(Q_LORA, MLA_HEADS, MLA_NOPE + MLA_ROPE)
            wuq = _relayout(wuq, _MLA_Q_HEAD).reshape(Q_LORA, MLA_HEADS * MLA_QK_PAD)
            q, k, vt = _proj1(h, odd_norm_g[i].reshape(1, d), wd,
                              odd_q_lat_g[i].reshape(1, Q_LORA), odd_kv_lat_g[i].reshape(1, KV_LORA),
                              wuq, odd_w_ukv[i].astype(BF16), cos_c, sin_c, t["chunk"],
                              t["kv_chunk"])
            attn = _attention(q, k, vt, group=1, bq=t["bq_mla"], name="mla_attn")
            h, w_up, w_dn = _out1(h.reshape(b * s, d), attn.reshape(b * s, -1),
                                  odd_w_o[i].astype(BF16), mlp_w, layer, t["bm_out"])
        h = _mlp(h.reshape(b * s, d), mlp_norm_g[layer].reshape(1, d), w_up, w_dn, fg,
                 bm=t["bm_mlp"], bf=t["bf_mlp"],
                 final_norm=(layer == depth - 1)).reshape(b, s, d)
    return h
```

```python
import functools

import jax
import jax.numpy as jnp
from jax import lax
from jax.experimental import pallas as pl
from jax.experimental.pallas import tpu as pltpu

F32 = jnp.float32
BF16 = jnp.bfloat16

NORM_EPS = 1e-6
ROPE_THETA = 10000.0
GRID_W = 64

HEAD_DIM = 128
ATTN_HEADS = 8
ATTN_KV_HEADS = 2
ATTN_GROUP = ATTN_HEADS // ATTN_KV_HEADS
ATTN_WIDTH = ATTN_HEADS * HEAD_DIM
KV_WIDTH = ATTN_KV_HEADS * HEAD_DIM
CONV_WIDTH = 1024

MLA_HEADS = 16
MLA_NOPE = 128
MLA_ROPE = 64
MLA_V = 128
Q_LORA = 512
KV_LORA = 512
MLA_QK_PAD = 256

LANES = 128
BF16_SUBLANES = 16
VMEM_LIMIT_BYTES = 56 * 1024 * 1024
NEG_BIG = -1e30
LOG2_E = 1.4426950408889634
ATTN_COL_GROUP = 512
ATTN_TRIP_ITEMS = 64
ATTN_LOOKAHEAD = 2
ATTN_RING = 4


def _tiles(seq):
    return dict(
        chunk=min(512, seq),
        kv_chunk=min(512, seq),
        bm_mlp=min(1024, seq),
        bf_mlp=512,
        bm_out=min(512, seq),
        bq_gqa=min(1024, seq),
        bq_mla=min(4096, seq),
    )


def _params(sem):
    return pltpu.CompilerParams(dimension_semantics=sem, vmem_limit_bytes=VMEM_LIMIT_BYTES)


def _rms(x, g):
    return x * lax.rsqrt(jnp.mean(x * x, axis=-1, keepdims=True) + NORM_EPS) * g


def _store_vt(vt_ref, head, v):
    vt = v.T.astype(BF16)
    for i in range(vt_ref.shape[1]):
        vt_ref[head, i] = vt[:, i * LANES:(i + 1) * LANES]


def _rope(x, cos, sin):
    return x * cos + pltpu.roll(x, LANES // 2, 1) * sin


def _proj0_kernel(h_ref, g_ref, w_ref, qg_ref, kg_ref, cos_ref, sin_ref,
                  q_ref, k_ref, vt_ref, gb_ref, cin_ref):
    xn = _rms(h_ref[...], g_ref[...]).astype(BF16)
    y = jnp.dot(xn, w_ref[...], preferred_element_type=F32)
    cos = cos_ref[...]
    sin = sin_ref[...]
    scale = HEAD_DIM ** -0.5 * LOG2_E
    for hh in range(ATTN_HEADS):
        yh = _rms(y[:, hh * HEAD_DIM:(hh + 1) * HEAD_DIM], qg_ref[...])
        q_ref[hh] = (_rope(yh, cos, sin) * scale).astype(BF16)
    o = ATTN_WIDTH
    for hh in range(ATTN_KV_HEADS):
        yh = _rms(y[:, o + hh * HEAD_DIM:o + (hh + 1) * HEAD_DIM], kg_ref[...])
        k_ref[hh, 0] = _rope(yh, cos, sin).astype(BF16)
    o += KV_WIDTH
    for hh in range(ATTN_KV_HEADS):
        _store_vt(vt_ref, hh, y[:, o + hh * HEAD_DIM:o + (hh + 1) * HEAD_DIM])
    o += KV_WIDTH
    gb_ref[...] = y[:, o:o + CONV_WIDTH].astype(BF16)
    o += CONV_WIDTH
    cin_ref[...] = (y[:, o:o + CONV_WIDTH] * y[:, o + CONV_WIDTH:o + 2 * CONV_WIDTH]).astype(BF16)


def _proj0(h, g, w, qg, kg, cos, sin, chunk, kv_chunk):
    b, s, d = h.shape
    n = w.shape[1]
    nc = s // chunk
    r = kv_chunk // chunk
    return pl.pallas_call(
        _proj0_kernel,
        grid=(b, nc),
        in_specs=[
            pl.BlockSpec((None, chunk, d), lambda bi, i: (bi, i, 0)),
            pl.BlockSpec((1, d), lambda bi, i: (0, 0)),
            pl.BlockSpec((d, n), lambda bi, i: (0, 0)),
            pl.BlockSpec((1, HEAD_DIM), lambda bi, i: (0, 0)),
            pl.BlockSpec((1, HEAD_DIM), lambda bi, i: (0, 0)),
            pl.BlockSpec((chunk, HEAD_DIM), lambda bi, i: (i, 0)),
            pl.BlockSpec((chunk, HEAD_DIM), lambda bi, i: (i, 0)),
        ],
        out_specs=[
            pl.BlockSpec((None, ATTN_HEADS, chunk, HEAD_DIM), lambda bi, i: (bi, 0, i, 0)),
            pl.BlockSpec((None, ATTN_KV_HEADS, 1, chunk, LANES), lambda bi, i: (bi, 0, 0, i, 0)),
            pl.BlockSpec((None, ATTN_KV_HEADS, None, chunk // LANES, HEAD_DIM, LANES),
                         lambda bi, i: (bi, 0, i // r, i % r, 0, 0)),
            pl.BlockSpec((None, chunk, CONV_WIDTH), lambda bi, i: (bi, i, 0)),
            pl.BlockSpec((None, chunk, CONV_WIDTH), lambda bi, i: (bi, i, 0)),
        ],
        out_shape=[
            jax.ShapeDtypeStruct((b, ATTN_HEADS, s, HEAD_DIM), BF16),
            jax.ShapeDtypeStruct((b, ATTN_KV_HEADS, 1, s, LANES), BF16),
            jax.ShapeDtypeStruct((b, ATTN_KV_HEADS, s // kv_chunk, kv_chunk // LANES, HEAD_DIM, LANES),
                                 BF16),
            jax.ShapeDtypeStruct((b, s, CONV_WIDTH), BF16),
            jax.ShapeDtypeStruct((b, s, CONV_WIDTH), BF16),
        ],
        compiler_params=_params(("parallel", "parallel")),
        name="proj0",
    )(h, g, w, qg, kg, cos, sin)


def _proj1_kernel(h_ref, g_ref, wd_ref, qg_ref, kvg_ref, wuq_ref, wukv_ref, cos_ref, sin_ref,
                  q_ref, k_ref, vt_ref):
    xn = _rms(h_ref[...], g_ref[...]).astype(BF16)
    lat = jnp.dot(xn, wd_ref[...], preferred_element_type=F32)
    cq = _rms(lat[:, :Q_LORA], qg_ref[...]).astype(BF16)
    ckv = _rms(lat[:, Q_LORA:Q_LORA + KV_LORA], kvg_ref[...]).astype(BF16)
    cos = cos_ref[...]
    sin = sin_ref[...]
    kr = _rope(lat[:, Q_LORA + KV_LORA:], cos, sin).astype(BF16)
    q = jnp.dot(cq, wuq_ref[...], preferred_element_type=F32)
    kv = jnp.dot(ckv, wukv_ref[...], preferred_element_type=F32)
    scale = (MLA_NOPE + MLA_ROPE) ** -0.5 * LOG2_E
    for hh in range(MLA_HEADS):
        o = hh * MLA_QK_PAD
        q_ref[hh, :, :MLA_NOPE] = (q[:, o:o + MLA_NOPE] * scale).astype(BF16)
        qr = _rope(q[:, o + MLA_NOPE:o + MLA_QK_PAD], cos, sin)
        q_ref[hh, :, MLA_NOPE:] = (qr * scale).astype(BF16)
        o = hh * (MLA_NOPE + MLA_V)
        k_ref[hh, 0] = kv[:, o:o + MLA_NOPE].astype(BF16)
        k_ref[hh, 1] = kr
        _store_vt(vt_ref, hh, kv[:, o + MLA_NOPE:o + MLA_NOPE + MLA_V])


def _proj1(h, g, wd, qg, kvg, wuq, wukv, cos, sin, chunk, kv_chunk):
    b, s, d = h.shape
    nc = s // chunk
    r = kv_chunk // chunk
    const = lambda bi, i: (0, 0)
    return pl.pallas_call(
        _proj1_kernel,
        grid=(b, nc),
        in_specs=[
            pl.BlockSpec((None, chunk, d), lambda bi, i: (bi, i, 0)),
            pl.BlockSpec((1, d), const),
            pl.BlockSpec(wd.shape, const),
            pl.BlockSpec((1, Q_LORA), const),
            pl.BlockSpec((1, KV_LORA), const),
            pl.BlockSpec(wuq.shape, const),
            pl.BlockSpec(wukv.shape, const),
            pl.BlockSpec((chunk, LANES), lambda bi, i: (i, 0)),
            pl.BlockSpec((chunk, LANES), lambda bi, i: (i, 0)),
        ],
        out_specs=[
            pl.BlockSpec((None, MLA_HEADS, chunk, MLA_QK_PAD), lambda bi, i: (bi, 0, i, 0)),
            pl.BlockSpec((None, MLA_HEADS, MLA_QK_PAD // LANES, chunk, LANES),
                         lambda bi, i: (bi, 0, 0, i, 0)),
            pl.BlockSpec((None, MLA_HEADS, None, chunk // LANES, MLA_V, LANES),
                         lambda bi, i: (bi, 0, i // r, i % r, 0, 0)),
        ],
        out_shape=[
            jax.ShapeDtypeStruct((b, MLA_HEADS, s, MLA_QK_PAD), BF16),
            jax.ShapeDtypeStruct((b, MLA_HEADS, MLA_QK_PAD // LANES, s, LANES), BF16),
            jax.ShapeDtypeStruct((b, MLA_HEADS, s // kv_chunk, kv_chunk // LANES, MLA_V, LANES), BF16),
        ],
        compiler_params=_params(("parallel", "parallel")),
        name="proj1",
    )(h, g, wd, qg, kvg, wuq, wukv, cos, sin)


def _attn_kernel(q_ref, k_ref, vt_ref, o_ref, m_sc, acc_sc, s_sc, mx_sc, qt_sc, *, group, bq,
                 chunk, n_chunks, unroll):
    n = group * bq
    gw = s_sc.shape[2]
    n_groups = n // gw
    ring = s_sc.shape[0]
    m_sc[...] = jnp.full(m_sc.shape, NEG_BIG, F32)
    acc_sc[...] = jnp.zeros(acc_sc.shape, F32)
    for gi in range(n_groups):
        qg = q_ref[(gi * gw) // bq, pl.ds((gi * gw) % bq, gw), :]
        qt_sc[:, gi * gw:(gi + 1) * gw] = qg.T

    ones_rows = (lax.broadcasted_iota(jnp.int32, (BF16_SUBLANES, chunk), 0) == 0).astype(BF16)

    items = [(j, gi) for j in range(unroll) for gi in range(n_groups)]
    assert len(items) % ring == 0
    lookahead = min(ATTN_LOOKAHEAD, ring - 1)

    def scores(c, gi, slot):
        rows = pl.ds(pl.multiple_of(c * chunk, chunk), chunk)
        kc = jnp.concatenate([k_ref[i, rows, :] for i in range(k_ref.shape[0])], axis=1)
        s = jnp.dot(kc, qt_sc[:, gi * gw:(gi + 1) * gw], preferred_element_type=F32)
        s_sc[slot] = s
        mx_sc[slot] = jnp.max(s, axis=0, keepdims=True)

    for idx in range(lookahead):
        scores(items[idx][0], items[idx][1], idx)

    def body(t, carry):
        for idx, (j, gi) in enumerate(items):
            ahead = idx + lookahead
            ja, ga = items[ahead % len(items)]
            ca = jnp.minimum((t + ahead // len(items)) * unroll + ja, n_chunks - 1)
            scores(ca, ga, ahead % ring)

            cols = slice(gi * gw, (gi + 1) * gw)
            slot = idx % ring
            m_prev = m_sc[:, cols]
            m_new = jnp.maximum(m_prev, mx_sc[slot])
            alpha = jnp.exp2(m_prev - m_new)
            p = jnp.exp2(s_sc[slot] - m_new)
            vt = jnp.concatenate([vt_ref[t * unroll + j, i] for i in range(vt_ref.shape[1])], axis=1)
            vt = jnp.concatenate([vt, ones_rows], axis=0)
            pv = jnp.dot(vt, p.astype(BF16), preferred_element_type=F32)
            acc_sc[:, cols] = alpha * acc_sc[:, cols] + pv
            m_sc[:, cols] = m_new
        return carry

    lax.fori_loop(0, n_chunks // unroll, body, 0)
    dv = acc_sc.shape[0] - BF16_SUBLANES
    o = acc_sc[:dv, :] / acc_sc[dv:dv + 1, :]
    for gi in range(group):
        o_ref[:, gi * dv:(gi + 1) * dv] = o[:, gi * bq:(gi + 1) * bq].T.astype(o_ref.dtype)


def _attention(q, k, vt, *, group, bq, name):
    b, h, s, dqk = q.shape
    hkv = k.shape[1]
    n_chunks, slabs, dv, _ = vt.shape[2:]
    chunk = slabs * LANES
    n = group * bq
    gw = min(ATTN_COL_GROUP, bq)
    unroll = max(1, ATTN_TRIP_ITEMS // (n // gw))
    while n_chunks % unroll:
        unroll -= 1
    items = unroll * (n // gw)
    ring = ATTN_RING if items % ATTN_RING == 0 else items
    kern = functools.partial(_attn_kernel, group=group, bq=bq, chunk=chunk, n_chunks=n_chunks,
                             unroll=unroll)
    return pl.pallas_call(
        kern,
        grid=(b, hkv, s // bq),
        in_specs=[
            pl.BlockSpec((None, group, bq, dqk), lambda bi, hi, qi: (bi, hi, qi, 0)),
            pl.BlockSpec((None, None, dqk // LANES, s, LANES), lambda bi, hi, qi: (bi, hi, 0, 0, 0)),
            pl.BlockSpec((None, None, n_chunks, slabs, dv, LANES),
                         lambda bi, hi, qi: (bi, hi, 0, 0, 0, 0)),
        ],
        out_specs=pl.BlockSpec((None, bq, group * dv), lambda bi, hi, qi: (bi, qi, hi)),
        out_shape=jax.ShapeDtypeStruct((b, s, h * dv), BF16),
        scratch_shapes=[
            pltpu.VMEM((1, n), F32),
            pltpu.VMEM((dv + BF16_SUBLANES, n), F32),
            pltpu.VMEM((ring, chunk, gw), F32),
            pltpu.VMEM((ring, 1, gw), F32),
            pltpu.VMEM((dqk, n), BF16),
        ],
        compiler_params=_params(("parallel", "parallel", "arbitrary")),
        name=name,
    )(q, k, vt)


def _cast_weight_slices(wup_ref, wdn_ref, wup_o_ref, wdn_o_ref):
    wup_o_ref[...] = wup_ref[...].astype(BF16)
    wdn_o_ref[...] = wdn_ref[...].astype(BF16)


def _weight_cast_specs(mlp_w, layer, steps, step_index):
    in_specs, out_specs, out_shapes = [], [], []
    for w in mlp_w:
        rows, cols = w.shape[1:]
        assert rows % (steps * BF16_SUBLANES) == 0
        blk = rows // steps
        in_specs.append(pl.BlockSpec((None, blk, cols),
                                     lambda *g: (layer, step_index(*g), 0)))
        out_specs.append(pl.BlockSpec((blk, cols), lambda *g: (step_index(*g), 0)))
        out_shapes.append(jax.ShapeDtypeStruct((rows, cols), BF16))
    return in_specs, out_specs, out_shapes


def _out0_kernel(h_ref, a_ref, gb_ref, cin_ref, cprev_ref, cnext_ref, cw_ref, w_ref, wup_ref, wdn_ref,
                 o_ref, wup_o_ref, wdn_o_ref):
    _cast_weight_slices(wup_ref, wdn_ref, wup_o_ref, wdn_o_ref)
    i = pl.program_id(1)
    bm = cin_ref.shape[0]
    c = cin_ref[...].astype(F32)
    prev_row = cprev_ref[BF16_SUBLANES - 1:BF16_SUBLANES, :].astype(F32)
    next_row = cnext_ref[0:1, :].astype(F32)
    prev_row = jnp.where(i == 0, 0.0, prev_row)
    next_row = jnp.where(i == pl.num_programs(1) - 1, 0.0, next_row)
    rows = lax.broadcasted_iota(jnp.int32, (bm, 1), 0)
    c_m1 = jnp.where(rows == 0, prev_row, pltpu.roll(c, 1, 0))
    c_p1 = jnp.where(rows == bm - 1, next_row, pltpu.roll(c, bm - 1, 0))
    conv = cw_ref[0:1, :] * c_m1 + cw_ref[1:2, :] * c + cw_ref[2:3, :] * c_p1
    sconv = (gb_ref[...].astype(F32) * conv).astype(BF16)
    aw = a_ref.shape[1]
    y = jnp.dot(a_ref[...], w_ref[:aw, :], preferred_element_type=F32)
    y = y + jnp.dot(sconv, w_ref[aw:, :], preferred_element_type=F32)
    o_ref[...] = h_ref[...] + y


def _out0(h, attn, gb, cin, conv_w, w, mlp_w, layer, bm):
    b, s, d = h.shape
    nb = s // bm
    w_specs, w_out_specs, w_out_shapes = _weight_cast_specs(mlp_w, layer, b * nb,
                                                            lambda bi, i: bi * nb + i)
    r = bm // BF16_SUBLANES
    last = s // BF16_SUBLANES - 1
    tile = lambda width: pl.BlockSpec((None, bm, width), lambda bi, i: (bi, i, 0))
    return pl.pallas_call(
        _out0_kernel,
        grid=(b, nb),
        in_specs=[
            tile(d), tile(attn.shape[2]), tile(CONV_WIDTH), tile(CONV_WIDTH),
            pl.BlockSpec((None, BF16_SUBLANES, CONV_WIDTH),
                         lambda bi, i: (bi, jnp.maximum(i * r - 1, 0), 0)),
            pl.BlockSpec((None, BF16_SUBLANES, CONV_WIDTH),
                         lambda bi, i: (bi, jnp.minimum((i + 1) * r, last), 0)),
            pl.BlockSpec(conv_w.shape, lambda bi, i: (0, 0)),
            pl.BlockSpec(w.shape, lambda bi, i: (0, 0)),
        ] + w_specs,
        out_specs=[tile(d)] + w_out_specs,
        out_shape=[jax.ShapeDtypeStruct(h.shape, F32)] + w_out_shapes,
        compiler_params=_params(("parallel", "parallel")),
        name="out0",
    )(h, attn, gb, cin, cin, cin, conv_w, w, *mlp_w)


def _out1_kernel(h_ref, a_ref, w_ref, wup_ref, wdn_ref, o_ref, wup_o_ref, wdn_o_ref):
    _cast_weight_slices(wup_ref, wdn_ref, wup_o_ref, wdn_o_ref)
    o_ref[...] = h_ref[...] + jnp.dot(a_ref[...], w_ref[...], preferred_element_type=F32)


def _out1(h, attn, w, mlp_w, layer, bm):
    t, d = h.shape
    w_specs, w_out_specs, w_out_shapes = _weight_cast_specs(mlp_w, layer, t // bm, lambda i: i)
    return pl.pallas_call(
        _out1_kernel,
        grid=(t // bm,),
        in_specs=[
            pl.BlockSpec((bm, d), lambda i: (i, 0)),
            pl.BlockSpec((bm, attn.shape[1]), lambda i: (i, 0)),
            pl.BlockSpec(w.shape, lambda i: (0, 0)),
        ] + w_specs,
        out_specs=[pl.BlockSpec((bm, d), lambda i: (i, 0))] + w_out_specs,
        out_shape=[jax.ShapeDtypeStruct(h.shape, F32)] + w_out_shapes,
        compiler_params=_params(("parallel",)),
        name="out1",
    )(h, attn, w, *mlp_w)


def _mlp_kernel(h_ref, g_ref, wup_ref, wdn_ref, fg_ref, o_ref, xn_sc, *, final_norm):
    f = pl.program_id(1)

    def mlp_chunk(xn):
        u = jnp.dot(xn, wup_ref[...], preferred_element_type=F32)
        a = jnp.square(jnp.maximum(u, 0.0)).astype(BF16)
        return jnp.dot(a, wdn_ref[...], preferred_element_type=F32)

    @pl.when(f == 0)
    def _():
        x = h_ref[...]
        xn = _rms(x, g_ref[...]).astype(BF16)
        xn_sc[...] = xn
        o_ref[...] = x + mlp_chunk(xn)

    @pl.when(f > 0)
    def _():
        o_ref[...] += mlp_chunk(xn_sc[...])

    if final_norm:
        @pl.when(f == pl.num_programs(1) - 1)
        def _():
            o_ref[...] = _rms(o_ref[...], fg_ref[...])


def _mlp(h, g, wup, wdn, fg, *, bm, bf, final_norm):
    t, d = h.shape
    dff = wup.shape[1]
    return pl.pallas_call(
        functools.partial(_mlp_kernel, final_norm=final_norm),
        grid=(t // bm, dff // bf),
        in_specs=[
            pl.BlockSpec((bm, d), lambda i, f: (i, 0)),
            pl.BlockSpec((1, d), lambda i, f: (0, 0)),
            pl.BlockSpec((d, bf), lambda i, f: (0, f)),
            pl.BlockSpec((bf, d), lambda i, f: (f, 0)),
            pl.BlockSpec((1, d), lambda i, f: (0, 0)),
        ],
        out_specs=pl.BlockSpec((bm, d), lambda i, f: (i, 0)),
        out_shape=jax.ShapeDtypeStruct(h.shape, F32),
        scratch_shapes=[pltpu.VMEM((bm, d), BF16)],
        compiler_params=_params(("parallel", "arbitrary")),
        name="mlp_final" if final_norm else "mlp",
    )(h, g, wup, wdn, fg)


def _rope_cos_sin(seq, rot_dim):
    rows = seq // GRID_W
    axis_dim = rot_dim // 2
    inv_freq = ROPE_THETA ** (-jnp.arange(0, axis_dim, 2, dtype=F32) / axis_dim)
    ang_row = jnp.arange(rows, dtype=F32)[:, None] * inv_freq
    ang_col = jnp.arange(GRID_W, dtype=F32)[:, None] * inv_freq
    shape = (rows, GRID_W, inv_freq.shape[0])

    def per_token(fn):
        by_row = jnp.broadcast_to(fn(ang_row)[:, None, :], shape)
        by_col = jnp.broadcast_to(fn(ang_col)[None, :, :], shape)
        return jnp.concatenate([by_row, by_col], axis=-1).reshape(seq, rot_dim // 2)

    return per_token(jnp.cos), per_token(jnp.sin)


def _rope_tables(seq, rot_dim):
    cos, sin = _rope_cos_sin(seq, rot_dim)
    pad = jnp.zeros((seq, LANES // 2 - rot_dim // 2), F32)
    cos_t = jnp.concatenate([cos, pad, cos, pad], axis=-1)
    sin_t = jnp.concatenate([-sin, pad, sin, pad], axis=-1)
    return cos_t, sin_t


def _pair_split(w):
    return w[..., 0::2], w[..., 1::2]


def _pair_split_sources(rot_dim, pad):
    blank = [-1] * pad
    return list(range(0, rot_dim, 2)) + blank + list(range(1, rot_dim, 2)) + blank


_PAIR_SPLIT_128 = _pair_split_sources(HEAD_DIM, 0)
_ROPE_PAD_64 = _pair_split_sources(MLA_ROPE, LANES // 2 - MLA_ROPE // 2)
_MLA_Q_HEAD = list(range(MLA_NOPE)) + [MLA_NOPE + c if c >= 0 else -1 for c in _ROPE_PAD_64]


def _relayout(w, sources):
    sel = [[1.0 if src == k else 0.0 for src in sources] for k in range(w.shape[-1])]
    return jnp.dot(w, jnp.array(sel, BF16), preferred_element_type=BF16)


def kernel(x, even_norm_g, even_w_in, even_q_norm_g, even_k_norm_g, even_conv_w, even_w_out,
           odd_norm_g, odd_w_down, odd_q_lat_g, odd_kv_lat_g, odd_w_uq, odd_w_ukv, odd_w_o,
           mlp_norm_g, mlp_w_up, mlp_w_down, final_norm_g):
    b, s, d = x.shape
    t = _tiles(s)
    depth = mlp_norm_g.shape[0]
    cos_a, sin_a = _rope_tables(s, HEAD_DIM)
    cos_c, sin_c = _rope_tables(s, MLA_ROPE)
    fg = final_norm_g.reshape(1, d)
    mlp_w = (mlp_w_up, mlp_w_down)

    h = x
    for layer in range(depth):
        i = layer // 2
        if layer % 2 == 0:
            w_in = even_w_in[i].astype(BF16)
            wq = w_in[:, :ATTN_WIDTH].reshape(d, ATTN_HEADS, HEAD_DIM)
            wk = w_in[:, ATTN_WIDTH:ATTN_WIDTH + KV_WIDTH].reshape(d, ATTN_KV_HEADS, HEAD_DIM)
            wq = _relayout(wq, _PAIR_SPLIT_128).reshape(d, ATTN_WIDTH)
            wk = _relayout(wk, _PAIR_SPLIT_128).reshape(d, KV_WIDTH)
            w0 = jnp.concatenate([wq, wk, w_in[:, ATTN_WIDTH + KV_WIDTH:]], axis=-1)
            qg = jnp.concatenate(_pair_split(even_q_norm_g[i]), axis=-1).reshape(1, HEAD_DIM)
            kg = jnp.concatenate(_pair_split(even_k_norm_g[i]), axis=-1).reshape(1, HEAD_DIM)
            q, k, vt, gb, cin = _proj0(h, even_norm_g[i].reshape(1, d), w0, qg, kg,
                                       cos_a, sin_a, t["chunk"], t["kv_chunk"])
            attn = _attention(q, k, vt, group=ATTN_GROUP, bq=t["bq_gqa"], name="gqa_attn")
            h, w_up, w_dn = _out0(h, attn, gb, cin, even_conv_w[i], even_w_out[i].astype(BF16),
                                  mlp_w, layer, t["bm_out"])
        else:
            w_down = odd_w_down[i].astype(BF16)
            wd = jnp.concatenate([w_down[:, :Q_LORA + KV_LORA],
                                  _relayout(w_down[:, Q_LORA + KV_LORA:], _ROPE_PAD_64)], axis=-1)
            wuq = odd_w_uq[i].astype(BF16).reshape(Q_LORA, MLA_HEADS, MLA_NOPE + MLA_ROPE)
            wuq = _relayout(wuq, _MLA_Q_HEAD).reshape(Q_LORA, MLA_HEADS * MLA_QK_PAD)
            q, k, vt = _proj1(h, odd_norm_g[i].reshape(1, d), wd,
                              odd_q_lat_g[i].reshape(1, Q_LORA), odd_kv_lat_g[i].reshape(1, KV_LORA),
                              wuq, odd_w_ukv[i].astype(BF16), cos_c, sin_c, t["chunk"],
                              t["kv_chunk"])
            attn = _attention(q, k, vt, group=1, bq=t["bq_mla"], name="mla_attn")
            h, w_up, w_dn = _out1(h.reshape(b * s, d), attn.reshape(b * s, -1),
                                  odd_w_o[i].astype(BF16), mlp_w, layer, t["bm_out"])
        h = _mlp(h.reshape(b * s, d), mlp_norm_g[layer].reshape(1, d), w_up, w_dn, fg,
                 bm=t["bm_mlp"], bf=t["bf_mlp"],
                 final_norm=(layer == depth - 1)).reshape(b, s, d)
    return h
```

```python
import functools

import jax
import jax.numpy as jnp
from jax import lax
from jax.experimental import pallas as pl
from jax.experimental.pallas import tpu as pltpu

F32 = jnp.float32
BF16 = jnp.bfloat16

NORM_EPS = 1e-6
ROPE_THETA = 10000.0
GRID_W = 64

HEAD_DIM = 128
ATTN_HEADS = 8
ATTN_KV_HEADS = 2
ATTN_GROUP = ATTN_HEADS // ATTN_KV_HEADS
ATTN_WIDTH = ATTN_HEADS * HEAD_DIM
KV_WIDTH = ATTN_KV_HEADS * HEAD_DIM
CONV_WIDTH = 1024

MLA_HEADS = 16
MLA_NOPE = 128
MLA_ROPE = 64
MLA_V = 128
Q_LORA = 512
KV_LORA = 512
MLA_QK_PAD = 256

LANES = 128
BF16_SUBLANES = 16
VMEM_LIMIT_BYTES = 56 * 1024 * 1024
NEG_BIG = -1e30
LOG2_E = 1.4426950408889634
ATTN_COL_GROUP = 512
ATTN_TRIP_ITEMS = 128
ATTN_LOOKAHEAD = 2
ATTN_RING = 4


def _tiles(seq):
    return dict(
        chunk=min(512, seq),
        kv_chunk=min(512, seq),
        bm_mlp=min(1024, seq),
        bf_mlp=512,
        bm_out=min(512, seq),
        bq_gqa=min(1024, seq),
        bq_mla=min(4096, seq),
    )


def _params(sem):
    return pltpu.CompilerParams(dimension_semantics=sem, vmem_limit_bytes=VMEM_LIMIT_BYTES)


def _rms(x, g):
    return x * lax.rsqrt(jnp.mean(x * x, axis=-1, keepdims=True) + NORM_EPS) * g


def _store_vt(vt_ref, head, v):
    vt = v.T.astype(BF16)
    for i in range(vt_ref.shape[1]):
        vt_ref[head, i] = vt[:, i * LANES:(i + 1) * LANES]


def _rope(x, cos, sin):
    return x * cos + pltpu.roll(x, LANES // 2, 1) * sin


def _proj0_kernel(h_ref, g_ref, w_ref, qg_ref, kg_ref, cos_ref, sin_ref,
                  q_ref, k_ref, vt_ref, gb_ref, cin_ref):
    xn = _rms(h_ref[...], g_ref[...]).astype(BF16)
    y = jnp.dot(xn, w_ref[...], preferred_element_type=F32)
    cos = cos_ref[...]
    sin = sin_ref[...]
    scale = HEAD_DIM ** -0.5 * LOG2_E
    for hh in range(ATTN_HEADS):
        yh = _rms(y[:, hh * HEAD_DIM:(hh + 1) * HEAD_DIM], qg_ref[...])
        q_ref[hh] = (_rope(yh, cos, sin) * scale).astype(BF16)
    o = ATTN_WIDTH
    for hh in range(ATTN_KV_HEADS):
        yh = _rms(y[:, o + hh * HEAD_DIM:o + (hh + 1) * HEAD_DIM], kg_ref[...])
        k_ref[hh, 0] = _rope(yh, cos, sin).astype(BF16)
    o += KV_WIDTH
    for hh in range(ATTN_KV_HEADS):
        _store_vt(vt_ref, hh, y[:, o + hh * HEAD_DIM:o + (hh + 1) * HEAD_DIM])
    o += KV_WIDTH
    gb_ref[...] = y[:, o:o + CONV_WIDTH].astype(BF16)
    o += CONV_WIDTH
    cin_ref[...] = (y[:, o:o + CONV_WIDTH] * y[:, o + CONV_WIDTH:o + 2 * CONV_WIDTH]).astype(BF16)


def _proj0(h, g, w, qg, kg, cos, sin, chunk, kv_chunk):
    b, s, d = h.shape
    n = w.shape[1]
    nc = s // chunk
    r = kv_chunk // chunk
    return pl.pallas_call(
        _proj0_kernel,
        grid=(b, nc),
        in_specs=[
            pl.BlockSpec((None, chunk, d), lambda bi, i: (bi, i, 0)),
            pl.BlockSpec((1, d), lambda bi, i: (0, 0)),
            pl.BlockSpec((d, n), lambda bi, i: (0, 0)),
            pl.BlockSpec((1, HEAD_DIM), lambda bi, i: (0, 0)),
            pl.BlockSpec((1, HEAD_DIM), lambda bi, i: (0, 0)),
            pl.BlockSpec((chunk, HEAD_DIM), lambda bi, i: (i, 0)),
            pl.BlockSpec((chunk, HEAD_DIM), lambda bi, i: (i, 0)),
        ],
        out_specs=[
            pl.BlockSpec((None, ATTN_HEADS, chunk, HEAD_DIM), lambda bi, i: (bi, 0, i, 0)),
            pl.BlockSpec((None, ATTN_KV_HEADS, 1, chunk, LANES), lambda bi, i: (bi, 0, 0, i, 0)),
            pl.BlockSpec((None, ATTN_KV_HEADS, None, chunk // LANES, HEAD_DIM, LANES),
                         lambda bi, i: (bi, 0, i // r, i % r, 0, 0)),
            pl.BlockSpec((None, chunk, CONV_WIDTH), lambda bi, i: (bi, i, 0)),
            pl.BlockSpec((None, chunk, CONV_WIDTH), lambda bi, i: (bi, i, 0)),
        ],
        out_shape=[
            jax.ShapeDtypeStruct((b, ATTN_HEADS, s, HEAD_DIM), BF16),
            jax.ShapeDtypeStruct((b, ATTN_KV_HEADS, 1, s, LANES), BF16),
            jax.ShapeDtypeStruct((b, ATTN_KV_HEADS, s // kv_chunk, kv_chunk // LANES, HEAD_DIM, LANES),
                                 BF16),
            jax.ShapeDtypeStruct((b, s, CONV_WIDTH), BF16),
            jax.ShapeDtypeStruct((b, s, CONV_WIDTH), BF16),
        ],
        compiler_params=_params(("parallel", "parallel")),
        name="proj0",
    )(h, g, w, qg, kg, cos, sin)


def _proj1_kernel(h_ref, g_ref, wd_ref, qg_ref, kvg_ref, wuq_ref, wukv_ref, cos_ref, sin_ref,
                  q_ref, k_ref, vt_ref):
    xn = _rms(h_ref[...], g_ref[...]).astype(BF16)
    lat = jnp.dot(xn, wd_ref[...], preferred_element_type=F32)
    cq = _rms(lat[:, :Q_LORA], qg_ref[...]).astype(BF16)
    ckv = _rms(lat[:, Q_LORA:Q_LORA + KV_LORA], kvg_ref[...]).astype(BF16)
    cos = cos_ref[...]
    sin = sin_ref[...]
    kr = _rope(lat[:, Q_LORA + KV_LORA:], cos, sin).astype(BF16)
    q = jnp.dot(cq, wuq_ref[...], preferred_element_type=F32)
    kv = jnp.dot(ckv, wukv_ref[...], preferred_element_type=F32)
    scale = (MLA_NOPE + MLA_ROPE) ** -0.5 * LOG2_E
    for hh in range(MLA_HEADS):
        o = hh * MLA_QK_PAD
        q_ref[hh, :, :MLA_NOPE] = (q[:, o:o + MLA_NOPE] * scale).astype(BF16)
        qr = _rope(q[:, o + MLA_NOPE:o + MLA_QK_PAD], cos, sin)
        q_ref[hh, :, MLA_NOPE:] = (qr * scale).astype(BF16)
        o = hh * (MLA_NOPE + MLA_V)
        k_ref[hh, 0] = kv[:, o:o + MLA_NOPE].astype(BF16)
        k_ref[hh, 1] = kr
        _store_vt(vt_ref, hh, kv[:, o + MLA_NOPE:o + MLA_NOPE + MLA_V])


def _proj1(h, g, wd, qg, kvg, wuq, wukv, cos, sin, chunk, kv_chunk):
    b, s, d = h.shape
    nc = s // chunk
    r = kv_chunk // chunk
    const = lambda bi, i: (0, 0)
    return pl.pallas_call(
        _proj1_kernel,
        grid=(b, nc),
        in_specs=[
            pl.BlockSpec((None, chunk, d), lambda bi, i: (bi, i, 0)),
            pl.BlockSpec((1, d), const),
            pl.BlockSpec(wd.shape, const),
            pl.BlockSpec((1, Q_LORA), const),
            pl.BlockSpec((1, KV_LORA), const),
            pl.BlockSpec(wuq.shape, const),
            pl.BlockSpec(wukv.shape, const),
            pl.BlockSpec((chunk, LANES), lambda bi, i: (i, 0)),
            pl.BlockSpec((chunk, LANES), lambda bi, i: (i, 0)),
        ],
        out_specs=[
            pl.BlockSpec((None, MLA_HEADS, chunk, MLA_QK_PAD), lambda bi, i: (bi, 0, i, 0)),
            pl.BlockSpec((None, MLA_HEADS, MLA_QK_PAD // LANES, chunk, LANES),
                         lambda bi, i: (bi, 0, 0, i, 0)),
            pl.BlockSpec((None, MLA_HEADS, None, chunk // LANES, MLA_V, LANES),
                         lambda bi, i: (bi, 0, i // r, i % r, 0, 0)),
        ],
        out_shape=[
            jax.ShapeDtypeStruct((b, MLA_HEADS, s, MLA_QK_PAD), BF16),
            jax.ShapeDtypeStruct((b, MLA_HEADS, MLA_QK_PAD // LANES, s, LANES), BF16),
            jax.ShapeDtypeStruct((b, MLA_HEADS, s // kv_chunk, kv_chunk // LANES, MLA_V, LANES), BF16),
        ],
        compiler_params=_params(("parallel", "parallel")),
        name="proj1",
    )(h, g, wd, qg, kvg, wuq, wukv, cos, sin)


def _attn_kernel(q_ref, k_ref, vt_ref, o_ref, m_sc, acc_sc, s_sc, mx_sc, qt_sc, *, group, bq,
                 chunk, n_chunks, unroll):
    n = group * bq
    gw = s_sc.shape[2]
    n_groups = n // gw
    ring = s_sc.shape[0]
    m_sc[...] = jnp.full(m_sc.shape, NEG_BIG, F32)
    acc_sc[...] = jnp.zeros(acc_sc.shape, F32)
    for gi in range(n_groups):
        qg = q_ref[(gi * gw) // bq, pl.ds((gi * gw) % bq, gw), :]
        qt_sc[:, gi * gw:(gi + 1) * gw] = qg.T

    ones_rows = (lax.broadcasted_iota(jnp.int32, (BF16_SUBLANES, chunk), 0) == 0).astype(BF16)

    items = [(j, gi) for j in range(unroll) for gi in range(n_groups)]
    assert len(items) % ring == 0
    lookahead = min(ATTN_LOOKAHEAD, ring - 1)

    def scores(c, gi, slot):
        rows = pl.ds(pl.multiple_of(c * chunk, chunk), chunk)
        kc = jnp.concatenate([k_ref[i, rows, :] for i in range(k_ref.shape[0])], axis=1)
        s = jnp.dot(kc, qt_sc[:, gi * gw:(gi + 1) * gw], preferred_element_type=F32)
        s_sc[slot] = s
        mx_sc[slot] = jnp.max(s, axis=0, keepdims=True)

    for idx in range(lookahead):
        scores(items[idx][0], items[idx][1], idx)

    def body(t, carry):
        for idx, (j, gi) in enumerate(items):
            ahead = idx + lookahead
            ja, ga = items[ahead % len(items)]
            ca = jnp.minimum((t + ahead // len(items)) * unroll + ja, n_chunks - 1)
            scores(ca, ga, ahead % ring)

            cols = slice(gi * gw, (gi + 1) * gw)
            slot = idx % ring
            m_prev = m_sc[:, cols]
            m_new = jnp.maximum(m_prev, mx_sc[slot])
            alpha = jnp.exp2(m_prev - m_new)
            p = jnp.exp2(s_sc[slot] - m_new)
            vt = jnp.concatenate([vt_ref[t * unroll + j, i] for i in range(vt_ref.shape[1])], axis=1)
            vt = jnp.concatenate([vt, ones_rows], axis=0)
            pv = jnp.dot(vt, p.astype(BF16), preferred_element_type=F32)
            acc_sc[:, cols] = alpha * acc_sc[:, cols] + pv
            m_sc[:, cols] = m_new
        return carry

    lax.fori_loop(0, n_chunks // unroll, body, 0)
    dv = acc_sc.shape[0] - BF16_SUBLANES
    o = acc_sc[:dv, :] / acc_sc[dv:dv + 1, :]
    for gi in range(group):
        o_ref[:, gi * dv:(gi + 1) * dv] = o[:, gi * bq:(gi + 1) * bq].T.astype(o_ref.dtype)


def _attention(q, k, vt, *, group, bq, name):
    b, h, s, dqk = q.shape
    hkv = k.shape[1]
    n_chunks, slabs, dv, _ = vt.shape[2:]
    chunk = slabs * LANES
    n = group * bq
    gw = min(ATTN_COL_GROUP, bq)
    unroll = max(1, ATTN_TRIP_ITEMS // (n // gw))
    while n_chunks % unroll:
        unroll -= 1
    items = unroll * (n // gw)
    ring = ATTN_RING if items % ATTN_RING == 0 else items
    kern = functools.partial(_attn_kernel, group=group, bq=bq, chunk=chunk, n_chunks=n_chunks,
                             unroll=unroll)
    return pl.pallas_call(
        kern,
        grid=(b, hkv, s // bq),
        in_specs=[
            pl.BlockSpec((None, group, bq, dqk), lambda bi, hi, qi: (bi, hi, qi, 0)),
            pl.BlockSpec((None, None, dqk // LANES, s, LANES), lambda bi, hi, qi: (bi, hi, 0, 0, 0)),
            pl.BlockSpec((None, None, n_chunks, slabs, dv, LANES),
                         lambda bi, hi, qi: (bi, hi, 0, 0, 0, 0)),
        ],
        out_specs=pl.BlockSpec((None, bq, group * dv), lambda bi, hi, qi: (bi, qi, hi)),
        out_shape=jax.ShapeDtypeStruct((b, s, h * dv), BF16),
        scratch_shapes=[
            pltpu.VMEM((1, n), F32),
            pltpu.VMEM((dv + BF16_SUBLANES, n), F32),
            pltpu.VMEM((ring, chunk, gw), F32),
            pltpu.VMEM((ring, 1, gw), F32),
            pltpu.VMEM((dqk, n), BF16),
        ],
        compiler_params=_params(("parallel", "parallel", "arbitrary")),
        name=name,
    )(q, k, vt)


def _cast_weight_slices(wup_ref, wdn_ref, wup_o_ref, wdn_o_ref):
    wup_o_ref[...] = wup_ref[...].astype(BF16)
    wdn_o_ref[...] = wdn_ref[...].astype(BF16)


def _weight_cast_specs(mlp_w, layer, steps, step_index):
    in_specs, out_specs, out_shapes = [], [], []
    for w in mlp_w:
        rows, cols = w.shape[1:]
        assert rows % (steps * BF16_SUBLANES) == 0
        blk = rows // steps
        in_specs.append(pl.BlockSpec((None, blk, cols),
                                     lambda *g: (layer, step_index(*g), 0)))
        out_specs.append(pl.BlockSpec((blk, cols), lambda *g: (step_index(*g), 0)))
        out_shapes.append(jax.ShapeDtypeStruct((rows, cols), BF16))
    return in_specs, out_specs, out_shapes


def _out0_kernel(h_ref, a_ref, gb_ref, cin_ref, cprev_ref, cnext_ref, cw_ref, w_ref, wup_ref, wdn_ref,
                 o_ref, wup_o_ref, wdn_o_ref):
    _cast_weight_slices(wup_ref, wdn_ref, wup_o_ref, wdn_o_ref)
    i = pl.program_id(1)
    bm = cin_ref.shape[0]
    c = cin_ref[...].astype(F32)
    prev_row = cprev_ref[BF16_SUBLANES - 1:BF16_SUBLANES, :].astype(F32)
    next_row = cnext_ref[0:1, :].astype(F32)
    prev_row = jnp.where(i == 0, 0.0, prev_row)
    next_row = jnp.where(i == pl.num_programs(1) - 1, 0.0, next_row)
    rows = lax.broadcasted_iota(jnp.int32, (bm, 1), 0)
    c_m1 = jnp.where(rows == 0, prev_row, pltpu.roll(c, 1, 0))
    c_p1 = jnp.where(rows == bm - 1, next_row, pltpu.roll(c, bm - 1, 0))
    conv = cw_ref[0:1, :] * c_m1 + cw_ref[1:2, :] * c + cw_ref[2:3, :] * c_p1
    sconv = (gb_ref[...].astype(F32) * conv).astype(BF16)
    aw = a_ref.shape[1]
    y = jnp.dot(a_ref[...], w_ref[:aw, :], preferred_element_type=F32)
    y = y + jnp.dot(sconv, w_ref[aw:, :], preferred_element_type=F32)
    o_ref[...] = h_ref[...] + y


def _out0(h, attn, gb, cin, conv_w, w, mlp_w, layer, bm):
    b, s, d = h.shape
    nb = s // bm
    w_specs, w_out_specs, w_out_shapes = _weight_cast_specs(mlp_w, layer, b * nb,
                                                            lambda bi, i: bi * nb + i)
    r = bm // BF16_SUBLANES
    last = s // BF16_SUBLANES - 1
    tile = lambda width: pl.BlockSpec((None, bm, width), lambda bi, i: (bi, i, 0))
    return pl.pallas_call(
        _out0_kernel,
        grid=(b, nb),
        in_specs=[
            tile(d), tile(attn.shape[2]), tile(CONV_WIDTH), tile(CONV_WIDTH),
            pl.BlockSpec((None, BF16_SUBLANES, CONV_WIDTH),
                         lambda bi, i: (bi, jnp.maximum(i * r - 1, 0), 0)),
            pl.BlockSpec((None, BF16_SUBLANES, CONV_WIDTH),
                         lambda bi, i: (bi, jnp.minimum((i + 1) * r, last), 0)),
            pl.BlockSpec(conv_w.shape, lambda bi, i: (0, 0)),
            pl.BlockSpec(w.shape, lambda bi, i: (0, 0)),
        ] + w_specs,
        out_specs=[tile(d)] + w_out_specs,
        out_shape=[jax.ShapeDtypeStruct(h.shape, F32)] + w_out_shapes,
        compiler_params=_params(("parallel", "parallel")),
        name="out0",
    )(h, attn, gb, cin, cin, cin, conv_w, w, *mlp_w)


def _out1_kernel(h_ref, a_ref, w_ref, wup_ref, wdn_ref, o_ref, wup_o_ref, wdn_o_ref):
    _cast_weight_slices(wup_ref, wdn_ref, wup_o_ref, wdn_o_ref)
    o_ref[...] = h_ref[...] + jnp.dot(a_ref[...], w_ref[...], preferred_element_type=F32)


def _out1(h, attn, w, mlp_w, layer, bm):
    t, d = h.shape
    w_specs, w_out_specs, w_out_shapes = _weight_cast_specs(mlp_w, layer, t // bm, lambda i: i)
    return pl.pallas_call(
        _out1_kernel,
        grid=(t // bm,),
        in_specs=[
            pl.BlockSpec((bm, d), lambda i: (i, 0)),
            pl.BlockSpec((bm, attn.shape[1]), lambda i: (i, 0)),
            pl.BlockSpec(w.shape, lambda i: (0, 0)),
        ] + w_specs,
        out_specs=[pl.BlockSpec((bm, d), lambda i: (i, 0))] + w_out_specs,
        out_shape=[jax.ShapeDtypeStruct(h.shape, F32)] + w_out_shapes,
        compiler_params=_params(("parallel",)),
        name="out1",
    )(h, attn, w, *mlp_w)


def _mlp_kernel(h_ref, g_ref, wup_ref, wdn_ref, fg_ref, o_ref, xn_sc, *, final_norm):
    f = pl.program_id(1)

    def mlp_chunk(xn):
        u = jnp.dot(xn, wup_ref[...], preferred_element_type=F32)
        a = jnp.square(jnp.maximum(u, 0.0)).astype(BF16)
        return jnp.dot(a, wdn_ref[...], preferred_element_type=F32)

    @pl.when(f == 0)
    def _():
        x = h_ref[...]
        xn = _rms(x, g_ref[...]).astype(BF16)
        xn_sc[...] = xn
        o_ref[...] = x + mlp_chunk(xn)

    @pl.when(f > 0)
    def _():
        o_ref[...] += mlp_chunk(xn_sc[...])

    if final_norm:
        @pl.when(f == pl.num_programs(1) - 1)
        def _():
            o_ref[...] = _rms(o_ref[...], fg_ref[...])


def _mlp(h, g, wup, wdn, fg, *, bm, bf, final_norm):
    t, d = h.shape
    dff = wup.shape[1]
    return pl.pallas_call(
        functools.partial(_mlp_kernel, final_norm=final_norm),
        grid=(t // bm, dff // bf),
        in_specs=[
            pl.BlockSpec((bm, d), lambda i, f: (i, 0)),
            pl.BlockSpec((1, d), lambda i, f: (0, 0)),
            pl.BlockSpec((d, bf), lambda i, f: (0, f)),
            pl.BlockSpec((bf, d), lambda i, f: (f, 0)),
            pl.BlockSpec((1, d), lambda i, f: (0, 0)),
        ],
        out_specs=pl.BlockSpec((bm, d), lambda i, f: (i, 0)),
        out_shape=jax.ShapeDtypeStruct(h.shape, F32),
        scratch_shapes=[pltpu.VMEM((bm, d), BF16)],
        compiler_params=_params(("parallel", "arbitrary")),
        name="mlp_final" if final_norm else "mlp",
    )(h, g, wup, wdn, fg)


def _rope_cos_sin(seq, rot_dim):
    rows = seq // GRID_W
    axis_dim = rot_dim // 2
    inv_freq = ROPE_THETA ** (-jnp.arange(0, axis_dim, 2, dtype=F32) / axis_dim)
    ang_row = jnp.arange(rows, dtype=F32)[:, None] * inv_freq
    ang_col = jnp.arange(GRID_W, dtype=F32)[:, None] * inv_freq
    shape = (rows, GRID_W, inv_freq.shape[0])

    def per_token(fn):
        by_row = jnp.broadcast_to(fn(ang_row)[:, None, :], shape)
        by_col = jnp.broadcast_to(fn(ang_col)[None, :, :], shape)
        return jnp.concatenate([by_row, by_col], axis=-1).reshape(seq, rot_dim // 2)

    return per_token(jnp.cos), per_token(jnp.sin)


def _rope_tables(seq, rot_dim):
    cos, sin = _rope_cos_sin(seq, rot_dim)
    pad = jnp.zeros((seq, LANES // 2 - rot_dim // 2), F32)
    cos_t = jnp.concatenate([cos, pad, cos, pad], axis=-1)
    sin_t = jnp.concatenate([-sin, pad, sin, pad], axis=-1)
    return cos_t, sin_t


def _pair_split(w):
    return w[..., 0::2], w[..., 1::2]


def _pair_split_sources(rot_dim, pad):
    blank = [-1] * pad
    return list(range(0, rot_dim, 2)) + blank + list(range(1, rot_dim, 2)) + blank


_PAIR_SPLIT_128 = _pair_split_sources(HEAD_DIM, 0)
_ROPE_PAD_64 = _pair_split_sources(MLA_ROPE, LANES // 2 - MLA_ROPE // 2)
_MLA_Q_HEAD = list(range(MLA_NOPE)) + [MLA_NOPE + c if c >= 0 else -1 for c in _ROPE_PAD_64]


def _relayout(w, sources):
    sel = [[1.0 if src == k else 0.0 for src in sources] for k in range(w.shape[-1])]
    return jnp.dot(w, jnp.array(sel, BF16), preferred_element_type=BF16)


def kernel(x, even_norm_g, even_w_in, even_q_norm_g, even_k_norm_g, even_conv_w, even_w_out,
           odd_norm_g, odd_w_down, odd_q_lat_g, odd_kv_lat_g, odd_w_uq, odd_w_ukv, odd_w_o,
           mlp_norm_g, mlp_w_up, mlp_w_down, final_norm_g):
    b, s, d = x.shape
    t = _tiles(s)
    depth = mlp_norm_g.shape[0]
    cos_a, sin_a = _rope_tables(s, HEAD_DIM)
    cos_c, sin_c = _rope_tables(s, MLA_ROPE)
    fg = final_norm_g.reshape(1, d)
    mlp_w = (mlp_w_up, mlp_w_down)

    h = x
    for layer in range(depth):
        i = layer // 2
        if layer % 2 == 0:
            w_in = even_w_in[i].astype(BF16)
            wq = w_in[:, :ATTN_WIDTH].reshape(d, ATTN_HEADS, HEAD_DIM)
            wk = w_in[:, ATTN_WIDTH:ATTN_WIDTH + KV_WIDTH].reshape(d, ATTN_KV_HEADS, HEAD_DIM)
            wq = _relayout(wq, _PAIR_SPLIT_128).reshape(d, ATTN_WIDTH)
            wk = _relayout(wk, _PAIR_SPLIT_128).reshape(d, KV_WIDTH)
            w0 = jnp.concatenate([wq, wk, w_in[:, ATTN_WIDTH + KV_WIDTH:]], axis=-1)
            qg = jnp.concatenate(_pair_split(even_q_norm_g[i]), axis=-1).reshape(1, HEAD_DIM)
            kg = jnp.concatenate(_pair_split(even_k_norm_g[i]), axis=-1).reshape(1, HEAD_DIM)
            q, k, vt, gb, cin = _proj0(h, even_norm_g[i].reshape(1, d), w0, qg, kg,
                                       cos_a, sin_a, t["chunk"], t["kv_chunk"])
            attn = _attention(q, k, vt, group=ATTN_GROUP, bq=t["bq_gqa"], name="gqa_attn")
            h, w_up, w_dn = _out0(h, attn, gb, cin, even_conv_w[i], even_w_out[i].astype(BF16),
                                  mlp_w, layer, t["bm_out"])
        else:
            w_down = odd_w_down[i].astype(BF16)
            wd = jnp.concatenate([w_down[:, :Q_LORA + KV_LORA],
                                  _relayout(w_down[:, Q_LORA + KV_LORA:], _ROPE_PAD_64)], axis=-1)
            wuq = odd_w_uq[i].astype(BF16).reshape(Q_LORA, MLA_HEADS, MLA_NOPE + MLA_ROPE)
            wuq = _relayout(wuq, _MLA_Q_HEAD).reshape(Q_LORA, MLA_HEADS * MLA_QK_PAD)
            q, k, vt = _proj1(h, odd_norm_g[i].reshape(1, d), wd,
                              odd_q_lat_g[i].reshape(1, Q_LORA), odd_kv_lat_g[i].reshape(1, KV_LORA),
                              wuq, odd_w_ukv[i].astype(BF16), cos_c, sin_c, t["chunk"],
                              t["kv_chunk"])
            attn = _attention(q, k, vt, group=1, bq=t["bq_mla"], name="mla_attn")
            h, w_up, w_dn = _out1(h.reshape(b * s, d), attn.reshape(b * s, -1),
                                  odd_w_o[i].astype(BF16), mlp_w, layer, t["bm_out"])
        h = _mlp(h.reshape(b * s, d), mlp_norm_g[layer].reshape(1, d), w_up, w_dn, fg,
                 bm=t["bm_mlp"], bf=t["bf_mlp"],
                 final_norm=(layer == depth - 1)).reshape(b, s, d)
    return h
```

```python
import functools

import jax
import jax.numpy as jnp
from jax import lax
from jax.experimental import pallas as pl
from jax.experimental.pallas import tpu as pltpu

F32 = jnp.float32
BF16 = jnp.bfloat16

NORM_EPS = 1e-6
ROPE_THETA = 10000.0
GRID_W = 64

HEAD_DIM = 128
ATTN_HEADS = 8
ATTN_KV_HEADS = 2
ATTN_GROUP = ATTN_HEADS // ATTN_KV_HEADS
ATTN_WIDTH = ATTN_HEADS * HEAD_DIM
KV_WIDTH = ATTN_KV_HEADS * HEAD_DIM
CONV_WIDTH = 1024

MLA_HEADS = 16
MLA_NOPE = 128
MLA_ROPE = 64
MLA_V = 128
Q_LORA = 512
KV_LORA = 512
MLA_QK_PAD = 256

LANES = 128
BF16_SUBLANES = 16
VMEM_LIMIT_BYTES = 56 * 1024 * 1024
NEG_BIG = -1e30
LOG2_E = 1.4426950408889634
ATTN_COL_GROUP = 512
ATTN_TRIP_ITEMS = 256
ATTN_LOOKAHEAD = 2
ATTN_RING = 4


def _tiles(seq):
    return dict(
        chunk=min(512, seq),
        kv_chunk=min(512, seq),
        bm_mlp=min(1024, seq),
        bf_mlp=512,
        bm_out=min(512, seq),
        bq_gqa=min(1024, seq),
        bq_mla=min(4096, seq),
    )


def _params(sem):
    return pltpu.CompilerParams(dimension_semantics=sem, vmem_limit_bytes=VMEM_LIMIT_BYTES)


def _rms(x, g):
    return x * lax.rsqrt(jnp.mean(x * x, axis=-1, keepdims=True) + NORM_EPS) * g


def _store_vt(vt_ref, head, v):
    vt = v.T.astype(BF16)
    for i in range(vt_ref.shape[1]):
        vt_ref[head, i] = vt[:, i * LANES:(i + 1) * LANES]


def _rope(x, cos, sin):
    return x * cos + pltpu.roll(x, LANES // 2, 1) * sin


def _proj0_kernel(h_ref, g_ref, w_ref, qg_ref, kg_ref, cos_ref, sin_ref,
                  q_ref, k_ref, vt_ref, gb_ref, cin_ref):
    xn = _rms(h_ref[...], g_ref[...]).astype(BF16)
    y = jnp.dot(xn, w_ref[...], preferred_element_type=F32)
    cos = cos_ref[...]
    sin = sin_ref[...]
    scale = HEAD_DIM ** -0.5 * LOG2_E
    for hh in range(ATTN_HEADS):
        yh = _rms(y[:, hh * HEAD_DIM:(hh + 1) * HEAD_DIM], qg_ref[...])
        q_ref[hh] = (_rope(yh, cos, sin) * scale).astype(BF16)
    o = ATTN_WIDTH
    for hh in range(ATTN_KV_HEADS):
        yh = _rms(y[:, o + hh * HEAD_DIM:o + (hh + 1) * HEAD_DIM], kg_ref[...])
        k_ref[hh, 0] = _rope(yh, cos, sin).astype(BF16)
    o += KV_WIDTH
    for hh in range(ATTN_KV_HEADS):
        _store_vt(vt_ref, hh, y[:, o + hh * HEAD_DIM:o + (hh + 1) * HEAD_DIM])
    o += KV_WIDTH
    gb_ref[...] = y[:, o:o + CONV_WIDTH].astype(BF16)
    o += CONV_WIDTH
    cin_ref[...] = (y[:, o:o + CONV_WIDTH] * y[:, o + CONV_WIDTH:o + 2 * CONV_WIDTH]).astype(BF16)


def _proj0(h, g, w, qg, kg, cos, sin, chunk, kv_chunk):
    b, s, d = h.shape
    n = w.shape[1]
    nc = s // chunk
    r = kv_chunk // chunk
    return pl.pallas_call(
        _proj0_kernel,
        grid=(b, nc),
        in_specs=[
            pl.BlockSpec((None, chunk, d), lambda bi, i: (bi, i, 0)),
            pl.BlockSpec((1, d), lambda bi, i: (0, 0)),
            pl.BlockSpec((d, n), lambda bi, i: (0, 0)),
            pl.BlockSpec((1, HEAD_DIM), lambda bi, i: (0, 0)),
            pl.BlockSpec((1, HEAD_DIM), lambda bi, i: (0, 0)),
            pl.BlockSpec((chunk, HEAD_DIM), lambda bi, i: (i, 0)),
            pl.BlockSpec((chunk, HEAD_DIM), lambda bi, i: (i, 0)),
        ],
        out_specs=[
            pl.BlockSpec((None, ATTN_HEADS, chunk, HEAD_DIM), lambda bi, i: (bi, 0, i, 0)),
            pl.BlockSpec((None, ATTN_KV_HEADS, 1, chunk, LANES), lambda bi, i: (bi, 0, 0, i, 0)),
            pl.BlockSpec((None, ATTN_KV_HEADS, None, chunk // LANES, HEAD_DIM, LANES),
                         lambda bi, i: (bi, 0, i // r, i % r, 0, 0)),
            pl.BlockSpec((None, chunk, CONV_WIDTH), lambda bi, i: (bi, i, 0)),
            pl.BlockSpec((None, chunk, CONV_WIDTH), lambda bi, i: (bi, i, 0)),
        ],
        out_shape=[
            jax.ShapeDtypeStruct((b, ATTN_HEADS, s, HEAD_DIM), BF16),
            jax.ShapeDtypeStruct((b, ATTN_KV_HEADS, 1, s, LANES), BF16),
            jax.ShapeDtypeStruct((b, ATTN_KV_HEADS, s // kv_chunk, kv_chunk // LANES, HEAD_DIM, LANES),
                                 BF16),
            jax.ShapeDtypeStruct((b, s, CONV_WIDTH), BF16),
            jax.ShapeDtypeStruct((b, s, CONV_WIDTH), BF16),
        ],
        compiler_params=_params(("parallel", "parallel")),
        name="proj0",
    )(h, g, w, qg, kg, cos, sin)


def _proj1_kernel(h_ref, g_ref, wd_ref, qg_ref, kvg_ref, wuq_ref, wukv_ref, cos_ref, sin_ref,
                  q_ref, k_ref, vt_ref):
    xn = _rms(h_ref[...], g_ref[...]).astype(BF16)
    lat = jnp.dot(xn, wd_ref[...], preferred_element_type=F32)
    cq = _rms(lat[:, :Q_LORA], qg_ref[...]).astype(BF16)
    ckv = _rms(lat[:, Q_LORA:Q_LORA + KV_LORA], kvg_ref[...]).astype(BF16)
    cos = cos_ref[...]
    sin = sin_ref[...]
    kr = _rope(lat[:, Q_LORA + KV_LORA:], cos, sin).astype(BF16)
    q = jnp.dot(cq, wuq_ref[...], preferred_element_type=F32)
    kv = jnp.dot(ckv, wukv_ref[...], preferred_element_type=F32)
    scale = (MLA_NOPE + MLA_ROPE) ** -0.5 * LOG2_E
    for hh in range(MLA_HEADS):
        o = hh * MLA_QK_PAD
        q_ref[hh, :, :MLA_NOPE] = (q[:, o:o + MLA_NOPE] * scale).astype(BF16)
        qr = _rope(q[:, o + MLA_NOPE:o + MLA_QK_PAD], cos, sin)
        q_ref[hh, :, MLA_NOPE:] = (qr * scale).astype(BF16)
        o = hh * (MLA_NOPE + MLA_V)
        k_ref[hh, 0] = kv[:, o:o + MLA_NOPE].astype(BF16)
        k_ref[hh, 1] = kr
        _store_vt(vt_ref, hh, kv[:, o + MLA_NOPE:o + MLA_NOPE + MLA_V])


def _proj1(h, g, wd, qg, kvg, wuq, wukv, cos, sin, chunk, kv_chunk):
    b, s, d = h.shape
    nc = s // chunk
    r = kv_chunk // chunk
    const = lambda bi, i: (0, 0)
    return pl.pallas_call(
        _proj1_kernel,
        grid=(b, nc),
        in_specs=[
            pl.BlockSpec((None, chunk, d), lambda bi, i: (bi, i, 0)),
            pl.BlockSpec((1, d), const),
            pl.BlockSpec(wd.shape, const),
            pl.BlockSpec((1, Q_LORA), const),
            pl.BlockSpec((1, KV_LORA), const),
            pl.BlockSpec(wuq.shape, const),
            pl.BlockSpec(wukv.shape, const),
            pl.BlockSpec((chunk, LANES), lambda bi, i: (i, 0)),
            pl.BlockSpec((chunk, LANES), lambda bi, i: (i, 0)),
        ],
        out_specs=[
            pl.BlockSpec((None, MLA_HEADS, chunk, MLA_QK_PAD), lambda bi, i: (bi, 0, i, 0)),
            pl.BlockSpec((None, MLA_HEADS, MLA_QK_PAD // LANES, chunk, LANES),
                         lambda bi, i: (bi, 0, 0, i, 0)),
            pl.BlockSpec((None, MLA_HEADS, None, chunk // LANES, MLA_V, LANES),
                         lambda bi, i: (bi, 0, i // r, i % r, 0, 0)),
        ],
        out_shape=[
            jax.ShapeDtypeStruct((b, MLA_HEADS, s, MLA_QK_PAD), BF16),
            jax.ShapeDtypeStruct((b, MLA_HEADS, MLA_QK_PAD // LANES, s, LANES), BF16),
            jax.ShapeDtypeStruct((b, MLA_HEADS, s // kv_chunk, kv_chunk // LANES, MLA_V, LANES), BF16),
        ],
        compiler_params=_params(("parallel", "parallel")),
        name="proj1",
    )(h, g, wd, qg, kvg, wuq, wukv, cos, sin)


def _attn_kernel(q_ref, k_ref, vt_ref, o_ref, m_sc, acc_sc, s_sc, mx_sc, qt_sc, *, group, bq,
                 chunk, n_chunks, unroll):
    n = group * bq
    gw = s_sc.shape[2]
    n_groups = n // gw
    ring = s_sc.shape[0]
    m_sc[...] = jnp.full(m_sc.shape, NEG_BIG, F32)
    acc_sc[...] = jnp.zeros(acc_sc.shape, F32)
    for gi in range(n_groups):
        qg = q_ref[(gi * gw) // bq, pl.ds((gi * gw) % bq, gw), :]
        qt_sc[:, gi * gw:(gi + 1) * gw] = qg.T

    ones_rows = (lax.broadcasted_iota(jnp.int32, (BF16_SUBLANES, chunk), 0) == 0).astype(BF16)

    items = [(j, gi) for j in range(unroll) for gi in range(n_groups)]
    assert len(items) % ring == 0
    lookahead = min(ATTN_LOOKAHEAD, ring - 1)

    def scores(c, gi, slot):
        rows = pl.ds(pl.multiple_of(c * chunk, chunk), chunk)
        kc = jnp.concatenate([k_ref[i, rows, :] for i in range(k_ref.shape[0])], axis=1)
        s = jnp.dot(kc, qt_sc[:, gi * gw:(gi + 1) * gw], preferred_element_type=F32)
        s_sc[slot] = s
        mx_sc[slot] = jnp.max(s, axis=0, keepdims=True)

    for idx in range(lookahead):
        scores(items[idx][0], items[idx][1], idx)

    def body(t, carry):
        for idx, (j, gi) in enumerate(items):
            ahead = idx + lookahead
            ja, ga = items[ahead % len(items)]
            ca = jnp.minimum((t + ahead // len(items)) * unroll + ja, n_chunks - 1)
            scores(ca, ga, ahead % ring)

            cols = slice(gi * gw, (gi + 1) * gw)
            slot = idx % ring
            m_prev = m_sc[:, cols]
            m_new = jnp.maximum(m_prev, mx_sc[slot])
            alpha = jnp.exp2(m_prev - m_new)
            p = jnp.exp2(s_sc[slot] - m_new)
            vt = jnp.concatenate([vt_ref[t * unroll + j, i] for i in range(vt_ref.shape[1])], axis=1)
            vt = jnp.concatenate([vt, ones_rows], axis=0)
            pv = jnp.dot(vt, p.astype(BF16), preferred_element_type=F32)
            acc_sc[:, cols] = alpha * acc_sc[:, cols] + pv
            m_sc[:, cols] = m_new
        return carry

    lax.fori_loop(0, n_chunks // unroll, body, 0)
    dv = acc_sc.shape[0] - BF16_SUBLANES
    o = acc_sc[:dv, :] / acc_sc[dv:dv + 1, :]
    for gi in range(group):
        o_ref[:, gi * dv:(gi + 1) * dv] = o[:, gi * bq:(gi + 1) * bq].T.astype(o_ref.dtype)


def _attention(q, k, vt, *, group, bq, name):
    b, h, s, dqk = q.shape
    hkv = k.shape[1]
    n_chunks, slabs, dv, _ = vt.shape[2:]
    chunk = slabs * LANES
    n = group * bq
    gw = min(ATTN_COL_GROUP, bq)
    unroll = max(1, ATTN_TRIP_ITEMS // (n // gw))
    while n_chunks % unroll:
        unroll -= 1
    items = unroll * (n // gw)
    ring = ATTN_RING if items % ATTN_RING == 0 else items
    kern = functools.partial(_attn_kernel, group=group, bq=bq, chunk=chunk, n_chunks=n_chunks,
                             unroll=unroll)
    return pl.pallas_call(
        kern,
        grid=(b, hkv, s // bq),
        in_specs=[
            pl.BlockSpec((None, group, bq, dqk), lambda bi, hi, qi: (bi, hi, qi, 0)),
            pl.BlockSpec((None, None, dqk // LANES, s, LANES), lambda bi, hi, qi: (bi, hi, 0, 0, 0)),
            pl.BlockSpec((None, None, n_chunks, slabs, dv, LANES),
                         lambda bi, hi, qi: (bi, hi, 0, 0, 0, 0)),
        ],
        out_specs=pl.BlockSpec((None, bq, group * dv), lambda bi, hi, qi: (bi, qi, hi)),
        out_shape=jax.ShapeDtypeStruct((b, s, h * dv), BF16),
        scratch_shapes=[
            pltpu.VMEM((1, n), F32),
            pltpu.VMEM((dv + BF16_SUBLANES, n), F32),
            pltpu.VMEM((ring, chunk, gw), F32),
            pltpu.VMEM((ring, 1, gw), F32),
            pltpu.VMEM((dqk, n), BF16),
        ],
        compiler_params=_params(("parallel", "parallel", "arbitrary")),
        name=name,
    )(q, k, vt)


def _cast_weight_slices(wup_ref, wdn_ref, wup_o_ref, wdn_o_ref):
    wup_o_ref[...] = wup_ref[...].astype(BF16)
    wdn_o_ref[...] = wdn_ref[...].astype(BF16)


def _weight_cast_specs(mlp_w, layer, steps, step_index):
    in_specs, out_specs, out_shapes = [], [], []
    for w in mlp_w:
        rows, cols = w.shape[1:]
        assert rows % (steps * BF16_SUBLANES) == 0
        blk = rows // steps
        in_specs.append(pl.BlockSpec((None, blk, cols),
                                     lambda *g: (layer, step_index(*g), 0)))
        out_specs.append(pl.BlockSpec((blk, cols), lambda *g: (step_index(*g), 0)))
        out_shapes.append(jax.ShapeDtypeStruct((rows, cols), BF16))
    return in_specs, out_specs, out_shapes


def _out0_kernel(h_ref, a_ref, gb_ref, cin_ref, cprev_ref, cnext_ref, cw_ref, w_ref, wup_ref, wdn_ref,
                 o_ref, wup_o_ref, wdn_o_ref):
    _cast_weight_slices(wup_ref, wdn_ref, wup_o_ref, wdn_o_ref)
    i = pl.program_id(1)
    bm = cin_ref.shape[0]
    c = cin_ref[...].astype(F32)
    prev_row = cprev_ref[BF16_SUBLANES - 1:BF16_SUBLANES, :].astype(F32)
    next_row = cnext_ref[0:1, :].astype(F32)
    prev_row = jnp.where(i == 0, 0.0, prev_row)
    next_row = jnp.where(i == pl.num_programs(1) - 1, 0.0, next_row)
    rows = lax.broadcasted_iota(jnp.int32, (bm, 1), 0)
    c_m1 = jnp.where(rows == 0, prev_row, pltpu.roll(c, 1, 0))
    c_p1 = jnp.where(rows == bm - 1, next_row, pltpu.roll(c, bm - 1, 0))
    conv = cw_ref[0:1, :] * c_m1 + cw_ref[1:2, :] * c + cw_ref[2:3, :] * c_p1
    sconv = (gb_ref[...].astype(F32) * conv).astype(BF16)
    aw = a_ref.shape[1]
    y = jnp.dot(a_ref[...], w_ref[:aw, :], preferred_element_type=F32)
    y = y + jnp.dot(sconv, w_ref[aw:, :], preferred_element_type=F32)
    o_ref[...] = h_ref[...] + y


def _out0(h, attn, gb, cin, conv_w, w, mlp_w, layer, bm):
    b, s, d = h.shape
    nb = s // bm
    w_specs, w_out_specs, w_out_shapes = _weight_cast_specs(mlp_w, layer, b * nb,
                                                            lambda bi, i: bi * nb + i)
    r = bm // BF16_SUBLANES
    last = s // BF16_SUBLANES - 1
    tile = lambda width: pl.BlockSpec((None, bm, width), lambda bi, i: (bi, i, 0))
    return pl.pallas_call(
        _out0_kernel,
        grid=(b, nb),
        in_specs=[
            tile(d), tile(attn.shape[2]), tile(CONV_WIDTH), tile(CONV_WIDTH),
            pl.BlockSpec((None, BF16_SUBLANES, CONV_WIDTH),
                         lambda bi, i: (bi, jnp.maximum(i * r - 1, 0), 0)),
            pl.BlockSpec((None, BF16_SUBLANES, CONV_WIDTH),
                         lambda bi, i: (bi, jnp.minimum((i + 1) * r, last), 0)),
            pl.BlockSpec(conv_w.shape, lambda bi, i: (0, 0)),
            pl.BlockSpec(w.shape, lambda bi, i: (0, 0)),
        ] + w_specs,
        out_specs=[tile(d)] + w_out_specs,
        out_shape=[jax.ShapeDtypeStruct(h.shape, F32)] + w_out_shapes,
        compiler_params=_params(("parallel", "parallel")),
        name="out0",
    )(h, attn, gb, cin, cin, cin, conv_w, w, *mlp_w)


def _out1_kernel(h_ref, a_ref, w_ref, wup_ref, wdn_ref, o_ref, wup_o_ref, wdn_o_ref):
    _cast_weight_slices(wup_ref, wdn_ref, wup_o_ref, wdn_o_ref)
    o_ref[...] = h_ref[...] + jnp.dot(a_ref[...], w_ref[...], preferred_element_type=F32)


def _out1(h, attn, w, mlp_w, layer, bm):
    t, d = h.shape
    w_specs, w_out_specs, w_out_shapes = _weight_cast_specs(mlp_w, layer, t // bm, lambda i: i)
    return pl.pallas_call(
        _out1_kernel,
        grid=(t // bm,),
        in_specs=[
            pl.BlockSpec((bm, d), lambda i: (i, 0)),
            pl.BlockSpec((bm, attn.shape[1]), lambda i: (i, 0)),
            pl.BlockSpec(w.shape, lambda i: (0, 0)),
        ] + w_specs,
        out_specs=[pl.BlockSpec((bm, d), lambda i: (i, 0))] + w_out_specs,
        out_shape=[jax.ShapeDtypeStruct(h.shape, F32)] + w_out_shapes,
        compiler_params=_params(("parallel",)),
        name="out1",
    )(h, attn, w, *mlp_w)


def _mlp_kernel(h_ref, g_ref, wup_ref, wdn_ref, fg_ref, o_ref, xn_sc, *, final_norm):
    f = pl.program_id(1)

    def mlp_chunk(xn):
        u = jnp.dot(xn, wup_ref[...], preferred_element_type=F32)
        a = jnp.square(jnp.maximum(u, 0.0)).astype(BF16)
        return jnp.dot(a, wdn_ref[...], preferred_element_type=F32)

    @pl.when(f == 0)
    def _():
        x = h_ref[...]
        xn = _rms(x, g_ref[...]).astype(BF16)
        xn_sc[...] = xn
        o_ref[...] = x + mlp_chunk(xn)

    @pl.when(f > 0)
    def _():
        o_ref[...] += mlp_chunk(xn_sc[...])

    if final_norm:
        @pl.when(f == pl.num_programs(1) - 1)
        def _():
            o_ref[...] = _rms(o_ref[...], fg_ref[...])


def _mlp(h, g, wup, wdn, fg, *, bm, bf, final_norm):
    t, d = h.shape
    dff = wup.shape[1]
    return pl.pallas_call(
        functools.partial(_mlp_kernel, final_norm=final_norm),
        grid=(t // bm, dff // bf),
        in_specs=[
            pl.BlockSpec((bm, d), lambda i, f: (i, 0)),
            pl.BlockSpec((1, d), lambda i, f: (0, 0)),
            pl.BlockSpec((d, bf), lambda i, f: (0, f)),
            pl.BlockSpec((bf, d), lambda i, f: (f, 0)),
            pl.BlockSpec((1, d), lambda i, f: (0, 0)),
        ],
        out_specs=pl.BlockSpec((bm, d), lambda i, f: (i, 0)),
        out_shape=jax.ShapeDtypeStruct(h.shape, F32),
        scratch_shapes=[pltpu.VMEM((bm, d), BF16)],
        compiler_params=_params(("parallel", "arbitrary")),
        name="mlp_final" if final_norm else "mlp",
    )(h, g, wup, wdn, fg)


def _rope_cos_sin(seq, rot_dim):
    rows = seq // GRID_W
    axis_dim = rot_dim // 2
    inv_freq = ROPE_THETA ** (-jnp.arange(0, axis_dim, 2, dtype=F32) / axis_dim)
    ang_row = jnp.arange(rows, dtype=F32)[:, None] * inv_freq
    ang_col = jnp.arange(GRID_W, dtype=F32)[:, None] * inv_freq
    shape = (rows, GRID_W, inv_freq.shape[0])

    def per_token(fn):
        by_row = jnp.broadcast_to(fn(ang_row)[:, None, :], shape)
        by_col = jnp.broadcast_to(fn(ang_col)[None, :, :], shape)
        return jnp.concatenate([by_row, by_col], axis=-1).reshape(seq, rot_dim // 2)

    return per_token(jnp.cos), per_token(jnp.sin)


def _rope_tables(seq, rot_dim):
    cos, sin = _rope_cos_sin(seq, rot_dim)
    pad = jnp.zeros((seq, LANES // 2 - rot_dim // 2), F32)
    cos_t = jnp.concatenate([cos, pad, cos, pad], axis=-1)
    sin_t = jnp.concatenate([-sin, pad, sin, pad], axis=-1)
    return cos_t, sin_t


def _pair_split(w):
    return w[..., 0::2], w[..., 1::2]


def _pair_split_sources(rot_dim, pad):
    blank = [-1] * pad
    return list(range(0, rot_dim, 2)) + blank + list(range(1, rot_dim, 2)) + blank


_PAIR_SPLIT_128 = _pair_split_sources(HEAD_DIM, 0)
_ROPE_PAD_64 = _pair_split_sources(MLA_ROPE, LANES // 2 - MLA_ROPE // 2)
_MLA_Q_HEAD = list(range(MLA_NOPE)) + [MLA_NOPE + c if c >= 0 else -1 for c in _ROPE_PAD_64]


def _relayout(w, sources):
    sel = [[1.0 if src == k else 0.0 for src in sources] for k in range(w.shape[-1])]
    return jnp.dot(w, jnp.array(sel, BF16), preferred_element_type=BF16)


def kernel(x, even_norm_g, even_w_in, even_q_norm_g, even_k_norm_g, even_conv_w, even_w_out,
           odd_norm_g, odd_w_down, odd_q_lat_g, odd_kv_lat_g, odd_w_uq, odd_w_ukv, odd_w_o,
           mlp_norm_g, mlp_w_up, mlp_w_down, final_norm_g):
    b, s, d = x.shape
    t = _tiles(s)
    depth = mlp_norm_g.shape[0]
    cos_a, sin_a = _rope_tables(s, HEAD_DIM)
    cos_c, sin_c = _rope_tables(s, MLA_ROPE)
    fg = final_norm_g.reshape(1, d)
    mlp_w = (mlp_w_up, mlp_w_down)

    h = x
    for layer in range(depth):
        i = layer // 2
        if layer % 2 == 0:
            w_in = even_w_in[i].astype(BF16)
            wq = w_in[:, :ATTN_WIDTH].reshape(d, ATTN_HEADS, HEAD_DIM)
            wk = w_in[:, ATTN_WIDTH:ATTN_WIDTH + KV_WIDTH].reshape(d, ATTN_KV_HEADS, HEAD_DIM)
            wq = _relayout(wq, _PAIR_SPLIT_128).reshape(d, ATTN_WIDTH)
            wk = _relayout(wk, _PAIR_SPLIT_128).reshape(d, KV_WIDTH)
            w0 = jnp.concatenate([wq, wk, w_in[:, ATTN_WIDTH + KV_WIDTH:]], axis=-1)
            qg = jnp.concatenate(_pair_split(even_q_norm_g[i]), axis=-1).reshape(1, HEAD_DIM)
            kg = jnp.concatenate(_pair_split(even_k_norm_g[i]), axis=-1).reshape(1, HEAD_DIM)
            q, k, vt, gb, cin = _proj0(h, even_norm_g[i].reshape(1, d), w0, qg, kg,
                                       cos_a, sin_a, t["chunk"], t["kv_chunk"])
            attn = _attention(q, k, vt, group=ATTN_GROUP, bq=t["bq_gqa"], name="gqa_attn")
            h, w_up, w_dn = _out0(h, attn, gb, cin, even_conv_w[i], even_w_out[i].astype(BF16),
                                  mlp_w, layer, t["bm_out"])
        else:
            w_down = odd_w_down[i].astype(BF16)
            wd = jnp.concatenate([w_down[:, :Q_LORA + KV_LORA],
                                  _relayout(w_down[:, Q_LORA + KV_LORA:], _ROPE_PAD_64)], axis=-1)
            wuq = odd_w_uq[i].astype(BF16).reshape(Q_LORA, MLA_HEADS, MLA_NOPE + MLA_ROPE)
            wuq = _relayout(wuq, _MLA_Q_HEAD).reshape(Q_LORA, MLA_HEADS * MLA_QK_PAD)
            q, k, vt = _proj1(h, odd_norm_g[i].reshape(1, d), wd,
                              odd_q_lat_g[i].reshape(1, Q_LORA), odd_kv_lat_g[i].reshape(1, KV_LORA),
                              wuq, odd_w_ukv[i].astype(BF16), cos_c, sin_c, t["chunk"],
                              t["kv_chunk"])
            attn = _attention(q, k, vt, group=1, bq=t["bq_mla"], name="mla_attn")
            h, w_up, w_dn = _out1(h.reshape(b * s, d), attn.reshape(b * s, -1),
                                  odd_w_o[i].astype(BF16), mlp_w, layer, t["bm_out"])
        h = _mlp(h.reshape(b * s, d), mlp_norm_g[layer].reshape(1, d), w_up, w_dn, fg,
                 bm=t["bm_mlp"], bf=t["bf_mlp"],
                 final_norm=(layer == depth - 1)).reshape(b, s, d)
    return h
```

```python
import functools

import jax
import jax.numpy as jnp
from jax import lax
from jax.experimental import pallas as pl
from jax.experimental.pallas import tpu as pltpu

F32 = jnp.float32
BF16 = jnp.bfloat16

NORM_EPS = 1e-6
ROPE_THETA = 10000.0
GRID_W = 64

HEAD_DIM = 128
ATTN_HEADS = 8
ATTN_KV_HEADS = 2
ATTN_GROUP = ATTN_HEADS // ATTN_KV_HEADS
ATTN_WIDTH = ATTN_HEADS * HEAD_DIM
KV_WIDTH = ATTN_KV_HEADS * HEAD_DIM
CONV_WIDTH = 1024

MLA_HEADS = 16
MLA_NOPE = 128
MLA_ROPE = 64
MLA_V = 128
Q_LORA = 512
KV_LORA = 512
MLA_QK_PAD = 256

LANES = 128
BF16_SUBLANES = 16
VMEM_LIMIT_BYTES = 56 * 1024 * 1024
NEG_BIG = -1e30
LOG2_E = 1.4426950408889634
ATTN_COL_GROUP = 512
ATTN_TRIP_ITEMS = 128
ATTN_LOOKAHEAD = 2
ATTN_RING = 4


def _tiles(seq):
    return dict(
        chunk=min(512, seq),
        kv_chunk=min(512, seq),
        bm_mlp=min(1024, seq),
        bf_mlp=512,
        bm_out=min(512, seq),
        bq_gqa=min(1024, seq),
        bq_mla=min(8192, seq),
    )


def _params(sem):
    return pltpu.CompilerParams(dimension_semantics=sem, vmem_limit_bytes=VMEM_LIMIT_BYTES)


def _rms(x, g):
    return x * lax.rsqrt(jnp.mean(x * x, axis=-1, keepdims=True) + NORM_EPS) * g


def _store_vt(vt_ref, head, v):
    vt = v.T.astype(BF16)
    for i in range(vt_ref.shape[1]):
        vt_ref[head, i] = vt[:, i * LANES:(i + 1) * LANES]


def _rope(x, cos, sin):
    return x * cos + pltpu.roll(x, LANES // 2, 1) * sin


def _proj0_kernel(h_ref, g_ref, w_ref, qg_ref, kg_ref, cos_ref, sin_ref,
                  q_ref, k_ref, vt_ref, gb_ref, cin_ref):
    xn = _rms(h_ref[...], g_ref[...]).astype(BF16)
    y = jnp.dot(xn, w_ref[...], preferred_element_type=F32)
    cos = cos_ref[...]
    sin = sin_ref[...]
    scale = HEAD_DIM ** -0.5 * LOG2_E
    for hh in range(ATTN_HEADS):
        yh = _rms(y[:, hh * HEAD_DIM:(hh + 1) * HEAD_DIM], qg_ref[...])
        q_ref[hh] = (_rope(yh, cos, sin) * scale).astype(BF16)
    o = ATTN_WIDTH
    for hh in range(ATTN_KV_HEADS):
        yh = _rms(y[:, o + hh * HEAD_DIM:o + (hh + 1) * HEAD_DIM], kg_ref[...])
        k_ref[hh, 0] = _rope(yh, cos, sin).astype(BF16)
    o += KV_WIDTH
    for hh in range(ATTN_KV_HEADS):
        _store_vt(vt_ref, hh, y[:, o + hh * HEAD_DIM:o + (hh + 1) * HEAD_DIM])
    o += KV_WIDTH
    gb_ref[...] = y[:, o:o + CONV_WIDTH].astype(BF16)
    o += CONV_WIDTH
    cin_ref[...] = (y[:, o:o + CONV_WIDTH] * y[:, o + CONV_WIDTH:o + 2 * CONV_WIDTH]).astype(BF16)


def _proj0(h, g, w, qg, kg, cos, sin, chunk, kv_chunk):
    b, s, d = h.shape
    n = w.shape[1]
    nc = s // chunk
    r = kv_chunk // chunk
    return pl.pallas_call(
        _proj0_kernel,
        grid=(b, nc),
        in_specs=[
            pl.BlockSpec((None, chunk, d), lambda bi, i: (bi, i, 0)),
            pl.BlockSpec((1, d), lambda bi, i: (0, 0)),
            pl.BlockSpec((d, n), lambda bi, i: (0, 0)),
            pl.BlockSpec((1, HEAD_DIM), lambda bi, i: (0, 0)),
            pl.BlockSpec((1, HEAD_DIM), lambda bi, i: (0, 0)),
            pl.BlockSpec((chunk, HEAD_DIM), lambda bi, i: (i, 0)),
            pl.BlockSpec((chunk, HEAD_DIM), lambda bi, i: (i, 0)),
        ],
        out_specs=[
            pl.BlockSpec((None, ATTN_HEADS, chunk, HEAD_DIM), lambda bi, i: (bi, 0, i, 0)),
            pl.BlockSpec((None, ATTN_KV_HEADS, 1, chunk, LANES), lambda bi, i: (bi, 0, 0, i, 0)),
            pl.BlockSpec((None, ATTN_KV_HEADS, None, chunk // LANES, HEAD_DIM, LANES),
                         lambda bi, i: (bi, 0, i // r, i % r, 0, 0)),
            pl.BlockSpec((None, chunk, CONV_WIDTH), lambda bi, i: (bi, i, 0)),
            pl.BlockSpec((None, chunk, CONV_WIDTH), lambda bi, i: (bi, i, 0)),
        ],
        out_shape=[
            jax.ShapeDtypeStruct((b, ATTN_HEADS, s, HEAD_DIM), BF16),
            jax.ShapeDtypeStruct((b, ATTN_KV_HEADS, 1, s, LANES), BF16),
            jax.ShapeDtypeStruct((b, ATTN_KV_HEADS, s // kv_chunk, kv_chunk // LANES, HEAD_DIM, LANES),
                                 BF16),
            jax.ShapeDtypeStruct((b, s, CONV_WIDTH), BF16),
            jax.ShapeDtypeStruct((b, s, CONV_WIDTH), BF16),
        ],
        compiler_params=_params(("parallel", "parallel")),
        name="proj0",
    )(h, g, w, qg, kg, cos, sin)


def _proj1_kernel(h_ref, g_ref, wd_ref, qg_ref, kvg_ref, wuq_ref, wukv_ref, cos_ref, sin_ref,
                  q_ref, k_ref, vt_ref):
    xn = _rms(h_ref[...], g_ref[...]).astype(BF16)
    lat = jnp.dot(xn, wd_ref[...], preferred_element_type=F32)
    cq = _rms(lat[:, :Q_LORA], qg_ref[...]).astype(BF16)
    ckv = _rms(lat[:, Q_LORA:Q_LORA + KV_LORA], kvg_ref[...]).astype(BF16)
    cos = cos_ref[...]
    sin = sin_ref[...]
    kr = _rope(lat[:, Q_LORA + KV_LORA:], cos, sin).astype(BF16)
    q = jnp.dot(cq, wuq_ref[...], preferred_element_type=F32)
    kv = jnp.dot(ckv, wukv_ref[...], preferred_element_type=F32)
    scale = (MLA_NOPE + MLA_ROPE) ** -0.5 * LOG2_E
    for hh in range(MLA_HEADS):
        o = hh * MLA_QK_PAD
        q_ref[hh, :, :MLA_NOPE] = (q[:, o:o + MLA_NOPE] * scale).astype(BF16)
        qr = _rope(q[:, o + MLA_NOPE:o + MLA_QK_PAD], cos, sin)
        q_ref[hh, :, MLA_NOPE:] = (qr * scale).astype(BF16)
        o = hh * (MLA_NOPE + MLA_V)
        k_ref[hh, 0] = kv[:, o:o + MLA_NOPE].astype(BF16)
        k_ref[hh, 1] = kr
        _store_vt(vt_ref, hh, kv[:, o + MLA_NOPE:o + MLA_NOPE + MLA_V])


def _proj1(h, g, wd, qg, kvg, wuq, wukv, cos, sin, chunk, kv_chunk):
    b, s, d = h.shape
    nc = s // chunk
    r = kv_chunk // chunk
    const = lambda bi, i: (0, 0)
    return pl.pallas_call(
        _proj1_kernel,
        grid=(b, nc),
        in_specs=[
            pl.BlockSpec((None, chunk, d), lambda bi, i: (bi, i, 0)),
            pl.BlockSpec((1, d), const),
            pl.BlockSpec(wd.shape, const),
            pl.BlockSpec((1, Q_LORA), const),
            pl.BlockSpec((1, KV_LORA), const),
            pl.BlockSpec(wuq.shape, const),
            pl.BlockSpec(wukv.shape, const),
            pl.BlockSpec((chunk, LANES), lambda bi, i: (i, 0)),
            pl.BlockSpec((chunk, LANES), lambda bi, i: (i, 0)),
        ],
        out_specs=[
            pl.BlockSpec((None, MLA_HEADS, chunk, MLA_QK_PAD), lambda bi, i: (bi, 0, i, 0)),
            pl.BlockSpec((None, MLA_HEADS, MLA_QK_PAD // LANES, chunk, LANES),
                         lambda bi, i: (bi, 0, 0, i, 0)),
            pl.BlockSpec((None, MLA_HEADS, None, chunk // LANES, MLA_V, LANES),
                         lambda bi, i: (bi, 0, i // r, i % r, 0, 0)),
        ],
        out_shape=[
            jax.ShapeDtypeStruct((b, MLA_HEADS, s, MLA_QK_PAD), BF16),
            jax.ShapeDtypeStruct((b, MLA_HEADS, MLA_QK_PAD // LANES, s, LANES), BF16),
            jax.ShapeDtypeStruct((b, MLA_HEADS, s // kv_chunk, kv_chunk // LANES, MLA_V, LANES), BF16),
        ],
        compiler_params=_params(("parallel", "parallel")),
        name="proj1",
    )(h, g, wd, qg, kvg, wuq, wukv, cos, sin)


def _attn_kernel(q_ref, k_ref, vt_ref, o_ref, m_sc, acc_sc, s_sc, mx_sc, qt_sc, *, group, bq,
                 chunk, n_chunks, unroll):
    n = group * bq
    gw = s_sc.shape[2]
    n_groups = n // gw
    ring = s_sc.shape[0]
    m_sc[...] = jnp.full(m_sc.shape, NEG_BIG, F32)
    acc_sc[...] = jnp.zeros(acc_sc.shape, F32)
    for gi in range(n_groups):
        qg = q_ref[(gi * gw) // bq, pl.ds((gi * gw) % bq, gw), :]
        qt_sc[:, gi * gw:(gi + 1) * gw] = qg.T

    ones_rows = (lax.broadcasted_iota(jnp.int32, (BF16_SUBLANES, chunk), 0) == 0).astype(BF16)

    items = [(j, gi) for j in range(unroll) for gi in range(n_groups)]
    assert len(items) % ring == 0
    lookahead = min(ATTN_LOOKAHEAD, ring - 1)

    def scores(c, gi, slot):
        rows = pl.ds(pl.multiple_of(c * chunk, chunk), chunk)
        kc = jnp.concatenate([k_ref[i, rows, :] for i in range(k_ref.shape[0])], axis=1)
        s = jnp.dot(kc, qt_sc[:, gi * gw:(gi + 1) * gw], preferred_element_type=F32)
        s_sc[slot] = s
        mx_sc[slot] = jnp.max(s, axis=0, keepdims=True)

    for idx in range(lookahead):
        scores(items[idx][0], items[idx][1], idx)

    def body(t, carry):
        for idx, (j, gi) in enumerate(items):
            ahead = idx + lookahead
            ja, ga = items[ahead % len(items)]
            ca = jnp.minimum((t + ahead // len(items)) * unroll + ja, n_chunks - 1)
            scores(ca, ga, ahead % ring)

            cols = slice(gi * gw, (gi + 1) * gw)
            slot = idx % ring
            m_prev = m_sc[:, cols]
            m_new = jnp.maximum(m_prev, mx_sc[slot])
            alpha = jnp.exp2(m_prev - m_new)
            p = jnp.exp2(s_sc[slot] - m_new)
            vt = jnp.concatenate([vt_ref[t * unroll + j, i] for i in range(vt_ref.shape[1])], axis=1)
            vt = jnp.concatenate([vt, ones_rows], axis=0)
            pv = jnp.dot(vt, p.astype(BF16), preferred_element_type=F32)
            acc_sc[:, cols] = alpha * acc_sc[:, cols] + pv
            m_sc[:, cols] = m_new
        return carry

    lax.fori_loop(0, n_chunks // unroll, body, 0)
    dv = acc_sc.shape[0] - BF16_SUBLANES
    o = acc_sc[:dv, :] / acc_sc[dv:dv + 1, :]
    for gi in range(group):
        o_ref[:, gi * dv:(gi + 1) * dv] = o[:, gi * bq:(gi + 1) * bq].T.astype(o_ref.dtype)


def _attention(q, k, vt, *, group, bq, name):
    b, h, s, dqk = q.shape
    hkv = k.shape[1]
    n_chunks, slabs, dv, _ = vt.shape[2:]
    chunk = slabs * LANES
    n = group * bq
    gw = min(ATTN_COL_GROUP, bq)
    unroll = max(1, ATTN_TRIP_ITEMS // (n // gw))
    while n_chunks % unroll:
        unroll -= 1
    items = unroll * (n // gw)
    ring = ATTN_RING if items % ATTN_RING == 0 else items
    kern = functools.partial(_attn_kernel, group=group, bq=bq, chunk=chunk, n_chunks=n_chunks,
                             unroll=unroll)
    return pl.pallas_call(
        kern,
        grid=(b, hkv, s // bq),
        in_specs=[
            pl.BlockSpec((None, group, bq, dqk), lambda bi, hi, qi: (bi, hi, qi, 0)),
            pl.BlockSpec((None, None, dqk // LANES, s, LANES), lambda bi, hi, qi: (bi, hi, 0, 0, 0)),
            pl.BlockSpec((None, None, n_chunks, slabs, dv, LANES),
                         lambda bi, hi, qi: (bi, hi, 0, 0, 0, 0)),
        ],
        out_specs=pl.BlockSpec((None, bq, group * dv), lambda bi, hi, qi: (bi, qi, hi)),
        out_shape=jax.ShapeDtypeStruct((b, s, h * dv), BF16),
        scratch_shapes=[
            pltpu.VMEM((1, n), F32),
            pltpu.VMEM((dv + BF16_SUBLANES, n), F32),
            pltpu.VMEM((ring, chunk, gw), F32),
            pltpu.VMEM((ring, 1, gw), F32),
            pltpu.VMEM((dqk, n), BF16),
        ],
        compiler_params=_params(("parallel", "parallel", "arbitrary")),
        name=name,
    )(q, k, vt)


def _cast_weight_slices(wup_ref, wdn_ref, wup_o_ref, wdn_o_ref):
    wup_o_ref[...] = wup_ref[...].astype(BF16)
    wdn_o_ref[...] = wdn_ref[...].astype(BF16)


def _weight_cast_specs(mlp_w, layer, steps, step_index):
    in_specs, out_specs, out_shapes = [], [], []
    for w in mlp_w:
        rows, cols = w.shape[1:]
        assert rows % (steps * BF16_SUBLANES) == 0
        blk = rows // steps
        in_specs.append(pl.BlockSpec((None, blk, cols),
                                     lambda *g: (layer, step_index(*g), 0)))
        out_specs.append(pl.BlockSpec((blk, cols), lambda *g: (step_index(*g), 0)))
        out_shapes.append(jax.ShapeDtypeStruct((rows, cols), BF16))
    return in_specs, out_specs, out_shapes


def _out0_kernel(h_ref, a_ref, gb_ref, cin_ref, cprev_ref, cnext_ref, cw_ref, w_ref, wup_ref, wdn_ref,
                 o_ref, wup_o_ref, wdn_o_ref):
    _cast_weight_slices(wup_ref, wdn_ref, wup_o_ref, wdn_o_ref)
    i = pl.program_id(1)
    bm = cin_ref.shape[0]
    c = cin_ref[...].astype(F32)
    prev_row = cprev_ref[BF16_SUBLANES - 1:BF16_SUBLANES, :].astype(F32)
    next_row = cnext_ref[0:1, :].astype(F32)
    prev_row = jnp.where(i == 0, 0.0, prev_row)
    next_row = jnp.where(i == pl.num_programs(1) - 1, 0.0, next_row)
    rows = lax.broadcasted_iota(jnp.int32, (bm, 1), 0)
    c_m1 = jnp.where(rows == 0, prev_row, pltpu.roll(c, 1, 0))
    c_p1 = jnp.where(rows == bm - 1, next_row, pltpu.roll(c, bm - 1, 0))
    conv = cw_ref[0:1, :] * c_m1 + cw_ref[1:2, :] * c + cw_ref[2:3, :] * c_p1
    sconv = (gb_ref[...].astype(F32) * conv).astype(BF16)
    aw = a_ref.shape[1]
    y = jnp.dot(a_ref[...], w_ref[:aw, :], preferred_element_type=F32)
    y = y + jnp.dot(sconv, w_ref[aw:, :], preferred_element_type=F32)
    o_ref[...] = h_ref[...] + y


def _out0(h, attn, gb, cin, conv_w, w, mlp_w, layer, bm):
    b, s, d = h.shape
    nb = s // bm
    w_specs, w_out_specs, w_out_shapes = _weight_cast_specs(mlp_w, layer, b * nb,
                                                            lambda bi, i: bi * nb + i)
    r = bm // BF16_SUBLANES
    last = s // BF16_SUBLANES - 1
    tile = lambda width: pl.BlockSpec((None, bm, width), lambda bi, i: (bi, i, 0))
    return pl.pallas_call(
        _out0_kernel,
        grid=(b, nb),
        in_specs=[
            tile(d), tile(attn.shape[2]), tile(CONV_WIDTH), tile(CONV_WIDTH),
            pl.BlockSpec((None, BF16_SUBLANES, CONV_WIDTH),
                         lambda bi, i: (bi, jnp.maximum(i * r - 1, 0), 0)),
            pl.BlockSpec((None, BF16_SUBLANES, CONV_WIDTH),
                         lambda bi, i: (bi, jnp.minimum((i + 1) * r, last), 0)),
            pl.BlockSpec(conv_w.shape, lambda bi, i: (0, 0)),
            pl.BlockSpec(w.shape, lambda bi, i: (0, 0)),
        ] + w_specs,
        out_specs=[tile(d)] + w_out_specs,
        out_shape=[jax.ShapeDtypeStruct(h.shape, F32)] + w_out_shapes,
        compiler_params=_params(("parallel", "parallel")),
        name="out0",
    )(h, attn, gb, cin, cin, cin, conv_w, w, *mlp_w)


def _out1_kernel(h_ref, a_ref, w_ref, wup_ref, wdn_ref, o_ref, wup_o_ref, wdn_o_ref):
    _cast_weight_slices(wup_ref, wdn_ref, wup_o_ref, wdn_o_ref)
    o_ref[...] = h_ref[...] + jnp.dot(a_ref[...], w_ref[...], preferred_element_type=F32)


def _out1(h, attn, w, mlp_w, layer, bm):
    t, d = h.shape
    w_specs, w_out_specs, w_out_shapes = _weight_cast_specs(mlp_w, layer, t // bm, lambda i: i)
    return pl.pallas_call(
        _out1_kernel,
        grid=(t // bm,),
        in_specs=[
            pl.BlockSpec((bm, d), lambda i: (i, 0)),
            pl.BlockSpec((bm, attn.shape[1]), lambda i: (i, 0)),
            pl.BlockSpec(w.shape, lambda i: (0, 0)),
        ] + w_specs,
        out_specs=[pl.BlockSpec((bm, d), lambda i: (i, 0))] + w_out_specs,
        out_shape=[jax.ShapeDtypeStruct(h.shape, F32)] + w_out_shapes,
        compiler_params=_params(("parallel",)),
        name="out1",
    )(h, attn, w, *mlp_w)


def _mlp_kernel(h_ref, g_ref, wup_ref, wdn_ref, fg_ref, o_ref, xn_sc, *, final_norm):
    f = pl.program_id(1)

    def mlp_chunk(xn):
        u = jnp.dot(xn, wup_ref[...], preferred_element_type=F32)
        a = jnp.square(jnp.maximum(u, 0.0)).astype(BF16)
        return jnp.dot(a, wdn_ref[...], preferred_element_type=F32)

    @pl.when(f == 0)
    def _():
        x = h_ref[...]
        xn = _rms(x, g_ref[...]).astype(BF16)
        xn_sc[...] = xn
        o_ref[...] = x + mlp_chunk(xn)

    @pl.when(f > 0)
    def _():
        o_ref[...] += mlp_chunk(xn_sc[...])

    if final_norm:
        @pl.when(f == pl.num_programs(1) - 1)
        def _():
            o_ref[...] = _rms(o_ref[...], fg_ref[...])


def _mlp(h, g, wup, wdn, fg, *, bm, bf, final_norm):
    t, d = h.shape
    dff = wup.shape[1]
    return pl.pallas_call(
        functools.partial(_mlp_kernel, final_norm=final_norm),
        grid=(t // bm, dff // bf),
        in_specs=[
            pl.BlockSpec((bm, d), lambda i, f: (i, 0)),
            pl.BlockSpec((1, d), lambda i, f: (0, 0)),
            pl.BlockSpec((d, bf), lambda i, f: (0, f)),
            pl.BlockSpec((bf, d), lambda i, f: (f, 0)),
            pl.BlockSpec((1, d), lambda i, f: (0, 0)),
        ],
        out_specs=pl.BlockSpec((bm, d), lambda i, f: (i, 0)),
        out_shape=jax.ShapeDtypeStruct(h.shape, F32),
        scratch_shapes=[pltpu.VMEM((bm, d), BF16)],
        compiler_params=_params(("parallel", "arbitrary")),
        name="mlp_final" if final_norm else "mlp",
    )(h, g, wup, wdn, fg)


def _rope_cos_sin(seq, rot_dim):
    rows = seq // GRID_W
    axis_dim = rot_dim // 2
    inv_freq = ROPE_THETA ** (-jnp.arange(0, axis_dim, 2, dtype=F32) / axis_dim)
    ang_row = jnp.arange(rows, dtype=F32)[:, None] * inv_freq
    ang_col = jnp.arange(GRID_W, dtype=F32)[:, None] * inv_freq
    shape = (rows, GRID_W, inv_freq.shape[0])

    def per_token(fn):
        by_row = jnp.broadcast_to(fn(ang_row)[:, None, :], shape)
        by_col = jnp.broadcast_to(fn(ang_col)[None, :, :], shape)
        return jnp.concatenate([by_row, by_col], axis=-1).reshape(seq, rot_dim // 2)

    return per_token(jnp.cos), per_token(jnp.sin)


def _rope_tables(seq, rot_dim):
    cos, sin = _rope_cos_sin(seq, rot_dim)
    pad = jnp.zeros((seq, LANES // 2 - rot_dim // 2), F32)
    cos_t = jnp.concatenate([cos, pad, cos, pad], axis=-1)
    sin_t = jnp.concatenate([-sin, pad, sin, pad], axis=-1)
    return cos_t, sin_t


def _pair_split(w):
    return w[..., 0::2], w[..., 1::2]


def _pair_split_sources(rot_dim, pad):
    blank = [-1] * pad
    return list(range(0, rot_dim, 2)) + blank + list(range(1, rot_dim, 2)) + blank


_PAIR_SPLIT_128 = _pair_split_sources(HEAD_DIM, 0)
_ROPE_PAD_64 = _pair_split_sources(MLA_ROPE, LANES // 2 - MLA_ROPE // 2)
_MLA_Q_HEAD = list(range(MLA_NOPE)) + [MLA_NOPE + c if c >= 0 else -1 for c in _ROPE_PAD_64]


def _relayout(w, sources):
    sel = [[1.0 if src == k else 0.0 for src in sources] for k in range(w.shape[-1])]
    return jnp.dot(w, jnp.array(sel, BF16), preferred_element_type=BF16)


def kernel(x, even_norm_g, even_w_in, even_q_norm_g, even_k_norm_g, even_conv_w, even_w_out,
           odd_norm_g, odd_w_down, odd_q_lat_g, odd_kv_lat_g, odd_w_uq, odd_w_ukv, odd_w_o,
           mlp_norm_g, mlp_w_up, mlp_w_down, final_norm_g):
    b, s, d = x.shape
    t = _tiles(s)
    depth = mlp_norm_g.shape[0]
    cos_a, sin_a = _rope_tables(s, HEAD_DIM)
    cos_c, sin_c = _rope_tables(s, MLA_ROPE)
    fg = final_norm_g.reshape(1, d)
    mlp_w = (mlp_w_up, mlp_w_down)

    h = x
    for layer in range(depth):
        i = layer // 2
        if layer % 2 == 0:
            w_in = even_w_in[i].astype(BF16)
            wq = w_in[:, :ATTN_WIDTH].reshape(d, ATTN_HEADS, HEAD_DIM)
            wk = w_in[:, ATTN_WIDTH:ATTN_WIDTH + KV_WIDTH].reshape(d, ATTN_KV_HEADS, HEAD_DIM)
            wq = _relayout(wq, _PAIR_SPLIT_128).reshape(d, ATTN_WIDTH)
            wk = _relayout(wk, _PAIR_SPLIT_128).reshape(d, KV_WIDTH)
            w0 = jnp.concatenate([wq, wk, w_in[:, ATTN_WIDTH + KV_WIDTH:]], axis=-1)
            qg = jnp.concatenate(_pair_split(even_q_norm_g[i]), axis=-1).reshape(1, HEAD_DIM)
            kg = jnp.concatenate(_pair_split(even_k_norm_g[i]), axis=-1).reshape(1, HEAD_DIM)
            q, k, vt, gb, cin = _proj0(h, even_norm_g[i].reshape(1, d), w0, qg, kg,
                                       cos_a, sin_a, t["chunk"], t["kv_chunk"])
            attn = _attention(q, k, vt, group=ATTN_GROUP, bq=t["bq_gqa"], name="gqa_attn")
            h, w_up, w_dn = _out0(h, attn, gb, cin, even_conv_w[i], even_w_out[i].astype(BF16),
                                  mlp_w, layer, t["bm_out"])
        else:
            w_down = odd_w_down[i].astype(BF16)
            wd = jnp.concatenate([w_down[:, :Q_LORA + KV_LORA],
                                  _relayout(w_down[:, Q_LORA + KV_LORA:], _ROPE_PAD_64)], axis=-1)
            wuq = odd_w_uq[i].astype(BF16).reshape(Q_LORA, MLA_HEADS, MLA_NOPE + MLA_ROPE)
            wuq = _relayout(wuq, _MLA_Q_HEAD).reshape(Q_LORA, MLA_HEADS * MLA_QK_PAD)
            q, k, vt = _proj1(h, odd_norm_g[i].reshape(1, d), wd,
                              odd_q_lat_g[i].reshape(1, Q_LORA), odd_kv_lat_g[i].reshape(1, KV_LORA),
                              wuq, odd_w_ukv[i].astype(BF16), cos_c, sin_c, t["chunk"],
                              t["kv_chunk"])
            attn = _attention(q, k, vt, group=1, bq=t["bq_mla"], name="mla_attn")
            h, w_up, w_dn = _out1(h.reshape(b * s, d), attn.reshape(b * s, -1),
                                  odd_w_o[i].astype(BF16), mlp_w, layer, t["bm_out"])
        h = _mlp(h.reshape(b * s, d), mlp_norm_g[layer].reshape(1, d), w_up, w_dn, fg,
                 bm=t["bm_mlp"], bf=t["bf_mlp"],
                 final_norm=(layer == depth - 1)).reshape(b, s, d)
    return h
```

```python
import functools

import jax
import jax.numpy as jnp
from jax import lax
from jax.experimental import pallas as pl
from jax.experimental.pallas import tpu as pltpu

F32 = jnp.float32
BF16 = jnp.bfloat16

NORM_EPS = 1e-6
ROPE_THETA = 10000.0
GRID_W = 64

HEAD_DIM = 128
ATTN_HEADS = 8
ATTN_KV_HEADS = 2
ATTN_GROUP = ATTN_HEADS // ATTN_KV_HEADS
ATTN_WIDTH = ATTN_HEADS * HEAD_DIM
KV_WIDTH = ATTN_KV_HEADS * HEAD_DIM
CONV_WIDTH = 1024

MLA_HEADS = 16
MLA_NOPE = 128
MLA_ROPE = 64
MLA_V = 128
Q_LORA = 512
KV_LORA = 512
MLA_QK_PAD = 256

LANES = 128
BF16_SUBLANES = 16
VMEM_LIMIT_BYTES = 56 * 1024 * 1024
NEG_BIG = -1e30
LOG2_E = 1.4426950408889634
ATTN_COL_GROUP = 512
ATTN_TRIP_ITEMS = 128
ATTN_LOOKAHEAD = 2
ATTN_RING = 4


def _tiles(seq):
    return dict(
        chunk=min(512, seq),
        kv_chunk=min(512, seq),
        bm_mlp=min(1024, seq),
        bf_mlp=512,
        bm_out=min(512, seq),
        bq_gqa=min(1024, seq),
        bq_mla=min(4096, seq),
    )


def _params(sem):
    return pltpu.CompilerParams(dimension_semantics=sem, vmem_limit_bytes=VMEM_LIMIT_BYTES)


def _rms(x, g):
    return x * lax.rsqrt(jnp.mean(x * x, axis=-1, keepdims=True) + NORM_EPS) * g


def _store_vt(vt_ref, head, v):
    vt = v.T.astype(BF16)
    for i in range(vt_ref.shape[1]):
        vt_ref[head, i] = vt[:, i * LANES:(i + 1) * LANES]


def _rope(x, cos, sin):
    return x * cos + pltpu.roll(x, LANES // 2, 1) * sin


def _proj0_kernel(h_ref, g_ref, w_ref, qg_ref, kg_ref, cos_ref, sin_ref,
                  q_ref, k_ref, vt_ref, gb_ref, cin_ref):
    xn = _rms(h_ref[...], g_ref[...]).astype(BF16)
    y = jnp.dot(xn, w_ref[...], preferred_element_type=F32)
    cos = cos_ref[...]
    sin = sin_ref[...]
    scale = HEAD_DIM ** -0.5 * LOG2_E
    for hh in range(ATTN_HEADS):
        yh = _rms(y[:, hh * HEAD_DIM:(hh + 1) * HEAD_DIM], qg_ref[...])
        q_ref[hh] = (_rope(yh, cos, sin) * scale).astype(BF16)
    o = ATTN_WIDTH
    for hh in range(ATTN_KV_HEADS):
        yh = _rms(y[:, o + hh * HEAD_DIM:o + (hh + 1) * HEAD_DIM], kg_ref[...])
        k_ref[hh, 0] = _rope(yh, cos, sin).astype(BF16)
    o += KV_WIDTH
    for hh in range(ATTN_KV_HEADS):
        _store_vt(vt_ref, hh, y[:, o + hh * HEAD_DIM:o + (hh + 1) * HEAD_DIM])
    o += KV_WIDTH
    gb_ref[...] = y[:, o:o + CONV_WIDTH].astype(BF16)
    o += CONV_WIDTH
    cin_ref[...] = (y[:, o:o + CONV_WIDTH] * y[:, o + CONV_WIDTH:o + 2 * CONV_WIDTH]).astype(BF16)


def _proj0(h, g, w, qg, kg, cos, sin, chunk, kv_chunk):
    b, s, d = h.shape
    n = w.shape[1]
    nc = s // chunk
    r = kv_chunk // chunk
    return pl.pallas_call(
        _proj0_kernel,
        grid=(b, nc),
        in_specs=[
            pl.BlockSpec((None, chunk, d), lambda bi, i: (bi, i, 0)),
            pl.BlockSpec((1, d), lambda bi, i: (0, 0)),
            pl.BlockSpec((d, n), lambda bi, i: (0, 0)),
            pl.BlockSpec((1, HEAD_DIM), lambda bi, i: (0, 0)),
            pl.BlockSpec((1, HEAD_DIM), lambda bi, i: (0, 0)),
            pl.BlockSpec((chunk, HEAD_DIM), lambda bi, i: (i, 0)),
            pl.BlockSpec((chunk, HEAD_DIM), lambda bi, i: (i, 0)),
        ],
        out_specs=[
            pl.BlockSpec((None, ATTN_HEADS, chunk, HEAD_DIM), lambda bi, i: (bi, 0, i, 0)),
            pl.BlockSpec((None, ATTN_KV_HEADS, 1, chunk, LANES), lambda bi, i: (bi, 0, 0, i, 0)),
            pl.BlockSpec((None, ATTN_KV_HEADS, None, chunk // LANES, HEAD_DIM, LANES),
                         lambda bi, i: (bi, 0, i // r, i % r, 0, 0)),
            pl.BlockSpec((None, chunk, CONV_WIDTH), lambda bi, i: (bi, i, 0)),
            pl.BlockSpec((None, chunk, CONV_WIDTH), lambda bi, i: (bi, i, 0)),
        ],
        out_shape=[
            jax.ShapeDtypeStruct((b, ATTN_HEADS, s, HEAD_DIM), BF16),
            jax.ShapeDtypeStruct((b, ATTN_KV_HEADS, 1, s, LANES), BF16),
            jax.ShapeDtypeStruct((b, ATTN_KV_HEADS, s // kv_chunk, kv_chunk // LANES, HEAD_DIM, LANES),
                                 BF16),
            jax.ShapeDtypeStruct((b, s, CONV_WIDTH), BF16),
            jax.ShapeDtypeStruct((b, s, CONV_WIDTH), BF16),
        ],
        compiler_params=_params(("parallel", "parallel")),
        name="proj0",
    )(h, g, w, qg, kg, cos, sin)


def _proj1_kernel(h_ref, g_ref, wd_ref, qg_ref, kvg_ref, wuq_ref, wukv_ref, cos_ref, sin_ref,
                  q_ref, k_ref, vt_ref):
    xn = _rms(h_ref[...], g_ref[...]).astype(BF16)
    lat = jnp.dot(xn, wd_ref[...], preferred_element_type=F32)
    cq = _rms(lat[:, :Q_LORA], qg_ref[...]).astype(BF16)
    ckv = _rms(lat[:, Q_LORA:Q_LORA + KV_LORA], kvg_ref[...]).astype(BF16)
    cos = cos_ref[...]
    sin = sin_ref[...]
    kr = _rope(lat[:, Q_LORA + KV_LORA:], cos, sin).astype(BF16)
    q = jnp.dot(cq, wuq_ref[...], preferred_element_type=F32)
    kv = jnp.dot(ckv, wukv_ref[...], preferred_element_type=F32)
    scale = (MLA_NOPE + MLA_ROPE) ** -0.5 * LOG2_E
    for hh in range(MLA_HEADS):
        o = hh * MLA_QK_PAD
        q_ref[hh, :, :MLA_NOPE] = (q[:, o:o + MLA_NOPE] * scale).astype(BF16)
        qr = _rope(q[:, o + MLA_NOPE:o + MLA_QK_PAD], cos, sin)
        q_ref[hh, :, MLA_NOPE:] = (qr * scale).astype(BF16)
        o = hh * (MLA_NOPE + MLA_V)
        k_ref[hh, 0] = kv[:, o:o + MLA_NOPE].astype(BF16)
        k_ref[hh, 1] = kr
        _store_vt(vt_ref, hh, kv[:, o + MLA_NOPE:o + MLA_NOPE + MLA_V])


def _proj1(h, g, wd, qg, kvg, wuq, wukv, cos, sin, chunk, kv_chunk):
    b, s, d = h.shape
    nc = s // chunk
    r = kv_chunk // chunk
    const = lambda bi, i: (0, 0)
    return pl.pallas_call(
        _proj1_kernel,
        grid=(b, nc),
        in_specs=[
            pl.BlockSpec((None, chunk, d), lambda bi, i: (bi, i, 0)),
            pl.BlockSpec((1, d), const),
            pl.BlockSpec(wd.shape, const),
            pl.BlockSpec((1, Q_LORA), const),
            pl.BlockSpec((1, KV_LORA), const),
            pl.BlockSpec(wuq.shape, const),
            pl.BlockSpec(wukv.shape, const),
            pl.BlockSpec((chunk, LANES), lambda bi, i: (i, 0)),
            pl.BlockSpec((chunk, LANES), lambda bi, i: (i, 0)),
        ],
        out_specs=[
            pl.BlockSpec((None, MLA_HEADS, chunk, MLA_QK_PAD), lambda bi, i: (bi, 0, i, 0)),
            pl.BlockSpec((None, MLA_HEADS, MLA_QK_PAD // LANES, chunk, LANES),
                         lambda bi, i: (bi, 0, 0, i, 0)),
            pl.BlockSpec((None, MLA_HEADS, None, chunk // LANES, MLA_V, LANES),
                         lambda bi, i: (bi, 0, i // r, i % r, 0, 0)),
        ],
        out_shape=[
            jax.ShapeDtypeStruct((b, MLA_HEADS, s, MLA_QK_PAD), BF16),
            jax.ShapeDtypeStruct((b, MLA_HEADS, MLA_QK_PAD // LANES, s, LANES), BF16),
            jax.ShapeDtypeStruct((b, MLA_HEADS, s // kv_chunk, kv_chunk // LANES, MLA_V, LANES), BF16),
        ],
        compiler_params=_params(("parallel", "parallel")),
        name="proj1",
    )(h, g, wd, qg, kvg, wuq, wukv, cos, sin)


def _attn_kernel(q_ref, k_ref, vt_ref, o_ref, m_sc, acc_sc, s_sc, mx_sc, qt_sc, *, group, bq,
                 chunk, n_chunks, unroll):
    n = group * bq
    gw = s_sc.shape[2]
    n_groups = n // gw
    ring = s_sc.shape[0]
    m_sc[...] = jnp.full(m_sc.shape, NEG_BIG, F32)
    acc_sc[...] = jnp.zeros(acc_sc.shape, F32)
    for gi in range(n_groups):
        qg = q_ref[(gi * gw) // bq, pl.ds((gi * gw) % bq, gw), :]
        qt_sc[:, gi * gw:(gi + 1) * gw] = qg.T

    ones_rows = (lax.broadcasted_iota(jnp.int32, (BF16_SUBLANES, chunk), 0) == 0).astype(BF16)

    items = [(j, gi) for j in range(unroll) for gi in range(n_groups)]
    assert len(items) % ring == 0
    lookahead = min(ATTN_LOOKAHEAD, ring - 1)

    def scores(c, gi, slot):
        rows = pl.ds(pl.multiple_of(c * chunk, chunk), chunk)
        kc = jnp.concatenate([k_ref[i, rows, :] for i in range(k_ref.shape[0])], axis=1)
        s = jnp.dot(kc, qt_sc[:, gi * gw:(gi + 1) * gw], preferred_element_type=F32)
        s_sc[slot] = s
        mx_sc[slot] = jnp.max(s, axis=0, keepdims=True)

    for idx in range(lookahead):
        scores(items[idx][0], items[idx][1], idx)

    def body(t, carry):
        for idx, (j, gi) in enumerate(items):
            ahead = idx + lookahead
            ja, ga = items[ahead % len(items)]
            ca = jnp.minimum((t + ahead // len(items)) * unroll + ja, n_chunks - 1)
            scores(ca, ga, ahead % ring)

            cols = slice(gi * gw, (gi + 1) * gw)
            slot = idx % ring
            m_prev = m_sc[:, cols]
            m_new = jnp.maximum(m_prev, mx_sc[slot])
            alpha = jnp.exp2(m_prev - m_new)
            p = jnp.exp2(s_sc[slot] - m_new)
            vt = jnp.concatenate([vt_ref[t * unroll + j, i] for i in range(vt_ref.shape[1])], axis=1)
            vt = jnp.concatenate([vt, ones_rows], axis=0)
            pv = jnp.dot(vt, p.astype(BF16), preferred_element_type=F32)
            acc_sc[:, cols] = alpha * acc_sc[:, cols] + pv
            m_sc[:, cols] = m_new
        return carry

    lax.fori_loop(0, n_chunks // unroll, body, 0)
    dv = acc_sc.shape[0] - BF16_SUBLANES
    o = acc_sc[:dv, :] / acc_sc[dv:dv + 1, :]
    for gi in range(group):
        o_ref[:, gi * dv:(gi + 1) * dv] = o[:, gi * bq:(gi + 1) * bq].T.astype(o_ref.dtype)


def _attention(q, k, vt, *, group, bq, name):
    b, h, s, dqk = q.shape
    hkv = k.shape[1]
    n_chunks, slabs, dv, _ = vt.shape[2:]
    chunk = slabs * LANES
    n = group * bq
    gw = min(ATTN_COL_GROUP, bq)
    unroll = max(1, ATTN_TRIP_ITEMS // (n // gw))
    while n_chunks % unroll:
        unroll -= 1
    items = unroll * (n // gw)
    ring = ATTN_RING if items % ATTN_RING == 0 else items
    kern = functools.partial(_attn_kernel, group=group, bq=bq, chunk=chunk, n_chunks=n_chunks,
                             unroll=unroll)
    return pl.pallas_call(
        kern,
        grid=(b, hkv, s // bq),
        in_specs=[
            pl.BlockSpec((None, group, bq, dqk), lambda bi, hi, qi: (bi, hi, qi, 0)),
            pl.BlockSpec((None, None, dqk // LANES, s, LANES), lambda bi, hi, qi: (bi, hi, 0, 0, 0)),
            pl.BlockSpec((None, None, n_chunks, slabs, dv, LANES),
                         lambda bi, hi, qi: (bi, hi, 0, 0, 0, 0)),
        ],
        out_specs=pl.BlockSpec((None, bq, group * dv), lambda bi, hi, qi: (bi, qi, hi)),
        out_shape=jax.ShapeDtypeStruct((b, s, h * dv), BF16),
        scratch_shapes=[
            pltpu.VMEM((1, n), F32),
            pltpu.VMEM((dv + BF16_SUBLANES, n), F32),
            pltpu.VMEM((ring, chunk, gw), F32),
            pltpu.VMEM((ring, 1, gw), F32),
            pltpu.VMEM((dqk, n), BF16),
        ],
        compiler_params=_params(("parallel", "parallel", "arbitrary")),
        name=name,
    )(q, k, vt)


def _cast_weight_slices(wup_ref, wdn_ref, wup_o_ref, wdn_o_ref):
    wup_o_ref[...] = wup_ref[...].astype(BF16)
    wdn_o_ref[...] = wdn_ref[...].astype(BF16)


def _weight_cast_specs(mlp_w, layer, steps, step_index):
    in_specs, out_specs, out_shapes = [], [], []
    for w in mlp_w:
        rows, cols = w.shape[1:]
        assert rows % (steps * BF16_SUBLANES) == 0
        blk = rows // steps
        in_specs.append(pl.BlockSpec((None, blk, cols),
                                     lambda *g: (layer, step_index(*g), 0)))
        out_specs.append(pl.BlockSpec((blk, cols), lambda *g: (step_index(*g), 0)))
        out_shapes.append(jax.ShapeDtypeStruct((rows, cols), BF16))
    return in_specs, out_specs, out_shapes


def _out0_kernel(h_ref, a_ref, gb_ref, cin_ref, cprev_ref, cnext_ref, cw_ref, w_ref, wup_ref, wdn_ref,
                 o_ref, wup_o_ref, wdn_o_ref):
    _cast_weight_slices(wup_ref, wdn_ref, wup_o_ref, wdn_o_ref)
    i = pl.program_id(1)
    bm = cin_ref.shape[0]
    c = cin_ref[...].astype(F32)
    prev_row = cprev_ref[BF16_SUBLANES - 1:BF16_SUBLANES, :].astype(F32)
    next_row = cnext_ref[0:1, :].astype(F32)
    prev_row = jnp.where(i == 0, 0.0, prev_row)
    next_row = jnp.where(i == pl.num_programs(1) - 1, 0.0, next_row)
    rows = lax.broadcasted_iota(jnp.int32, (bm, 1), 0)
    c_m1 = jnp.where(rows == 0, prev_row, pltpu.roll(c, 1, 0))
    c_p1 = jnp.where(rows == bm - 1, next_row, pltpu.roll(c, bm - 1, 0))
    conv = cw_ref[0:1, :] * c_m1 + cw_ref[1:2, :] * c + cw_ref[2:3, :] * c_p1
    sconv = (gb_ref[...].astype(F32) * conv).astype(BF16)
    aw = a_ref.shape[1]
    y = jnp.dot(a_ref[...], w_ref[:aw, :], preferred_element_type=F32)
    y = y + jnp.dot(sconv, w_ref[aw:, :], preferred_element_type=F32)
    o_ref[...] = h_ref[...] + y


def _out0(h, attn, gb, cin, conv_w, w, mlp_w, layer, bm):
    b, s, d = h.shape
    nb = s // bm
    w_specs, w_out_specs, w_out_shapes = _weight_cast_specs(mlp_w, layer, b * nb,
                                                            lambda bi, i: bi * nb + i)
    r = bm // BF16_SUBLANES
    last = s // BF16_SUBLANES - 1
    tile = lambda width: pl.BlockSpec((None, bm, width), lambda bi, i: (bi, i, 0))
    return pl.pallas_call(
        _out0_kernel,
        grid=(b, nb),
        in_specs=[
            tile(d), tile(attn.shape[2]), tile(CONV_WIDTH), tile(CONV_WIDTH),
            pl.BlockSpec((None, BF16_SUBLANES, CONV_WIDTH),
                         lambda bi, i: (bi, jnp.maximum(i * r - 1, 0), 0)),
            pl.BlockSpec((None, BF16_SUBLANES, CONV_WIDTH),
                         lambda bi, i: (bi, jnp.minimum((i + 1) * r, last), 0)),
            pl.BlockSpec(conv_w.shape, lambda bi, i: (0, 0)),
            pl.BlockSpec(w.shape, lambda bi, i: (0, 0)),
        ] + w_specs,
        out_specs=[tile(d)] + w_out_specs,
        out_shape=[jax.ShapeDtypeStruct(h.shape, F32)] + w_out_shapes,
        compiler_params=_params(("parallel", "parallel")),
        name="out0",
    )(h, attn, gb, cin, cin, cin, conv_w, w, *mlp_w)


def _out1_kernel(h_ref, a_ref, w_ref, wup_ref, wdn_ref, o_ref, wup_o_ref, wdn_o_ref):
    _cast_weight_slices(wup_ref, wdn_ref, wup_o_ref, wdn_o_ref)
    o_ref[...] = h_ref[...] + jnp.dot(a_ref[...], w_ref[...], preferred_element_type=F32)


def _out1(h, attn, w, mlp_w, layer, bm):
    t, d = h.shape
    w_specs, w_out_specs, w_out_shapes = _weight_cast_specs(mlp_w, layer, t // bm, lambda i: i)
    return pl.pallas_call(
        _out1_kernel,
        grid=(t // bm,),
        in_specs=[
            pl.BlockSpec((bm, d), lambda i: (i, 0)),
            pl.BlockSpec((bm, attn.shape[1]), lambda i: (i, 0)),
            pl.BlockSpec(w.shape, lambda i: (0, 0)),
        ] + w_specs,
        out_specs=[pl.BlockSpec((bm, d), lambda i: (i, 0))] + w_out_specs,
        out_shape=[jax.ShapeDtypeStruct(h.shape, F32)] + w_out_shapes,
        compiler_params=_params(("parallel",)),
        name="out1",
    )(h, attn, w, *mlp_w)


def _mlp_kernel(h_ref, g_ref, wup_ref, wdn_ref, fg_ref, o_ref, xn_sc, *, final_norm):
    f = pl.program_id(1)

    def mlp_chunk(xn):
        u = jnp.dot(xn, wup_ref[...], preferred_element_type=F32)
        a = jnp.square(jnp.maximum(u, 0.0)).astype(BF16)
        return jnp.dot(a, wdn_ref[...], preferred_element_type=F32)

    @pl.when(f == 0)
    def _():
        x = h_ref[...]
        xn = _rms(x, g_ref[...]).astype(BF16)
        xn_sc[...] = xn
        o_ref[...] = x + mlp_chunk(xn)

    last = pl.num_programs(1) - 1

    @pl.when((f > 0) & ((f < last) | (not final_norm)))
    def _():
        o_ref[...] += mlp_chunk(xn_sc[...])

    if final_norm:
        @pl.when((f == last) & (f > 0))
        def _():
            o_ref[...] = _rms(o_ref[...] + mlp_chunk(xn_sc[...]), fg_ref[...])


def _mlp(h, g, wup, wdn, fg, *, bm, bf, final_norm):
    t, d = h.shape
    dff = wup.shape[1]
    return pl.pallas_call(
        functools.partial(_mlp_kernel, final_norm=final_norm),
        grid=(t // bm, dff // bf),
        in_specs=[
            pl.BlockSpec((bm, d), lambda i, f: (i, 0)),
            pl.BlockSpec((1, d), lambda i, f: (0, 0)),
            pl.BlockSpec((d, bf), lambda i, f: (0, f)),
            pl.BlockSpec((bf, d), lambda i, f: (f, 0)),
            pl.BlockSpec((1, d), lambda i, f: (0, 0)),
        ],
        out_specs=pl.BlockSpec((bm, d), lambda i, f: (i, 0)),
        out_shape=jax.ShapeDtypeStruct(h.shape, F32),
        scratch_shapes=[pltpu.VMEM((bm, d), BF16)],
        compiler_params=_params(("parallel", "arbitrary")),
        name="mlp_final" if final_norm else "mlp",
    )(h, g, wup, wdn, fg)


def _rope_cos_sin(seq, rot_dim):
    rows = seq // GRID_W
    axis_dim = rot_dim // 2
    inv_freq = ROPE_THETA ** (-jnp.arange(0, axis_dim, 2, dtype=F32) / axis_dim)
    ang_row = jnp.arange(rows, dtype=F32)[:, None] * inv_freq
    ang_col = jnp.arange(GRID_W, dtype=F32)[:, None] * inv_freq
    shape = (rows, GRID_W, inv_freq.shape[0])

    def per_token(fn):
        by_row = jnp.broadcast_to(fn(ang_row)[:, None, :], shape)
        by_col = jnp.broadcast_to(fn(ang_col)[None, :, :], shape)
        return jnp.concatenate([by_row, by_col], axis=-1).reshape(seq, rot_dim // 2)

    return per_token(jnp.cos), per_token(jnp.sin)


def _rope_tables(seq, rot_dim):
    cos, sin = _rope_cos_sin(seq, rot_dim)
    pad = jnp.zeros((seq, LANES // 2 - rot_dim // 2), F32)
    cos_t = jnp.concatenate([cos, pad, cos, pad], axis=-1)
    sin_t = jnp.concatenate([-sin, pad, sin, pad], axis=-1)
    return cos_t, sin_t


def _pair_split(w):
    return w[..., 0::2], w[..., 1::2]


def _pair_split_sources(rot_dim, pad):
    blank = [-1] * pad
    return list(range(0, rot_dim, 2)) + blank + list(range(1, rot_dim, 2)) + blank


_PAIR_SPLIT_128 = _pair_split_sources(HEAD_DIM, 0)
_ROPE_PAD_64 = _pair_split_sources(MLA_ROPE, LANES // 2 - MLA_ROPE // 2)
_MLA_Q_HEAD = list(range(MLA_NOPE)) + [MLA_NOPE + c if c >= 0 else -1 for c in _ROPE_PAD_64]


def _relayout(w, sources):
    sel = [[1.0 if src == k else 0.0 for src in sources] for k in range(w.shape[-1])]
    return jnp.dot(w, jnp.array(sel, BF16), preferred_element_type=BF16)


def kernel(x, even_norm_g, even_w_in, even_q_norm_g, even_k_norm_g, even_conv_w, even_w_out,
           odd_norm_g, odd_w_down, odd_q_lat_g, odd_kv_lat_g, odd_w_uq, odd_w_ukv, odd_w_o,
           mlp_norm_g, mlp_w_up, mlp_w_down, final_norm_g):
    b, s, d = x.shape
    t = _tiles(s)
    depth = mlp_norm_g.shape[0]
    cos_a, sin_a = _rope_tables(s, HEAD_DIM)
    cos_c, sin_c = _rope_tables(s, MLA_ROPE)
    fg = final_norm_g.reshape(1, d)
    mlp_w = (mlp_w_up, mlp_w_down)

    h = x
    for layer in range(depth):
        i = layer // 2
        if layer % 2 == 0:
            w_in = even_w_in[i].astype(BF16)
            wq = w_in[:, :ATTN_WIDTH].reshape(d, ATTN_HEADS, HEAD_DIM)
            wk = w_in[:, ATTN_WIDTH:ATTN_WIDTH + KV_WIDTH].reshape(d, ATTN_KV_HEADS, HEAD_DIM)
            wq = _relayout(wq, _PAIR_SPLIT_128).reshape(d, ATTN_WIDTH)
            wk = _relayout(wk, _PAIR_SPLIT_128).reshape(d, KV_WIDTH)
            w0 = jnp.concatenate([wq, wk, w_in[:, ATTN_WIDTH + KV_WIDTH:]], axis=-1)
            qg = jnp.concatenate(_pair_split(even_q_norm_g[i]), axis=-1).reshape(1, HEAD_DIM)
            kg = jnp.concatenate(_pair_split(even_k_norm_g[i]), axis=-1).reshape(1, HEAD_DIM)
            q, k, vt, gb, cin = _proj0(h, even_norm_g[i].reshape(1, d), w0, qg, kg,
                                       cos_a, sin_a, t["chunk"], t["kv_chunk"])
            attn = _attention(q, k, vt, group=ATTN_GROUP, bq=t["bq_gqa"], name="gqa_attn")
            h, w_up, w_dn = _out0(h, attn, gb, cin, even_conv_w[i], even_w_out[i].astype(BF16),
                                  mlp_w, layer, t["bm_out"])
        else:
            w_down = odd_w_down[i].astype(BF16)
            wd = jnp.concatenate([w_down[:, :Q_LORA + KV_LORA],
                                  _relayout(w_down[:, Q_LORA + KV_LORA:], _ROPE_PAD_64)], axis=-1)
            wuq = odd_w_uq[i].astype(BF16).reshape(Q_LORA, MLA_HEADS, MLA_NOPE + MLA_ROPE)
            wuq = _relayout(wuq, _MLA_Q_HEAD).reshape(Q_LORA, MLA_HEADS * MLA_QK_PAD)
            q, k, vt = _proj1(h, odd_norm_g[i].reshape(1, d), wd,
                              odd_q_lat_g[i].reshape(1, Q_LORA), odd_kv_lat_g[i].reshape(1, KV_LORA),
                              wuq, odd_w_ukv[i].astype(BF16), cos_c, sin_c, t["chunk"],
                              t["kv_chunk"])
            attn = _attention(q, k, vt, group=1, bq=t["bq_mla"], name="mla_attn")
            h, w_up, w_dn = _out1(h.reshape(b * s, d), attn.reshape(b * s, -1),
                                  odd_w_o[i].astype(BF16), mlp_w, layer, t["bm_out"])
        h = _mlp(h.reshape(b * s, d), mlp_norm_g[layer].reshape(1, d), w_up, w_dn, fg,
                 bm=t["bm_mlp"], bf=t["bf_mlp"],
                 final_norm=(layer == depth - 1)).reshape(b, s, d)
    return h
```
